```python
import jax, jax.numpy as jnp
from jax import lax
import numpy as np


D_MODEL = 1024
BATCH = 8
SEQ = 2048
DEPTH = 2

GRID_W = 64
ROPE_THETA = 10000.0
EPS = 1e-6
Q_BLOCK = 128
A_HEADS = 8
A_KV_HEADS = 2
A_HEAD_DIM = 64
B_HEADS = 8
B_NOPE_DIM = 64
B_ROPE_DIM = 32
B_V_DIM = 64
B_Q_RANK = 256
B_KV_RANK = 256
C_HEADS = 4
C_DK = 128
C_DV = 128
C_GATE_RANK = 16
C_TAU = 16.0
C_CHUNK = 64
N_BRANCH = 3
D_FF = 2816
N_EXPERTS = 8
TOP_K = 2
D_FF_EXPERT = 3584
N_DENSE = (DEPTH + 1) // 2
N_MOE = DEPTH // 2
IN_SIZES = (A_HEADS * A_HEAD_DIM, A_KV_HEADS * A_HEAD_DIM, A_KV_HEADS * A_HEAD_DIM,
            B_Q_RANK, B_KV_RANK, B_ROPE_DIM,
            C_HEADS * C_DK, C_HEADS * C_DK, C_HEADS * C_DV, C_HEADS * C_DV, C_GATE_RANK, C_GATE_RANK,
            N_BRANCH * D_MODEL)
IN_WIDTH = sum(IN_SIZES)

kernel_name = 'hybrid_gated_gqa_mla_gla_moe_encoder'


def rmsnorm(x, g):
    xf = x.astype(jnp.float32)
    y = xf * lax.rsqrt(jnp.mean(xf * xf, axis=-1, keepdims=True) + EPS)
    return (y * g.astype(jnp.float32)).astype(x.dtype)


def axial_angles(T, rot_dim):
    rows = T // GRID_W
    row = jnp.repeat(jnp.arange(rows, dtype=jnp.float32), GRID_W)
    col = jnp.tile(jnp.arange(GRID_W, dtype=jnp.float32), rows)
    n_freq = rot_dim // 4
    inv_freq = ROPE_THETA ** (-jnp.arange(n_freq, dtype=jnp.float32) / n_freq)
    return row[:, None] * inv_freq, col[:, None] * inv_freq


def rope_1d(x, ang):
    x1, x2 = jnp.split(x, 2, axis=-1)
    c, s = jnp.cos(ang), jnp.sin(ang)
    return jnp.concatenate([x1 * c - x2 * s, x2 * c + x1 * s], axis=-1)


def axial_rope(x, ang_row, ang_col):
    xf = x.astype(jnp.float32)
    half = x.shape[-1] // 2
    y = jnp.concatenate([rope_1d(xf[..., :half], ang_row), rope_1d(xf[..., half:], ang_col)], axis=-1)
    return y.astype(x.dtype)


def heads(t, n):
    Bsz, T, w = t.shape
    return t.reshape(Bsz, T, n, w // n).transpose(0, 2, 1, 3)


def merge_heads(t):
    Bsz, n, T, d = t.shape
    return t.transpose(0, 2, 1, 3).reshape(Bsz, T, n * d)


def blocked_attention(q, k, v, scale):
    Bsz, Hk, G, T, dq = q.shape
    n_blk = T // Q_BLOCK
    qb = jnp.moveaxis(q.reshape(Bsz, Hk, G, n_blk, Q_BLOCK, dq), 3, 0)

    def one_block(q_blk):
        s = jnp.einsum('bhgqd,bhkd->bhgqk', q_blk, k).astype(jnp.float32) * scale
        p = jax.nn.softmax(s, axis=-1).astype(v.dtype)
        return jnp.einsum('bhgqk,bhkv->bhgqv', p, v)

    out = lax.map(one_block, qb)
    return jnp.moveaxis(out, 0, 3).reshape(Bsz, Hk, G, T, v.shape[-1])


def gla_chunked(q, k, v, log_a):
    Bsz, H, T, dk = q.shape
    dv = v.shape[-1]
    n = T // C_CHUNK
    shp = (Bsz, H, n, C_CHUNK)
    q = q.reshape(shp + (dk,))
    k = k.reshape(shp + (dk,))
    v = v.reshape(shp + (dv,))
    b = jnp.cumsum(log_a.reshape(shp + (dk,)), axis=3)
    q_t = q * jnp.exp(b)
    k_t = k * jnp.exp(-b)
    causal = jnp.tril(jnp.ones((C_CHUNK, C_CHUNK), dtype=bool))
    scores = jnp.where(causal, jnp.einsum('bhncd,bhnsd->bhncs', q_t, k_t), 0.0)
    intra = jnp.einsum('bhncs,bhnsv->bhncv', scores, v)
    b_last = b[:, :, :, -1:, :]
    chunk_kv = jnp.einsum('bhncd,bhncv->bhndv', k * jnp.exp(b_last - b), v)
    chunk_decay = jnp.exp(b_last[:, :, :, 0, :])

    def step(state, inp):
        dec, kv = inp
        return dec[..., None] * state + kv, state

    init = jnp.zeros((Bsz, H, dk, dv), q.dtype)
    _, prev = lax.scan(step, init, (jnp.moveaxis(chunk_decay, 2, 0), jnp.moveaxis(chunk_kv, 2, 0)))
    prev = jnp.moveaxis(prev, 0, 2)
    inter = jnp.einsum('bhncd,bhndv->bhncv', q_t, prev)
    return (intra + inter).reshape(Bsz, H, T, dv)


def bidir_gla(q, k, v, log_a_fwd, log_a_bwd):
    flip = lambda t: jnp.flip(t, axis=2)
    fwd = gla_chunked(q, k, v, log_a_fwd)
    bwd = flip(gla_chunked(flip(q), flip(k), flip(v), flip(log_a_bwd)))
    return fwd + bwd


def token_mixer(h, w_in, b_gate, g_a_q, g_a_k, g_b_q, w_b_q_up, g_b_kv, w_b_kv_up,
                w_c_af_up, b_c_af, w_c_ab_up, b_c_ab, g_c_out, w_pa, w_pb, w_pc, w_out, ang_a, ang_b):
    Bsz, T, _ = h.shape
    f32 = jnp.float32
    proj = h @ w_in
    offs = []
    acc = 0
    for s in IN_SIZES[:-1]:
        acc += s
        offs.append(acc)
    (a_q, a_k, a_v, b_cq, b_ckv, b_kr, c_q, c_k, c_v, c_g, c_af, c_ab, gates) = jnp.split(proj, offs, axis=-1)

    qa = axial_rope(rmsnorm(heads(a_q, A_HEADS), g_a_q), *ang_a)
    ka = axial_rope(rmsnorm(heads(a_k, A_KV_HEADS), g_a_k), *ang_a)
    va = heads(a_v, A_KV_HEADS)
    qa = qa.reshape(Bsz, A_KV_HEADS, A_HEADS // A_KV_HEADS, T, A_HEAD_DIM)
    ya = blocked_attention(qa, ka, va, A_HEAD_DIM ** -0.5).reshape(Bsz, A_HEADS, T, A_HEAD_DIM)
    ya = merge_heads(ya)

    qb = heads(rmsnorm(b_cq, g_b_q) @ w_b_q_up, B_HEADS)
    q_nope, q_rope = qb[..., :B_NOPE_DIM], qb[..., B_NOPE_DIM:]
    kvb = heads(rmsnorm(b_ckv, g_b_kv) @ w_b_kv_up, B_HEADS)
    k_nope, vb = kvb[..., :B_NOPE_DIM], kvb[..., B_NOPE_DIM:]
    k_rope = axial_rope(b_kr[:, None], *ang_b)
    qb = jnp.concatenate([q_nope, axial_rope(q_rope, *ang_b)], axis=-1)
    kb = jnp.concatenate([k_nope, jnp.broadcast_to(k_rope, k_nope.shape[:-1] + (B_ROPE_DIM,))], axis=-1)
    yb = blocked_attention(qb[:, :, None], kb, vb, (B_NOPE_DIM + B_ROPE_DIM) ** -0.5)[:, :, 0]
    yb = merge_heads(yb)

    qc = heads(c_q, C_HEADS).astype(f32) * (C_DK ** -0.5)
    kc = heads(c_k, C_HEADS).astype(f32)
    vc = heads(c_v, C_HEADS).astype(f32)
    la_f = heads(jax.nn.log_sigmoid((c_af @ w_c_af_up + b_c_af).astype(f32)) / C_TAU, C_HEADS)
    la_b = heads(jax.nn.log_sigmoid((c_ab @ w_c_ab_up + b_c_ab).astype(f32)) / C_TAU, C_HEADS)
    oc = bidir_gla(qc, kc, vc, la_f, la_b)
    yc = merge_heads(rmsnorm(oc, g_c_out)).astype(h.dtype) * jax.nn.silu(c_g)

    gb = jax.nn.sigmoid(gates + b_gate).reshape(Bsz, T, N_BRANCH, D_MODEL)
    m = gb[:, :, 0] * (ya @ w_pa) + gb[:, :, 1] * (yb @ w_pb) + gb[:, :, 2] * (yc @ w_pc)
    return m @ w_out


def swiglu(h, wg, wu, wd):
    return (jax.nn.silu(h @ wg) * (h @ wu)) @ wd


def moe_ffn(h, w_router, w_e_gate, w_e_up, w_e_down):
    logits = jnp.einsum('btd,de->bte', h.astype(jnp.float32), w_router.astype(jnp.float32))
    top_v, top_i = lax.top_k(logits, TOP_K)
    top_w = jax.nn.softmax(top_v, axis=-1)
    gate = jnp.einsum('btk,btke->bte', top_w, jax.nn.one_hot(top_i, N_EXPERTS, dtype=jnp.float32)).astype(h.dtype)
    out = jnp.zeros_like(h)
    for e in range(N_EXPERTS):
        out = out + gate[..., e:e + 1] * swiglu(h, w_e_gate[e], w_e_up[e], w_e_down[e])
    return out


def setup_inputs(seed: int = 0) -> dict:
    key = jax.random.key(seed)
    keys = jax.random.split(key, 28)

    def nrm(i, shape, scale):
        return scale * jax.random.normal(keys[i], shape, dtype=jnp.float32)

    def gain(i, shape):
        return 1.0 + nrm(i, shape, 0.01)

    return {
        'x': nrm(0, (BATCH, SEQ, D_MODEL), 1.0),
        'w_in': nrm(1, (DEPTH, D_MODEL, IN_WIDTH), D_MODEL ** -0.5),
        'b_gate': nrm(2, (DEPTH, N_BRANCH * D_MODEL), 0.02),
        'g_mix': gain(3, (DEPTH, D_MODEL)),
        'g_a_q': gain(4, (DEPTH, A_HEAD_DIM)),
        'g_a_k': gain(5, (DEPTH, A_HEAD_DIM)),
        'g_b_q': gain(6, (DEPTH, B_Q_RANK)),
        'w_b_q_up': nrm(7, (DEPTH, B_Q_RANK, B_HEADS * (B_NOPE_DIM + B_ROPE_DIM)), B_Q_RANK ** -0.5),
        'g_b_kv': gain(8, (DEPTH, B_KV_RANK)),
        'w_b_kv_up': nrm(9, (DEPTH, B_KV_RANK, B_HEADS * (B_NOPE_DIM + B_V_DIM)), B_KV_RANK ** -0.5),
        'w_c_af_up': nrm(10, (DEPTH, C_GATE_RANK, C_HEADS * C_DK), C_GATE_RANK ** -0.5),
        'b_c_af': nrm(11, (DEPTH, C_HEADS * C_DK), 0.1),
        'w_c_ab_up': nrm(12, (DEPTH, C_GATE_RANK, C_HEADS * C_DK), C_GATE_RANK ** -0.5),
        'b_c_ab': nrm(13, (DEPTH, C_HEADS * C_DK), 0.1),
        'g_c_out': gain(14, (DEPTH, C_DV)),
        'w_pa': nrm(15, (DEPTH, A_HEADS * A_HEAD_DIM, D_MODEL), (A_HEADS * A_HEAD_DIM) ** -0.5),
        'w_pb': nrm(16, (DEPTH, B_HEADS * B_V_DIM, D_MODEL), (B_HEADS * B_V_DIM) ** -0.5),
        'w_pc': nrm(17, (DEPTH, C_HEADS * C_DV, D_MODEL), (C_HEADS * C_DV) ** -0.5),
        'w_out': nrm(18, (DEPTH, D_MODEL, D_MODEL), D_MODEL ** -0.5),
        'g_ffn': gain(19, (DEPTH, D_MODEL)),
        'w_ff_gate': nrm(20, (N_DENSE, D_MODEL, D_FF), D_MODEL ** -0.5),
        'w_ff_up': nrm(21, (N_DENSE, D_MODEL, D_FF), D_MODEL ** -0.5),
        'w_ff_down': nrm(22, (N_DENSE, D_FF, D_MODEL), D_FF ** -0.5),
        'w_router': nrm(23, (N_MOE, D_MODEL, N_EXPERTS), D_MODEL ** -0.5),
        'w_e_gate': nrm(24, (N_MOE, N_EXPERTS, D_MODEL, D_FF_EXPERT), D_MODEL ** -0.5),
        'w_e_up': nrm(25, (N_MOE, N_EXPERTS, D_MODEL, D_FF_EXPERT), D_MODEL ** -0.5),
        'w_e_down': nrm(26, (N_MOE, N_EXPERTS, D_FF_EXPERT, D_MODEL), D_FF_EXPERT ** -0.5),
        'g_final': gain(27, (D_MODEL,)),
    }


def reference(x, w_in, b_gate, g_mix, g_a_q, g_a_k, g_b_q, w_b_q_up, g_b_kv, w_b_kv_up,
              w_c_af_up, b_c_af, w_c_ab_up, b_c_ab, g_c_out, w_pa, w_pb, w_pc, w_out, g_ffn,
              w_ff_gate, w_ff_up, w_ff_down, w_router, w_e_gate, w_e_up, w_e_down, g_final):
    T = x.shape[1]
    ang_a = axial_angles(T, A_HEAD_DIM)
    ang_b = axial_angles(T, B_ROPE_DIM)
    for i in range(DEPTH):
        h = rmsnorm(x, g_mix[i])
        x = x + token_mixer(h, w_in[i], b_gate[i], g_a_q[i], g_a_k[i], g_b_q[i], w_b_q_up[i],
                            g_b_kv[i], w_b_kv_up[i], w_c_af_up[i], b_c_af[i], w_c_ab_up[i], b_c_ab[i],
                            g_c_out[i], w_pa[i], w_pb[i], w_pc[i], w_out[i], ang_a, ang_b)
        h = rmsnorm(x, g_ffn[i])
        j = i // 2
        if i % 2 == 0:
            x = x + swiglu(h, w_ff_gate[j], w_ff_up[j], w_ff_down[j])
        else:
            x = x + moe_ffn(h, w_router[j], w_e_gate[j], w_e_up[j], w_e_down[j])
    return rmsnorm(x, g_final)
```

```python
import functools

import jax
import jax.numpy as jnp
import numpy as np
from jax import lax
from jax.experimental import pallas as pl
from jax.experimental.pallas import tpu as pltpu

F32 = jnp.float32
BF16 = jnp.bfloat16

D_MODEL = 1024
GRID_W = 64
ROPE_THETA = 10000.0
EPS = 1e-6
A_HEADS, A_KV_HEADS, A_HEAD_DIM = 8, 2, 64
B_HEADS, B_NOPE_DIM, B_ROPE_DIM, B_V_DIM = 8, 64, 32, 64
B_Q_RANK = B_KV_RANK = 256
C_HEADS, C_DK, C_DV, C_GATE_RANK, C_TAU, C_CHUNK = 4, 128, 128, 16, 16.0, 64
N_BRANCH = 3
D_FF = 2816
N_EXPERTS, TOP_K, D_FF_EXPERT = 8, 2, 3584

LANES = 128
VMEM_LIMIT = 56 * 1024 * 1024

P_AQ, P_AK, P_AV, P_BC, P_KR, P_C, P_DEC, P_GATE = 0, 512, 768, 1024, 1536, 1664, 3712, 3840
P_WIDTH = P_GATE + N_BRANCH * D_MODEL


def _cparams(sem):
    return pltpu.CompilerParams(dimension_semantics=sem, vmem_limit_bytes=VMEM_LIMIT)


def _const_spec(shape):
    nd = len(shape)
    return pl.BlockSpec(shape, lambda *_: (0,) * nd, pipeline_mode=pl.Buffered(1))


def _rms(xf, g):
    return xf * lax.rsqrt(jnp.mean(xf * xf, axis=-1, keepdims=True) + EPS) * g


def _sigmoid(x):
    return 1.0 / (1.0 + jnp.exp(-x))


def _split_bf16(x):
    hi = x.astype(BF16)
    lo = (x - hi.astype(F32)).astype(BF16)
    return hi, lo


def _dot(a, b):
    return jnp.dot(a, b, preferred_element_type=F32)


def _dot_nt(a, b):
    return lax.dot_general(a, b, (((1,), (1,)), ((), ())), preferred_element_type=F32)


def _dot_tn(a, b):
    return lax.dot_general(a, b, (((0,), (0,)), ((), ())), preferred_element_type=F32)


def _inproj_kernel(x_ref, g_ref, w_ref, aq_ref, ak_ref, av_ref, bc_ref, kr_ref, c_ref, dec_ref, gate_ref):
    h = _rms(x_ref[...], g_ref[...]).astype(BF16)

    def proj(o_ref, off, width, chunk=512):
        for c0 in range(0, width, chunk):
            cw = min(chunk, width - c0)
            o_ref[:, c0:c0 + cw] = _dot(h, w_ref[:, off + c0:off + c0 + cw]).astype(o_ref.dtype)

    proj(aq_ref, P_AQ, 512)
    proj(ak_ref, P_AK, 256)
    proj(av_ref, P_AV, 256)
    proj(bc_ref, P_BC, 512)
    proj(kr_ref, P_KR, 128)
    proj(c_ref, P_C, 2048)
    proj(dec_ref, P_DEC, 128)
    proj(gate_ref, P_GATE, N_BRANCH * D_MODEL)


def _inproj(x2, g, w_packed, tm):
    n = x2.shape[0]
    widths = (512, 256, 256, 512, 128, 2048, 128, N_BRANCH * D_MODEL)
    dtypes = (BF16, BF16, BF16, BF16, BF16, BF16, F32, BF16)
    return pl.pallas_call(
        _inproj_kernel,
        grid=(n // tm,),
        in_specs=[pl.BlockSpec((tm, D_MODEL), lambda i: (i, 0)),
                  _const_spec((1, D_MODEL)),
                  _const_spec((D_MODEL, P_WIDTH))],
        out_specs=[pl.BlockSpec((tm, w), lambda i: (i, 0)) for w in widths],
        out_shape=[jax.ShapeDtypeStruct((n, w), dt) for w, dt in zip(widths, dtypes)],
        compiler_params=_cparams(("parallel",)),
        name="inproj",
    )(x2, g, w_packed)


def _rope_group(x, cos, sin_signed, lane, half):
    fwd = pltpu.roll(x, LANES - half, axis=1)
    bwd = pltpu.roll(x, half, axis=1)
    swapped = jnp.where((lane % (2 * half)) < half, fwd, bwd)
    return x * cos + swapped * sin_signed


def _seg64_meansq(x, lane):
    sq = x * x
    lo = jnp.sum(jnp.where(lane < 64, sq, 0.0), axis=-1, keepdims=True)
    hi = jnp.sum(jnp.where(lane < 64, 0.0, sq), axis=-1, keepdims=True)
    return jnp.where(lane < 64, lo, hi) * (1.0 / 64.0)


def _prep_a_kernel(aq_ref, ak_ref, gq_ref, gk_ref, cos_ref, sin_ref, q_ref, k_ref):
    rows = aq_ref.shape[0]
    lane = lax.broadcasted_iota(jnp.int32, (rows, LANES), 1)
    cos, sin = cos_ref[...], sin_ref[...]
    gq, gk = gq_ref[...], gk_ref[...]
    scale = A_HEAD_DIM ** -0.5
    for gi in range(4):
        sl = slice(gi * LANES, (gi + 1) * LANES)
        x = aq_ref[:, sl].astype(F32)
        y = x * lax.rsqrt(_seg64_meansq(x, lane) + EPS) * gq
        q_ref[:, sl] = (_rope_group(y, cos, sin, lane, 16) * scale).astype(q_ref.dtype)
    for gi in range(2):
        sl = slice(gi * LANES, (gi + 1) * LANES)
        x = ak_ref[:, sl].astype(F32)
        y = x * lax.rsqrt(_seg64_meansq(x, lane) + EPS) * gk
        k_ref[:, sl] = _rope_group(y, cos, sin, lane, 16).astype(k_ref.dtype)


def _prep_a(aq, ak2, gq, gk, cos, sin, tm, t_len):
    n = aq.shape[0]
    nt = t_len // tm
    return pl.pallas_call(
        _prep_a_kernel,
        grid=(n // tm,),
        in_specs=[pl.BlockSpec((tm, 512), lambda i: (i, 0)),
                  pl.BlockSpec((tm, 256), lambda i: (i, 0)),
                  _const_spec((1, LANES)), _const_spec((1, LANES)),
                  pl.BlockSpec((tm, LANES), lambda i: (i % nt, 0)),
                  pl.BlockSpec((tm, LANES), lambda i: (i % nt, 0))],
        out_specs=[pl.BlockSpec((tm, 512), lambda i: (i, 0)),
                   pl.BlockSpec((tm, 256), lambda i: (i, 0))],
        out_shape=[jax.ShapeDtypeStruct((n, 512), BF16), jax.ShapeDtypeStruct((n, 256), BF16)],
        compiler_params=_cparams(("parallel",)),
        name="prep_a",
    )(aq, ak2, gq, gk, cos, sin)


def _prep_b_kernel(bc_ref, kr_ref, gq_ref, gkv_ref, wq_ref, wk_ref, wv_ref, e_ref, cos_ref, sin_ref,
                   q_ref, k_ref, v_ref):
    rows = bc_ref.shape[0]
    lane = lax.broadcasted_iota(jnp.int32, (rows, LANES), 1)
    cos, sin = cos_ref[...], sin_ref[...]
    cq = _rms(bc_ref[:, :B_Q_RANK].astype(F32), gq_ref[...]).astype(BF16)
    ckv = _rms(bc_ref[:, B_Q_RANK:].astype(F32), gkv_ref[...]).astype(BF16)
    kr = kr_ref[...]
    scale = (B_NOPE_DIM + B_ROPE_DIM) ** -0.5
    for h in range(B_HEADS):
        sl = slice(h * LANES, (h + 1) * LANES)
        q = _dot(cq, wq_ref[:, sl])
        q_ref[:, sl] = (_rope_group(q, cos, sin, lane, 8) * scale).astype(q_ref.dtype)
        k = _dot(ckv, wk_ref[:, sl]) + _dot(kr, e_ref[...])
        k_ref[:, sl] = _rope_group(k, cos, sin, lane, 8).astype(k_ref.dtype)
    v_ref[...] = _dot(ckv, wv_ref[...]).astype(v_ref.dtype)


def _prep_b(bc, kr, gq, gkv, wq, wk, wv, e_mat, cos, sin, tm, t_len):
    n = bc.shape[0]
    nt = t_len // tm
    return pl.pallas_call(
        _prep_b_kernel,
        grid=(n // tm,),
        in_specs=[pl.BlockSpec((tm, 512), lambda i: (i, 0)),
                  pl.BlockSpec((tm, LANES), lambda i: (i, 0)),
                  _const_spec((1, B_Q_RANK)), _const_spec((1, B_KV_RANK)),
                  _const_spec((B_Q_RANK, B_HEADS * LANES)),
                  _const_spec((B_KV_RANK, B_HEADS * LANES)),
                  _const_spec((B_KV_RANK, B_HEADS * B_V_DIM)),
                  _const_spec((LANES, LANES)),
                  pl.BlockSpec((tm, LANES), lambda i: (i % nt, 0)),
                  pl.BlockSpec((tm, LANES), lambda i: (i % nt, 0))],
        out_specs=[pl.BlockSpec((tm, 1024), lambda i: (i, 0)),
                   pl.BlockSpec((tm, 1024), lambda i: (i, 0)),
                   pl.BlockSpec((tm, 512), lambda i: (i, 0))],
        out_shape=[jax.ShapeDtypeStruct((n, 1024), BF16), jax.ShapeDtypeStruct((n, 1024), BF16),
                   jax.ShapeDtypeStruct((n, 512), BF16)],
        compiler_params=_cparams(("parallel",)),
        name="prep_b",
    )(bc, kr, gq, gkv, wq, wk, wv, e_mat, cos, sin)


def _softmax_pv(s, v3):
    m = jnp.max(s, axis=-1, keepdims=True)
    p = jnp.exp(s - m).astype(BF16)
    r = _dot(p, v3)
    return r[:, :LANES] / r[:, LANES:]


def _attn_a_kernel(q_ref, k_ref, v_ref, o_ref, v3_ref):
    tq = q_ref.shape[1]

    @pl.when(pl.program_id(2) == 0)
    def _():
        v3_ref[:, :LANES] = v_ref[0]
        v3_ref[:, LANES:] = jnp.ones((v_ref.shape[1], LANES), BF16)

    lane = lax.broadcasted_iota(jnp.int32, (tq, LANES), 1)
    first = lane < 64
    parts = []
    for gi in range(2):
        qp = q_ref[0, :, gi * LANES:(gi + 1) * LANES].astype(F32)
        parts.append(jnp.where(first, qp, 0.0).astype(BF16))
        parts.append(jnp.where(first, 0.0, qp).astype(BF16))
    qs = jnp.concatenate(parts, axis=0)
    s = _dot_nt(qs, k_ref[0])
    o = _softmax_pv(s, v3_ref[...])
    for gi in range(2):
        o0 = o[(2 * gi) * tq:(2 * gi + 1) * tq]
        o1 = o[(2 * gi + 1) * tq:(2 * gi + 2) * tq]
        o_ref[0, :, gi * LANES:(gi + 1) * LANES] = jnp.where(first, o0, o1).astype(o_ref.dtype)


def _attn_a(q, k2, v2, bsz, t_len, tq):
    q3, k3, v3 = (a.reshape(bsz, t_len, a.shape[-1]) for a in (q, k2, v2))
    out = pl.pallas_call(
        _attn_a_kernel,
        grid=(bsz, A_KV_HEADS, t_len // tq),
        in_specs=[pl.BlockSpec((1, tq, 256), lambda b, g, i: (b, i, g)),
                  pl.BlockSpec((1, t_len, LANES), lambda b, g, i: (b, 0, g)),
                  pl.BlockSpec((1, t_len, LANES), lambda b, g, i: (b, 0, g))],
        out_specs=pl.BlockSpec((1, tq, 256), lambda b, g, i: (b, i, g)),
        out_shape=jax.ShapeDtypeStruct((bsz, t_len, 512), BF16),
        scratch_shapes=[pltpu.VMEM((t_len, 2 * LANES), BF16)],
        compiler_params=_cparams(("parallel", "parallel", "arbitrary")),
        name="attn_a",
    )(q3, k3, v3)
    return out.reshape(bsz * t_len, 512)


def _attn_b_kernel(q_ref, k_ref, v_ref, o_ref, v3_ref):
    tq = q_ref.shape[1]

    @pl.when(pl.program_id(2) == 0)
    def _():
        v3_ref[:, :LANES] = v_ref[0]
        v3_ref[:, LANES:] = jnp.ones((v_ref.shape[1], LANES), BF16)

    lane = lax.broadcasted_iota(jnp.int32, (tq, LANES), 1)
    outs = []
    for j in range(2):
        sl = slice(j * LANES, (j + 1) * LANES)
        s = _dot_nt(q_ref[0, :, sl], k_ref[0, :, sl])
        outs.append(_softmax_pv(s, v3_ref[...]))
    o_ref[0] = jnp.where(lane < 64, outs[0], outs[1]).astype(o_ref.dtype)


def _attn_b(q, k, v, bsz, t_len, tq):
    q3, k3, v3 = (a.reshape(bsz, t_len, a.shape[-1]) for a in (q, k, v))
    out = pl.pallas_call(
        _attn_b_kernel,
        grid=(bsz, B_HEADS // 2, t_len // tq),
        in_specs=[pl.BlockSpec((1, tq, 256), lambda b, p, i: (b, i, p)),
                  pl.BlockSpec((1, t_len, 256), lambda b, p, i: (b, 0, p)),
                  pl.BlockSpec((1, t_len, LANES), lambda b, p, i: (b, 0, p))],
        out_specs=pl.BlockSpec((1, tq, LANES), lambda b, p, i: (b, i, p)),
        out_shape=jax.ShapeDtypeStruct((bsz, t_len, 512), BF16),
        scratch_shapes=[pltpu.VMEM((t_len, 2 * LANES), BF16)],
        compiler_params=_cparams(("parallel", "parallel", "arbitrary")),
        name="attn_b",
    )(q3, k3, v3)
    return out.reshape(bsz * t_len, 512)


def _log_sigmoid(z):
    return jnp.minimum(z, 0.0) - jnp.log(1.0 + jnp.exp(-jnp.abs(z)))


def _dot3(a, b):
    ah, al = _split_bf16(a)
    bh, bl = _split_bf16(b)
    return _dot(ah, bh) + (_dot(ah, bl) + _dot(al, bh))


def _gla_kernel(q_ref, k_ref, v_ref, g_ref, dec_ref, wf_ref, wb_ref, bf_ref, bb_ref, gco_ref,
                o_ref, acc_ref, st_ref, *, tile):
    t_len = q_ref.shape[1]
    n_tiles = t_len // tile
    n_chunks = tile // C_CHUNK
    row = lax.broadcasted_iota(jnp.int32, (tile, tile), 0)
    col = lax.broadcasted_iota(jnp.int32, (tile, tile), 1)
    same = (row // C_CHUNK) == (col // C_CHUNK)
    chunk_ones = jnp.where(same, 1.0, 0.0).astype(BF16)
    lower = jnp.where(same & (col <= row), 1.0, 0.0).astype(BF16)
    upper = jnp.where(same & (col >= row), 1.0, 0.0).astype(BF16)
    q_scale = C_DK ** -0.5

    def one_direction(forward):
        st_ref[...] = jnp.zeros_like(st_ref)
        w_ref, b_ref = (wf_ref, bf_ref) if forward else (wb_ref, bb_ref)
        cum, keep = (lower, same & (col <= row)) if forward else (upper, same & (col >= row))

        def tile_body(it, carry):
            j = it if forward else n_tiles - 1 - it
            rows = pl.ds(pl.multiple_of(j * tile, tile), tile)
            q = q_ref[0, rows, :].astype(F32) * q_scale
            k = k_ref[0, rows, :].astype(F32)
            vb = v_ref[0, rows, :]
            la = _log_sigmoid(_dot3(dec_ref[0, rows, :], w_ref[...]) + b_ref[...]) * (1.0 / C_TAU)
            la_hi, la_lo = _split_bf16(la)
            b = _dot(cum, la_hi) + _dot(cum, la_lo)
            tot = _dot(chunk_ones, la_hi) + _dot(chunk_ones, la_lo)
            qt = (q * jnp.exp(b)).astype(BF16)
            kt = (k * jnp.exp(-b)).astype(BF16)
            k2 = (k * jnp.exp(tot - b)).astype(BF16)
            scores = jnp.where(keep, _dot_nt(qt, kt), 0.0).astype(BF16)
            intra = _dot(scores, vb)
            dec = jnp.exp(tot)
            order = range(n_chunks) if forward else range(n_chunks - 1, -1, -1)
            for c in order:
                cs = slice(c * C_CHUNK, (c + 1) * C_CHUNK)
                st = st_ref[...]
                inter = _dot_nt(qt[cs], st.astype(BF16))
                out_rows = pl.ds(pl.multiple_of(j * tile + c * C_CHUNK, C_CHUNK), C_CHUNK)
                if forward:
                    acc_ref[out_rows, :] = intra[cs] + inter
                else:
                    acc_ref[out_rows, :] += intra[cs] + inter
                st_ref[...] = st * dec[c * C_CHUNK:c * C_CHUNK + 1, :] + _dot_tn(vb[cs], k2[cs])
            return carry

        lax.fori_loop(0, n_tiles, tile_body, 0)

    one_direction(True)
    one_direction(False)
    oc = acc_ref[...]
    y = _rms(oc, gco_ref[...])
    gate = g_ref[0].astype(F32)
    o_ref[0] = (y * (gate * _sigmoid(gate))).astype(o_ref.dtype)


def _gla(cqkvg, dec, wf, wb, bf, bb, gco, bsz, t_len, tile):
    c3 = cqkvg.reshape(bsz, t_len, 4 * C_HEADS * LANES)
    d3 = dec.reshape(bsz, t_len, LANES)

    def cspec(j):
        return pl.BlockSpec((1, t_len, LANES), lambda b, h: (b, 0, j * C_HEADS + h))

    out = pl.pallas_call(
        functools.partial(_gla_kernel, tile=tile),
        grid=(bsz, C_HEADS),
        in_specs=[cspec(0), cspec(1), cspec(2), cspec(3),
                  pl.BlockSpec((1, t_len, LANES), lambda b, h: (b, 0, 0)),
                  pl.BlockSpec((LANES, LANES), lambda b, h: (0, h)),
                  pl.BlockSpec((LANES, LANES), lambda b, h: (0, h)),
                  pl.BlockSpec((1, LANES), lambda b, h: (0, h)),
                  pl.BlockSpec((1, LANES), lambda b, h: (0, h)),
                  pl.BlockSpec((1, LANES), lambda b, h: (0, 0))],
        out_specs=pl.BlockSpec((1, t_len, LANES), lambda b, h: (b, 0, h)),
        out_shape=jax.ShapeDtypeStruct((bsz, t_len, C_HEADS * LANES), BF16),
        scratch_shapes=[pltpu.VMEM((t_len, LANES), F32), pltpu.VMEM((C_DV, C_DK), F32)],
        compiler_params=_cparams(("parallel", "parallel")),
        name="gla",
    )(c3, c3, c3, c3, d3, wf, wb, bf, bb, gco)
    return out.reshape(bsz * t_len, C_HEADS * LANES)


def _merge_kernel(x_ref, ya_ref, yb_ref, yc_ref, gate_ref, bg_ref, wpa_ref, wpb_ref, wpc_ref, wo_ref, o_ref):
    m = None
    for j, (y_ref, w_ref) in enumerate(((ya_ref, wpa_ref), (yb_ref, wpb_ref), (yc_ref, wpc_ref))):
        sl = slice(j * D_MODEL, (j + 1) * D_MODEL)
        gb = _sigmoid(gate_ref[:, sl].astype(F32) + bg_ref[:, sl])
        term = gb * _dot(y_ref[...], w_ref[...])
        m = term if m is None else m + term
    o_ref[...] = x_ref[...] + _dot(m.astype(BF16), wo_ref[...])


def _merge(x2, ya, yb, yc, gates, b_gate, wpa, wpb, wpc, wo, tm):
    n = x2.shape[0]
    return pl.pallas_call(
        _merge_kernel,
        grid=(n // tm,),
        in_specs=[pl.BlockSpec((tm, D_MODEL), lambda i: (i, 0)),
                  pl.BlockSpec((tm, 512), lambda i: (i, 0)),
                  pl.BlockSpec((tm, 512), lambda i: (i, 0)),
                  pl.BlockSpec((tm, 512), lambda i: (i, 0)),
                  pl.BlockSpec((tm, N_BRANCH * D_MODEL), lambda i: (i, 0)),
                  _const_spec((1, N_BRANCH * D_MODEL)),
                  _const_spec((512, D_MODEL)), _const_spec((512, D_MODEL)), _const_spec((512, D_MODEL)),
                  _const_spec((D_MODEL, D_MODEL))],
        out_specs=pl.BlockSpec((tm, D_MODEL), lambda i: (i, 0)),
        out_shape=jax.ShapeDtypeStruct((n, D_MODEL), F32),
        compiler_params=_cparams(("parallel",)),
        name="merge",
    )(x2, ya, yb, yc, gates, b_gate, wpa, wpb, wpc, wo)


def _ffn_kernel(x_ref, g_ref, wg_ref, wu_ref, wd_ref, o_ref, *, chunk):
    x = x_ref[...]
    h = _rms(x, g_ref[...]).astype(BF16)
    o_ref[...] = x
    for c0 in range(0, D_FF, chunk):
        gt = _dot(h, wg_ref[:, c0:c0 + chunk])
        up = _dot(h, wu_ref[:, c0:c0 + chunk])
        a = (gt * _sigmoid(gt) * up).astype(BF16)
        o_ref[...] += _dot(a, wd_ref[c0:c0 + chunk, :])


def _ffn(x2, g, wg, wu, wd, tm, chunk=256):
    n = x2.shape[0]
    return pl.pallas_call(
        functools.partial(_ffn_kernel, chunk=chunk),
        grid=(n // tm,),
        in_specs=[pl.BlockSpec((tm, D_MODEL), lambda i: (i, 0)),
                  _const_spec((1, D_MODEL)),
                  _const_spec((D_MODEL, D_FF)), _const_spec((D_MODEL, D_FF)), _const_spec((D_FF, D_MODEL))],
        out_specs=pl.BlockSpec((tm, D_MODEL), lambda i: (i, 0)),
        out_shape=jax.ShapeDtypeStruct((n, D_MODEL), F32),
        compiler_params=_cparams(("parallel",)),
        name="ffn",
    )(x2, g, wg, wu, wd)


def _router_kernel(x_ref, g_ref, wr_ref, h_ref, gate_ref):
    hf = _rms(x_ref[...], g_ref[...])
    h_ref[...] = hf.astype(h_ref.dtype)
    logits = _dot3(hf, wr_ref[...])
    lane = lax.broadcasted_iota(jnp.int32, logits.shape, 1).astype(F32)
    neg = jnp.float32(-jnp.inf)
    logits = jnp.where(lane < N_EXPERTS, logits, neg)
    v1 = jnp.max(logits, axis=-1, keepdims=True)
    i1 = jnp.min(jnp.where(logits == v1, lane, float(LANES)), axis=-1, keepdims=True)
    rest = jnp.where(lane == i1, neg, logits)
    v2 = jnp.max(rest, axis=-1, keepdims=True)
    i2 = jnp.min(jnp.where(rest == v2, lane, float(LANES)), axis=-1, keepdims=True)
    e2 = jnp.exp(v2 - v1)
    w1 = 1.0 / (1.0 + e2)
    w2 = e2 / (1.0 + e2)
    gate_ref[...] = jnp.where(lane == i1, w1, 0.0) + jnp.where(lane == i2, w2, 0.0)


def _router(x2, g, wr_pad, tm):
    n = x2.shape[0]
    return pl.pallas_call(
        _router_kernel,
        grid=(n // tm,),
        in_specs=[pl.BlockSpec((tm, D_MODEL), lambda i: (i, 0)),
                  _const_spec((1, D_MODEL)),
                  _const_spec((D_MODEL, LANES))],
        out_specs=[pl.BlockSpec((tm, D_MODEL), lambda i: (i, 0)),
                   pl.BlockSpec((tm, LANES), lambda i: (i, 0))],
        out_shape=[jax.ShapeDtypeStruct((n, D_MODEL), BF16), jax.ShapeDtypeStruct((n, LANES), F32)],
        compiler_params=_cparams(("parallel",)),
        name="router",
    )(x2, g, wr_pad)


def _moe_kernel(x_ref, h_ref, gate_ref, wg_ref, wu_ref, wd_ref, gfin_ref, o_ref, acc_ref, *, final_norm):
    e, c = pl.program_id(1), pl.program_id(2)
    ne, nc = pl.num_programs(1), pl.num_programs(2)

    @pl.when((e == 0) & (c == 0))
    def _():
        acc_ref[...] = x_ref[...]

    gate = gate_ref[...]
    lane = lax.broadcasted_iota(jnp.int32, gate.shape, 1)
    ge = jnp.sum(jnp.where(lane == e, gate, 0.0), axis=-1, keepdims=True)
    h = h_ref[...]
    gt = _dot(h, wg_ref[0].astype(BF16))
    up = _dot(h, wu_ref[0].astype(BF16))
    a = (ge * (gt * _sigmoid(gt) * up)).astype(BF16)
    acc_ref[...] += _dot(a, wd_ref[0].astype(BF16))

    @pl.when((e == ne - 1) & (c == nc - 1))
    def _():
        y = acc_ref[...]
        o_ref[...] = _rms(y, gfin_ref[...]) if final_norm else y


def _moe(x2, h, gate, wg, wu, wd, gfin, tm, fc, final_norm):
    n = x2.shape[0]
    return pl.pallas_call(
        functools.partial(_moe_kernel, final_norm=final_norm),
        grid=(n // tm, N_EXPERTS, D_FF_EXPERT // fc),
        in_specs=[pl.BlockSpec((tm, D_MODEL), lambda i, e, c: (i, 0)),
                  pl.BlockSpec((tm, D_MODEL), lambda i, e, c: (i, 0)),
                  pl.BlockSpec((tm, LANES), lambda i, e, c: (i, 0)),
                  pl.BlockSpec((1, D_MODEL, fc), lambda i, e, c: (e, 0, c)),
                  pl.BlockSpec((1, D_MODEL, fc), lambda i, e, c: (e, 0, c)),
                  pl.BlockSpec((1, fc, D_MODEL), lambda i, e, c: (e, c, 0)),
                  _const_spec((1, D_MODEL))],
        out_specs=pl.BlockSpec((tm, D_MODEL), lambda i, e, c: (i, 0)),
        out_shape=jax.ShapeDtypeStruct((n, D_MODEL), F32),
        scratch_shapes=[pltpu.VMEM((tm, D_MODEL), F32)],
        compiler_params=_cparams(("parallel", "arbitrary", "arbitrary")),
        name="moe",
    )(x2, h, gate, wg, wu, wd, gfin)


def _final_norm_kernel(x_ref, g_ref, o_ref):
    o_ref[...] = _rms(x_ref[...], g_ref[...])


def _final_norm(x2, g, tm):
    n = x2.shape[0]
    return pl.pallas_call(
        _final_norm_kernel,
        grid=(n // tm,),
        in_specs=[pl.BlockSpec((tm, D_MODEL), lambda i: (i, 0)), _const_spec((1, D_MODEL))],
        out_specs=pl.BlockSpec((tm, D_MODEL), lambda i: (i, 0)),
        out_shape=jax.ShapeDtypeStruct((n, D_MODEL), F32),
        compiler_params=_cparams(("parallel",)),
        name="final_norm",
    )(x2, g)


def _rope_tables(t_len):
    rows = t_len // GRID_W
    row = jnp.repeat(jnp.arange(rows, dtype=F32), GRID_W)
    col = jnp.tile(jnp.arange(GRID_W, dtype=F32), rows)
    lane = np.arange(LANES)

    def table(rot_dim, lane_in_slice, active):
        n_freq = rot_dim // 4
        inv_freq = ROPE_THETA ** (-jnp.arange(n_freq, dtype=F32) / n_freq)
        freq = inv_freq[lane_in_slice % n_freq]
        use_row = (lane_in_slice % rot_dim) < (rot_dim // 2)
        ang = jnp.where(use_row[None, :], row[:, None], col[:, None]) * freq[None, :]
        sign = np.where((lane_in_slice % (rot_dim // 2)) < n_freq, -1.0, 1.0).astype(np.float32)
        cos = jnp.where(active[None, :], jnp.cos(ang), 1.0)
        sin = jnp.where(active[None, :], jnp.sin(ang) * sign[None, :], 0.0)
        return cos.astype(F32), sin.astype(F32)

    cos_a, sin_a = table(A_HEAD_DIM, lane % A_HEAD_DIM, np.ones(LANES, bool))
    in_rope = (lane >= B_NOPE_DIM) & (lane < B_NOPE_DIM + B_ROPE_DIM)
    cos_b, sin_b = table(B_ROPE_DIM, (lane - B_NOPE_DIM) % B_ROPE_DIM, in_rope)
    return cos_a, sin_a, cos_b, sin_b


def _pack_w_in(w):
    d = A_HEAD_DIM
    o_ak, o_av, o_bc, o_kr, o_c, o_dec, o_gate = 512, 640, 768, 1280, 1312, 3360, 3392
    dup = lambda base: [w[:, base + hd * d:base + (hd + 1) * d] for hd in (0, 0, 1, 1)]
    zeros = lambda k: jnp.zeros((D_MODEL, k), w.dtype)
    cols = [w[:, :512], *dup(o_ak), *dup(o_av), w[:, o_bc:o_kr],
            w[:, o_kr:o_c], zeros(LANES - B_ROPE_DIM),
            w[:, o_c:o_dec],
            w[:, o_dec:o_gate], zeros(LANES - 2 * C_GATE_RANK),
            w[:, o_gate:]]
    return jnp.concatenate(cols, axis=1).astype(BF16)


def _pack_b_weights(w_q_up, w_kv_up):
    hq = w_q_up.reshape(B_Q_RANK, B_HEADS, B_NOPE_DIM + B_ROPE_DIM)
    wq = jnp.pad(hq, ((0, 0), (0, 0), (0, LANES - B_NOPE_DIM - B_ROPE_DIM))).reshape(B_Q_RANK, B_HEADS * LANES)
    hkv = w_kv_up.reshape(B_KV_RANK, B_HEADS, B_NOPE_DIM + B_V_DIM)
    wk = jnp.pad(hkv[:, :, :B_NOPE_DIM], ((0, 0), (0, 0), (0, LANES - B_NOPE_DIM))).reshape(B_KV_RANK, B_HEADS * LANES)
    wv = hkv[:, :, B_NOPE_DIM:].reshape(B_KV_RANK, B_HEADS * B_V_DIM)
    return wq.astype(BF16), wk.astype(BF16), wv.astype(BF16)


def _rope_placement():
    e = np.zeros((LANES, LANES), np.float32)
    e[np.arange(B_ROPE_DIM), B_NOPE_DIM + np.arange(B_ROPE_DIM)] = 1.0
    return jnp.asarray(e, BF16)


def kernel(x, w_in, b_gate, g_mix, g_a_q, g_a_k, g_b_q, w_b_q_up, g_b_kv, w_b_kv_up, w_c_af_up, b_c_af, w_c_ab_up, b_c_ab, g_c_out, w_pa, w_pb, w_pc, w_out, g_ffn, w_ff_gate, w_ff_up, w_ff_down, w_router, w_e_gate, w_e_up, w_e_down, g_final):
    bsz, t_len, _ = x.shape
    n = bsz * t_len
    depth = w_in.shape[0]
    tm = min(512, t_len)
    tq_a = min(128, t_len)
    tq_b = min(512, t_len)
    gla_tile = min(256, t_len)
    tm_moe = min(1024, n)

    cos_a, sin_a, cos_b, sin_b = _rope_tables(t_len)
    e_mat = _rope_placement()
    row = lambda v: v.reshape(1, -1).astype(F32)
    x2 = x.reshape(n, D_MODEL)

    for i in range(depth):
        w_packed = _pack_w_in(w_in[i])
        aq, ak2, av2, bc, kr, cqkvg, dec, gates = _inproj(x2, row(g_mix[i]), w_packed, tm)

        qa, ka = _prep_a(aq, ak2, row(jnp.tile(g_a_q[i], 2)), row(jnp.tile(g_a_k[i], 2)), cos_a, sin_a, tm, t_len)
        ya = _attn_a(qa, ka, av2, bsz, t_len, tq_a)

        wq, wk, wv = _pack_b_weights(w_b_q_up[i], w_b_kv_up[i])
        qb, kb, vb = _prep_b(bc, kr, row(g_b_q[i]), row(g_b_kv[i]), wq, wk, wv, e_mat, cos_b, sin_b, tm, t_len)
        yb = _attn_b(qb, kb, vb, bsz, t_len, tq_b)

        wf = jnp.pad(w_c_af_up[i], ((0, LANES - C_GATE_RANK), (0, 0)))
        wb = jnp.pad(w_c_ab_up[i], ((C_GATE_RANK, LANES - 2 * C_GATE_RANK), (0, 0)))
        yc = _gla(cqkvg, dec, wf, wb, row(b_c_af[i]), row(b_c_ab[i]), row(g_c_out[i]), bsz, t_len, gla_tile)

        x2 = _merge(x2, ya, yb, yc, gates, row(b_gate[i]), w_pa[i].astype(BF16), w_pb[i].astype(BF16),
                    w_pc[i].astype(BF16), w_out[i].astype(BF16), tm)

        j = i // 2
        last = i == depth - 1
        if i % 2 == 0:
            x2 = _ffn(x2, row(g_ffn[i]), w_ff_gate[j].astype(BF16), w_ff_up[j].astype(BF16),
                      w_ff_down[j].astype(BF16), tm)
            if last:
                x2 = _final_norm(x2, row(g_final), tm)
        else:
            wr = jnp.pad(w_router[j], ((0, 0), (0, LANES - N_EXPERTS)))
            h, gate = _router(x2, row(g_ffn[i]), wr, tm)
            x2 = _moe(x2, h, gate, w_e_gate[j], w_e_up[j], w_e_down[j], row(g_final), tm_moe, 512, last)
    return x2.reshape(bsz, t_len, D_MODEL)
```

```python
import functools

import jax
import jax.numpy as jnp
import numpy as np
from jax import lax
from jax.experimental import pallas as pl
from jax.experimental.pallas import tpu as pltpu

F32 = jnp.float32
BF16 = jnp.bfloat16

D_MODEL = 1024
GRID_W = 64
ROPE_THETA = 10000.0
EPS = 1e-6
A_HEADS, A_KV_HEADS, A_HEAD_DIM = 8, 2, 64
B_HEADS, B_NOPE_DIM, B_ROPE_DIM, B_V_DIM = 8, 64, 32, 64
B_Q_RANK = B_KV_RANK = 256
C_HEADS, C_DK, C_DV, C_GATE_RANK, C_TAU, C_CHUNK = 4, 128, 128, 16, 16.0, 64
N_BRANCH = 3
D_FF = 2816
N_EXPERTS, TOP_K, D_FF_EXPERT = 8, 2, 3584

LANES = 128
VMEM_LIMIT = 56 * 1024 * 1024

P_AQ, P_AK, P_AV, P_BC, P_KR, P_C, P_DEC, P_GATE = 0, 512, 768, 1024, 1536, 1664, 3712, 3840
P_WIDTH = P_GATE + N_BRANCH * D_MODEL


def _cparams(sem):
    return pltpu.CompilerParams(dimension_semantics=sem, vmem_limit_bytes=VMEM_LIMIT)


def _const_spec(shape):
    nd = len(shape)
    return pl.BlockSpec(shape, lambda *_: (0,) * nd, pipeline_mode=pl.Buffered(1))


def _rms(xf, g):
    return xf * lax.rsqrt(jnp.mean(xf * xf, axis=-1, keepdims=True) + EPS) * g


def _sigmoid(x):
    return 1.0 / (1.0 + jnp.exp(-x))


def _split_bf16(x):
    hi = x.astype(BF16)
    lo = (x - hi.astype(F32)).astype(BF16)
    return hi, lo


def _dot(a, b):
    return jnp.dot(a, b, preferred_element_type=F32)


def _dot_nt(a, b):
    return lax.dot_general(a, b, (((1,), (1,)), ((), ())), preferred_element_type=F32)


def _dot_tn(a, b):
    return lax.dot_general(a, b, (((0,), (0,)), ((), ())), preferred_element_type=F32)


def _inproj_kernel(x_ref, g_ref, w_ref, aq_ref, ak_ref, av_ref, bc_ref, kr_ref, c_ref, dec_ref, gate_ref):
    h = _rms(x_ref[...], g_ref[...]).astype(BF16)

    def proj(o_ref, off, width, chunk=512):
        for c0 in range(0, width, chunk):
            cw = min(chunk, width - c0)
            o_ref[:, c0:c0 + cw] = _dot(h, w_ref[:, off + c0:off + c0 + cw]).astype(o_ref.dtype)

    proj(aq_ref, P_AQ, 512)
    proj(ak_ref, P_AK, 256)
    proj(av_ref, P_AV, 256)
    proj(bc_ref, P_BC, 512)
    proj(kr_ref, P_KR, 128)
    proj(c_ref, P_C, 2048)
    proj(dec_ref, P_DEC, 128)
    proj(gate_ref, P_GATE, N_BRANCH * D_MODEL)


def _inproj(x2, g, w_packed, tm):
    n = x2.shape[0]
    widths = (512, 256, 256, 512, 128, 2048, 128, N_BRANCH * D_MODEL)
    dtypes = (BF16, BF16, BF16, BF16, BF16, BF16, F32, BF16)
    return pl.pallas_call(
        _inproj_kernel,
        grid=(n // tm,),
        in_specs=[pl.BlockSpec((tm, D_MODEL), lambda i: (i, 0)),
                  _const_spec((1, D_MODEL)),
                  _const_spec((D_MODEL, P_WIDTH))],
        out_specs=[pl.BlockSpec((tm, w), lambda i: (i, 0)) for w in widths],
        out_shape=[jax.ShapeDtypeStruct((n, w), dt) for w, dt in zip(widths, dtypes)],
        compiler_params=_cparams(("parallel",)),
        name="inproj",
    )(x2, g, w_packed)


def _rope_group(x, cos, sin_signed, lane, half):
    fwd = pltpu.roll(x, LANES - half, axis=1)
    bwd = pltpu.roll(x, half, axis=1)
    swapped = jnp.where((lane % (2 * half)) < half, fwd, bwd)
    return x * cos + swapped * sin_signed


def _seg64_meansq(x, lane):
    sq = x * x
    lo = jnp.sum(jnp.where(lane < 64, sq, 0.0), axis=-1, keepdims=True)
    hi = jnp.sum(jnp.where(lane < 64, 0.0, sq), axis=-1, keepdims=True)
    return jnp.where(lane < 64, lo, hi) * (1.0 / 64.0)


def _prep_a_kernel(aq_ref, ak_ref, gq_ref, gk_ref, cos_ref, sin_ref, q_ref, k_ref):
    rows = aq_ref.shape[0]
    lane = lax.broadcasted_iota(jnp.int32, (rows, LANES), 1)
    cos, sin = cos_ref[...], sin_ref[...]
    gq, gk = gq_ref[...], gk_ref[...]
    scale = A_HEAD_DIM ** -0.5
    for gi in range(4):
        sl = slice(gi * LANES, (gi + 1) * LANES)
        x = aq_ref[:, sl].astype(F32)
        y = x * lax.rsqrt(_seg64_meansq(x, lane) + EPS) * gq
        q_ref[:, sl] = (_rope_group(y, cos, sin, lane, 16) * scale).astype(q_ref.dtype)
    for gi in range(2):
        sl = slice(gi * LANES, (gi + 1) * LANES)
        x = ak_ref[:, sl].astype(F32)
        y = x * lax.rsqrt(_seg64_meansq(x, lane) + EPS) * gk
        k_ref[:, sl] = _rope_group(y, cos, sin, lane, 16).astype(k_ref.dtype)


def _prep_a(aq, ak2, gq, gk, cos, sin, tm, t_len):
    n = aq.shape[0]
    nt = t_len // tm
    return pl.pallas_call(
        _prep_a_kernel,
        grid=(n // tm,),
        in_specs=[pl.BlockSpec((tm, 512), lambda i: (i, 0)),
                  pl.BlockSpec((tm, 256), lambda i: (i, 0)),
                  _const_spec((1, LANES)), _const_spec((1, LANES)),
                  pl.BlockSpec((tm, LANES), lambda i: (i % nt, 0)),
                  pl.BlockSpec((tm, LANES), lambda i: (i % nt, 0))],
        out_specs=[pl.BlockSpec((tm, 512), lambda i: (i, 0)),
                   pl.BlockSpec((tm, 256), lambda i: (i, 0))],
        out_shape=[jax.ShapeDtypeStruct((n, 512), BF16), jax.ShapeDtypeStruct((n, 256), BF16)],
        compiler_params=_cparams(("parallel",)),
        name="prep_a",
    )(aq, ak2, gq, gk, cos, sin)


def _prep_b_kernel(bc_ref, kr_ref, gq_ref, gkv_ref, wq_ref, wk_ref, wv_ref, e_ref, cos_ref, sin_ref,
                   q_ref, k_ref, v_ref):
    rows = bc_ref.shape[0]
    lane = lax.broadcasted_iota(jnp.int32, (rows, LANES), 1)
    cos, sin = cos_ref[...], sin_ref[...]
    cq = _rms(bc_ref[:, :B_Q_RANK].astype(F32), gq_ref[...]).astype(BF16)
    ckv = _rms(bc_ref[:, B_Q_RANK:].astype(F32), gkv_ref[...]).astype(BF16)
    kr = kr_ref[...]
    scale = (B_NOPE_DIM + B_ROPE_DIM) ** -0.5
    for h in range(B_HEADS):
        sl = slice(h * LANES, (h + 1) * LANES)
        q = _dot(cq, wq_ref[:, sl])
        q_ref[:, sl] = (_rope_group(q, cos, sin, lane, 8) * scale).astype(q_ref.dtype)
        k = _dot(ckv, wk_ref[:, sl]) + _dot(kr, e_ref[...])
        k_ref[:, sl] = _rope_group(k, cos, sin, lane, 8).astype(k_ref.dtype)
    v_ref[...] = _dot(ckv, wv_ref[...]).astype(v_ref.dtype)


def _prep_b(bc, kr, gq, gkv, wq, wk, wv, e_mat, cos, sin, tm, t_len):
    n = bc.shape[0]
    nt = t_len // tm
    return pl.pallas_call(
        _prep_b_kernel,
        grid=(n // tm,),
        in_specs=[pl.BlockSpec((tm, 512), lambda i: (i, 0)),
                  pl.BlockSpec((tm, LANES), lambda i: (i, 0)),
                  _const_spec((1, B_Q_RANK)), _const_spec((1, B_KV_RANK)),
                  _const_spec((B_Q_RANK, B_HEADS * LANES)),
                  _const_spec((B_KV_RANK, B_HEADS * LANES)),
                  _const_spec((B_KV_RANK, B_HEADS * B_V_DIM)),
                  _const_spec((LANES, LANES)),
                  pl.BlockSpec((tm, LANES), lambda i: (i % nt, 0)),
                  pl.BlockSpec((tm, LANES), lambda i: (i % nt, 0))],
        out_specs=[pl.BlockSpec((tm, 1024), lambda i: (i, 0)),
                   pl.BlockSpec((tm, 1024), lambda i: (i, 0)),
                   pl.BlockSpec((tm, 512), lambda i: (i, 0))],
        out_shape=[jax.ShapeDtypeStruct((n, 1024), BF16), jax.ShapeDtypeStruct((n, 1024), BF16),
                   jax.ShapeDtypeStruct((n, 512), BF16)],
        compiler_params=_cparams(("parallel",)),
        name="prep_b",
    )(bc, kr, gq, gkv, wq, wk, wv, e_mat, cos, sin)


def _softmax_pv(s, v3):
    m = jnp.max(s, axis=-1, keepdims=True)
    p = jnp.exp(s - m).astype(BF16)
    r = _dot(p, v3)
    return r[:, :LANES] / r[:, LANES:]


def _attn_a_kernel(q_ref, k_ref, v_ref, o_ref, v3_ref):
    tq = q_ref.shape[1]

    @pl.when(pl.program_id(2) == 0)
    def _():
        v3_ref[:, :LANES] = v_ref[0]
        v3_ref[:, LANES:] = jnp.ones((v_ref.shape[1], LANES), BF16)

    lane = lax.broadcasted_iota(jnp.int32, (tq, LANES), 1)
    first = lane < 64
    parts = []
    for gi in range(2):
        qp = q_ref[0, :, gi * LANES:(gi + 1) * LANES].astype(F32)
        parts.append(jnp.where(first, qp, 0.0).astype(BF16))
        parts.append(jnp.where(first, 0.0, qp).astype(BF16))
    qs = jnp.concatenate(parts, axis=0)
    s = _dot_nt(qs, k_ref[0])
    o = _softmax_pv(s, v3_ref[...])
    for gi in range(2):
        o0 = o[(2 * gi) * tq:(2 * gi + 1) * tq]
        o1 = o[(2 * gi + 1) * tq:(2 * gi + 2) * tq]
        o_ref[0, :, gi * LANES:(gi + 1) * LANES] = jnp.where(first, o0, o1).astype(o_ref.dtype)


def _attn_a(q, k2, v2, bsz, t_len, tq):
    q3, k3, v3 = (a.reshape(bsz, t_len, a.shape[-1]) for a in (q, k2, v2))
    out = pl.pallas_call(
        _attn_a_kernel,
        grid=(bsz, A_KV_HEADS, t_len // tq),
        in_specs=[pl.BlockSpec((1, tq, 256), lambda b, g, i: (b, i, g)),
                  pl.BlockSpec((1, t_len, LANES), lambda b, g, i: (b, 0, g)),
                  pl.BlockSpec((1, t_len, LANES), lambda b, g, i: (b, 0, g))],
        out_specs=pl.BlockSpec((1, tq, 256), lambda b, g, i: (b, i, g)),
        out_shape=jax.ShapeDtypeStruct((bsz, t_len, 512), BF16),
        scratch_shapes=[pltpu.VMEM((t_len, 2 * LANES), BF16)],
        compiler_params=_cparams(("parallel", "parallel", "arbitrary")),
        name="attn_a",
    )(q3, k3, v3)
    return out.reshape(bsz * t_len, 512)


def _attn_b_kernel(q_ref, k_ref, v_ref, o_ref, v3_ref):
    tq = q_ref.shape[1]

    @pl.when(pl.program_id(2) == 0)
    def _():
        v3_ref[:, :LANES] = v_ref[0]
        v3_ref[:, LANES:] = jnp.ones((v_ref.shape[1], LANES), BF16)

    lane = lax.broadcasted_iota(jnp.int32, (tq, LANES), 1)
    outs = []
    for j in range(2):
        sl = slice(j * LANES, (j + 1) * LANES)
        s = _dot_nt(q_ref[0, :, sl], k_ref[0, :, sl])
        outs.append(_softmax_pv(s, v3_ref[...]))
    o_ref[0] = jnp.where(lane < 64, outs[0], outs[1]).astype(o_ref.dtype)


def _attn_b(q, k, v, bsz, t_len, tq):
    q3, k3, v3 = (a.reshape(bsz, t_len, a.shape[-1]) for a in (q, k, v))
    out = pl.pallas_call(
        _attn_b_kernel,
        grid=(bsz, B_HEADS // 2, t_len // tq),
        in_specs=[pl.BlockSpec((1, tq, 256), lambda b, p, i: (b, i, p)),
                  pl.BlockSpec((1, t_len, 256), lambda b, p, i: (b, 0, p)),
                  pl.BlockSpec((1, t_len, LANES), lambda b, p, i: (b, 0, p))],
        out_specs=pl.BlockSpec((1, tq, LANES), lambda b, p, i: (b, i, p)),
        out_shape=jax.ShapeDtypeStruct((bsz, t_len, 512), BF16),
        scratch_shapes=[pltpu.VMEM((t_len, 2 * LANES), BF16)],
        compiler_params=_cparams(("parallel", "parallel", "arbitrary")),
        name="attn_b",
    )(q3, k3, v3)
    return out.reshape(bsz * t_len, 512)


def _log_sigmoid(z):
    return jnp.minimum(z, 0.0) - jnp.log(1.0 + jnp.exp(-jnp.abs(z)))


def _dot3(a, b):
    ah, al = _split_bf16(a)
    bh, bl = _split_bf16(b)
    return _dot(ah, bh) + (_dot(ah, bl) + _dot(al, bh))


def _gla_kernel(q_ref, k_ref, v_ref, g_ref, dec_ref, wf_ref, wb_ref, bf_ref, bb_ref, gco_ref,
                o_ref, acc_ref, st_ref, *, tile):
    t_len = q_ref.shape[1]
    n_tiles = t_len // tile
    n_chunks = tile // C_CHUNK
    row = lax.broadcasted_iota(jnp.int32, (tile, tile), 0)
    col = lax.broadcasted_iota(jnp.int32, (tile, tile), 1)
    same = (row // C_CHUNK) == (col // C_CHUNK)
    chunk_ones = jnp.where(same, 1.0, 0.0).astype(BF16)
    lower = jnp.where(same & (col <= row), 1.0, 0.0).astype(BF16)
    upper = jnp.where(same & (col >= row), 1.0, 0.0).astype(BF16)
    q_scale = C_DK ** -0.5

    def one_direction(forward):
        st_ref[...] = jnp.zeros_like(st_ref)
        w_ref, b_ref = (wf_ref, bf_ref) if forward else (wb_ref, bb_ref)
        cum, keep = (lower, same & (col <= row)) if forward else (upper, same & (col >= row))

        def tile_body(it, carry):
            j = it if forward else n_tiles - 1 - it
            rows = pl.ds(pl.multiple_of(j * tile, tile), tile)
            q = q_ref[0, rows, :].astype(F32) * q_scale
            k = k_ref[0, rows, :].astype(F32)
            vb = v_ref[0, rows, :]
            la = _log_sigmoid(_dot3(dec_ref[0, rows, :], w_ref[...]) + b_ref[...]) * (1.0 / C_TAU)
            la_hi, la_lo = _split_bf16(la)
            b = _dot(cum, la_hi) + _dot(cum, la_lo)
            tot = _dot(chunk_ones, la_hi) + _dot(chunk_ones, la_lo)
            qt = (q * jnp.exp(b)).astype(BF16)
            kt = (k * jnp.exp(-b)).astype(BF16)
            k2 = (k * jnp.exp(tot - b)).astype(BF16)
            scores = jnp.where(keep, _dot_nt(qt, kt), 0.0).astype(BF16)
            intra = _dot(scores, vb)
            dec = jnp.exp(tot)
            order = range(n_chunks) if forward else range(n_chunks - 1, -1, -1)
            for c in order:
                cs = slice(c * C_CHUNK, (c + 1) * C_CHUNK)
                st = st_ref[...]
                inter = _dot_nt(qt[cs], st.astype(BF16))
                out_rows = pl.ds(pl.multiple_of(j * tile + c * C_CHUNK, C_CHUNK), C_CHUNK)
                if forward:
                    acc_ref[out_rows, :] = intra[cs] + inter
                else:
                    acc_ref[out_rows, :] += intra[cs] + inter
                st_ref[...] = st * dec[c * C_CHUNK:c * C_CHUNK + 1, :] + _dot_tn(vb[cs], k2[cs])
            return carry

        lax.fori_loop(0, n_tiles, tile_body, 0)

    one_direction(True)
    one_direction(False)
    oc = acc_ref[...]
    y = _rms(oc, gco_ref[...])
    gate = g_ref[0].astype(F32)
    o_ref[0] = (y * (gate * _sigmoid(gate))).astype(o_ref.dtype)


def _gla(cqkvg, dec, wf, wb, bf, bb, gco, bsz, t_len, tile):
    c3 = cqkvg.reshape(bsz, t_len, 4 * C_HEADS * LANES)
    d3 = dec.reshape(bsz, t_len, LANES)

    def cspec(j):
        return pl.BlockSpec((1, t_len, LANES), lambda b, h: (b, 0, j * C_HEADS + h))

    out = pl.pallas_call(
        functools.partial(_gla_kernel, tile=tile),
        grid=(bsz, C_HEADS),
        in_specs=[cspec(0), cspec(1), cspec(2), cspec(3),
                  pl.BlockSpec((1, t_len, LANES), lambda b, h: (b, 0, 0)),
                  pl.BlockSpec((LANES, LANES), lambda b, h: (0, h)),
                  pl.BlockSpec((LANES, LANES), lambda b, h: (0, h)),
                  pl.BlockSpec((1, LANES), lambda b, h: (0, h)),
                  pl.BlockSpec((1, LANES), lambda b, h: (0, h)),
                  pl.BlockSpec((1, LANES), lambda b, h: (0, 0))],
        out_specs=pl.BlockSpec((1, t_len, LANES), lambda b, h: (b, 0, h)),
        out_shape=jax.ShapeDtypeStruct((bsz, t_len, C_HEADS * LANES), BF16),
        scratch_shapes=[pltpu.VMEM((t_len, LANES), F32), pltpu.VMEM((C_DV, C_DK), F32)],
        compiler_params=_cparams(("parallel", "parallel")),
        name="gla",
    )(c3, c3, c3, c3, d3, wf, wb, bf, bb, gco)
    return out.reshape(bsz * t_len, C_HEADS * LANES)


def _merge_kernel(x_ref, ya_ref, yb_ref, yc_ref, gate_ref, bg_ref, wpa_ref, wpb_ref, wpc_ref, wo_ref, o_ref):
    m = None
    for j, (y_ref, w_ref) in enumerate(((ya_ref, wpa_ref), (yb_ref, wpb_ref), (yc_ref, wpc_ref))):
        sl = slice(j * D_MODEL, (j + 1) * D_MODEL)
        gb = _sigmoid(gate_ref[:, sl].astype(F32) + bg_ref[:, sl])
        term = gb * _dot(y_ref[...], w_ref[...])
        m = term if m is None else m + term
    o_ref[...] = x_ref[...] + _dot(m.astype(BF16), wo_ref[...])


def _merge(x2, ya, yb, yc, gates, b_gate, wpa, wpb, wpc, wo, tm):
    n = x2.shape[0]
    return pl.pallas_call(
        _merge_kernel,
        grid=(n // tm,),
        in_specs=[pl.BlockSpec((tm, D_MODEL), lambda i: (i, 0)),
                  pl.BlockSpec((tm, 512), lambda i: (i, 0)),
                  pl.BlockSpec((tm, 512), lambda i: (i, 0)),
                  pl.BlockSpec((tm, 512), lambda i: (i, 0)),
                  pl.BlockSpec((tm, N_BRANCH * D_MODEL), lambda i: (i, 0)),
                  _const_spec((1, N_BRANCH * D_MODEL)),
                  _const_spec((512, D_MODEL)), _const_spec((512, D_MODEL)), _const_spec((512, D_MODEL)),
                  _const_spec((D_MODEL, D_MODEL))],
        out_specs=pl.BlockSpec((tm, D_MODEL), lambda i: (i, 0)),
        out_shape=jax.ShapeDtypeStruct((n, D_MODEL), F32),
        compiler_params=_cparams(("parallel",)),
        name="merge",
    )(x2, ya, yb, yc, gates, b_gate, wpa, wpb, wpc, wo)


def _ffn_kernel(x_ref, g_ref, wg_ref, wu_ref, wd_ref, o_ref, *, chunk):
    x = x_ref[...]
    h = _rms(x, g_ref[...]).astype(BF16)
    o_ref[...] = x
    for c0 in range(0, D_FF, chunk):
        gt = _dot(h, wg_ref[:, c0:c0 + chunk])
        up = _dot(h, wu_ref[:, c0:c0 + chunk])
        a = (gt * _sigmoid(gt) * up).astype(BF16)
        o_ref[...] += _dot(a, wd_ref[c0:c0 + chunk, :])


def _ffn(x2, g, wg, wu, wd, tm, chunk=256):
    n = x2.shape[0]
    return pl.pallas_call(
        functools.partial(_ffn_kernel, chunk=chunk),
        grid=(n // tm,),
        in_specs=[pl.BlockSpec((tm, D_MODEL), lambda i: (i, 0)),
                  _const_spec((1, D_MODEL)),
                  _const_spec((D_MODEL, D_FF)), _const_spec((D_MODEL, D_FF)), _const_spec((D_FF, D_MODEL))],
        out_specs=pl.BlockSpec((tm, D_MODEL), lambda i: (i, 0)),
        out_shape=jax.ShapeDtypeStruct((n, D_MODEL), F32),
        compiler_params=_cparams(("parallel",)),
        name="ffn",
    )(x2, g, wg, wu, wd)


def _router_kernel(x_ref, g_ref, wr_ref, h_ref, gate_ref, rank_ref, rank_t_ref, cnt_ref, tot_ref, carry_ref):
    tm = x_ref.shape[0]

    @pl.when(pl.program_id(0) == 0)
    def _():
        carry_ref[...] = jnp.zeros_like(carry_ref)

    hf = _rms(x_ref[...], g_ref[...])
    h_ref[...] = hf.astype(h_ref.dtype)
    logits = _dot3(hf, wr_ref[...])
    lane = lax.broadcasted_iota(jnp.int32, logits.shape, 1).astype(F32)
    neg = jnp.float32(-jnp.inf)
    logits = jnp.where(lane < N_EXPERTS, logits, neg)
    v1 = jnp.max(logits, axis=-1, keepdims=True)
    i1 = jnp.min(jnp.where(logits == v1, lane, float(LANES)), axis=-1, keepdims=True)
    rest = jnp.where(lane == i1, neg, logits)
    v2 = jnp.max(rest, axis=-1, keepdims=True)
    i2 = jnp.min(jnp.where(rest == v2, lane, float(LANES)), axis=-1, keepdims=True)
    e2 = jnp.exp(v2 - v1)
    w1 = 1.0 / (1.0 + e2)
    w2 = e2 / (1.0 + e2)
    gate_ref[...] = jnp.where(lane == i1, w1, 0.0) + jnp.where(lane == i2, w2, 0.0)

    chosen = (lane == i1) | (lane == i2)
    assign = jnp.where(chosen, 1.0, 0.0)
    row = lax.broadcasted_iota(jnp.int32, (tm, tm), 0)
    col = lax.broadcasted_iota(jnp.int32, (tm, tm), 1)
    earlier = jnp.where(col < row, 1.0, 0.0).astype(BF16)
    carry = carry_ref[...]
    before = _dot(earlier, assign.astype(BF16)) + carry[0:1, :]
    rank = jnp.where(chosen, before, -1.0)
    rank_ref[...] = rank
    rank_t_ref[...] = rank.T[:N_EXPERTS, :]
    cnt_ref[0] = carry
    carry = carry + jnp.sum(assign, axis=0, keepdims=True)
    carry_ref[...] = carry
    tot_ref[...] = carry


def _router(x2, g, wr_pad, tm):
    n = x2.shape[0]
    nt = n // tm
    return pl.pallas_call(
        _router_kernel,
        grid=(nt,),
        in_specs=[pl.BlockSpec((tm, D_MODEL), lambda i: (i, 0)),
                  _const_spec((1, D_MODEL)),
                  _const_spec((D_MODEL, LANES))],
        out_specs=[pl.BlockSpec((tm, D_MODEL), lambda i: (i, 0)),
                   pl.BlockSpec((tm, LANES), lambda i: (i, 0)),
                   pl.BlockSpec((tm, LANES), lambda i: (i, 0)),
                   pl.BlockSpec((N_EXPERTS, tm), lambda i: (0, i)),
                   pl.BlockSpec((1, 8, LANES), lambda i: (i, 0, 0)),
                   pl.BlockSpec((8, LANES), lambda i: (0, 0))],
        out_shape=[jax.ShapeDtypeStruct((n, D_MODEL), BF16),
                   jax.ShapeDtypeStruct((n, LANES), F32),
                   jax.ShapeDtypeStruct((n, LANES), F32),
                   jax.ShapeDtypeStruct((N_EXPERTS, n), F32),
                   jax.ShapeDtypeStruct((nt, 8, LANES), F32),
                   jax.ShapeDtypeStruct((8, LANES), F32)],
        scratch_shapes=[pltpu.VMEM((8, LANES), F32)],
        compiler_params=_cparams(("arbitrary",)),
        name="router",
    )(x2, g, wr_pad)


def _moe_plan(cnt, tot, n, tb, rs, tg):
    i32 = jnp.int32
    counts = tot[0, :N_EXPERTS].astype(i32)
    cum = jnp.concatenate([cnt[:, 0, :N_EXPERTS], tot[:1, :N_EXPERTS]], axis=0).astype(i32)
    n_sb = (counts + rs - 1) // rs
    sb_end = jnp.cumsum(n_sb)
    sb_start = sb_end - n_sb
    goff = sb_start * rs
    n_valid_sb = sb_end[-1]
    s_max = 2 * n // rs + N_EXPERTS
    s_ids = jnp.arange(s_max, dtype=i32)
    last_sb = n_valid_sb - 1
    s_eff = jnp.minimum(s_ids, last_sb)
    sb_expert = jnp.minimum(jnp.sum(s_eff[:, None] >= sb_end[None, :], axis=1), N_EXPERTS - 1).astype(i32)
    sb_rows = jnp.clip(counts[sb_expert] - (s_eff - sb_start[sb_expert]) * rs, 0, rs)
    sb_tiles = ((sb_rows + tb - 1) // tb).astype(i32)

    t_ids = jnp.arange(s_max * rs // tg, dtype=i32)
    t_sb = t_ids * tg // rs
    t_exp = sb_expert[jnp.minimum(t_sb, last_sb)]
    t_r0 = t_ids * tg - goff[t_exp]
    t_valid = (t_sb <= last_sb) & (t_r0 < counts[t_exp])
    t_r1 = jnp.minimum(t_r0 + tg, counts[t_exp])
    cum_e = cum[:, t_exp]
    t_lo = jnp.sum(cum_e[1:] <= t_r0[None, :], axis=0)
    t_hi = jnp.sum(cum_e[:-1] < t_r1[None, :], axis=0)
    t_last = (last_sb + 1) * (rs // tg) - 1
    gather_tabs = tuple(a.astype(i32) for a in (t_exp, t_r0, t_lo, t_hi, t_valid, t_last.reshape(1)))

    start = goff[None, :] + cum[:-1]
    num = cum[1:] - cum[:-1]
    last_blk = (last_sb + 1) * (rs // tb) - 1
    b0 = jnp.minimum(start // tb, last_blk)
    b1 = jnp.minimum((start + jnp.maximum(num, 1) - 1) // tb, last_blk)
    use0 = num > 0
    use1 = use0 & (b1 != b0)
    blk = jnp.stack([b0, b1], axis=-1).reshape(-1).astype(i32)
    use = jnp.stack([use0, use1], axis=-1).reshape(-1).astype(i32)
    expert_tabs = (sb_expert, sb_tiles, last_sb.reshape(1).astype(i32))
    return gather_tabs, expert_tabs, (blk, use, goff.astype(i32))


def _gather_kernel(exp_ref, r0_ref, lo_ref, hi_ref, valid_ref, last_ref, rank_ref, h_ref, o_ref, acc_ref, *, tb):
    t = pl.program_id(0)
    tg = o_ref.shape[0]

    @pl.when((t <= last_ref[0]) & (valid_ref[t] == 0))
    def _():
        o_ref[...] = jnp.zeros_like(o_ref)

    @pl.when((t <= last_ref[0]) & (valid_ref[t] == 1))
    def _():
        e = exp_ref[t]
        want = lax.broadcasted_iota(jnp.int32, (tg, tb), 0) + r0_ref[t]
        acc_ref[...] = jnp.zeros_like(acc_ref)

        def body(kb, carry):
            rk = rank_ref[e, pl.ds(kb, 1), :].astype(jnp.int32)
            onehot = jnp.where(rk == want, 1.0, 0.0).astype(BF16)
            acc_ref[...] += _dot(onehot, h_ref[pl.ds(pl.multiple_of(kb * tb, tb), tb), :])
            return carry

        lax.fori_loop(lo_ref[t], hi_ref[t], body, 0)
        o_ref[...] = acc_ref[...].astype(o_ref.dtype)


def _gather(tabs, rank_t3, h, n_rows, tg, tb):
    n = h.shape[0]
    clamp = lambda t, tabs_last: jnp.minimum(t, tabs_last[0])
    return pl.pallas_call(
        functools.partial(_gather_kernel, tb=tb),
        grid_spec=pltpu.PrefetchScalarGridSpec(
            num_scalar_prefetch=6,
            grid=(n_rows // tg,),
            in_specs=[_const_spec((N_EXPERTS, n // tb, tb)),
                      _const_spec((n, D_MODEL))],
            out_specs=pl.BlockSpec((tg, D_MODEL), lambda t, e, r, lo, hi, v, last: (clamp(t, last), 0)),
            scratch_shapes=[pltpu.VMEM((tg, D_MODEL), F32)]),
        out_shape=jax.ShapeDtypeStruct((n_rows, D_MODEL), BF16),
        compiler_params=_cparams(("arbitrary",)),
        name="moe_gather",
    )(*tabs, rank_t3, h)


def _experts_kernel(exp_ref, tiles_ref, last_ref, xs_ref, wg_ref, wu_ref, wd_ref, o_ref,
                    acc_ref, wgb_ref, wub_ref, wdb_ref, *, tb):
    s, c = pl.program_id(0), pl.program_id(1)
    nc = pl.num_programs(1)

    @pl.when(s <= last_ref[0])
    def _():
        @pl.when(c == 0)
        def _():
            acc_ref[...] = jnp.zeros_like(acc_ref)

        wgb_ref[...] = wg_ref[0].astype(BF16)
        wub_ref[...] = wu_ref[0].astype(BF16)
        wdb_ref[...] = wd_ref[0].astype(BF16)

        def body(t, carry):
            rows = pl.ds(pl.multiple_of(t * tb, tb), tb)
            x = xs_ref[rows, :]
            gt = _dot(x, wgb_ref[...])
            up = _dot(x, wub_ref[...])
            a = (gt * _sigmoid(gt) * up).astype(BF16)
            acc_ref[rows, :] += _dot(a, wdb_ref[...])
            return carry

        lax.fori_loop(0, tiles_ref[s], body, 0)

        @pl.when(c == nc - 1)
        def _():
            o_ref[...] = acc_ref[...].astype(o_ref.dtype)


def _experts(tabs, xs, wg, wu, wd, rs, tb, fc):
    n_rows = xs.shape[0]
    nc = D_FF_EXPERT // fc

    def sb(s, last):
        return jnp.minimum(s, last[0])

    def chunk(s, c, last):
        return jnp.where(s <= last[0], c, nc - 1)

    return pl.pallas_call(
        functools.partial(_experts_kernel, tb=tb),
        grid_spec=pltpu.PrefetchScalarGridSpec(
            num_scalar_prefetch=3,
            grid=(n_rows // rs, nc),
            in_specs=[pl.BlockSpec((rs, D_MODEL), lambda s, c, ex, tl, last: (sb(s, last), 0)),
                      pl.BlockSpec((1, D_MODEL, fc), lambda s, c, ex, tl, last: (ex[sb(s, last)], 0, chunk(s, c, last))),
                      pl.BlockSpec((1, D_MODEL, fc), lambda s, c, ex, tl, last: (ex[sb(s, last)], 0, chunk(s, c, last))),
                      pl.BlockSpec((1, fc, D_MODEL), lambda s, c, ex, tl, last: (ex[sb(s, last)], chunk(s, c, last), 0))],
            out_specs=pl.BlockSpec((rs, D_MODEL), lambda s, c, ex, tl, last: (sb(s, last), 0)),
            scratch_shapes=[pltpu.VMEM((rs, D_MODEL), F32),
                            pltpu.VMEM((D_MODEL, fc), BF16), pltpu.VMEM((D_MODEL, fc), BF16),
                            pltpu.VMEM((fc, D_MODEL), BF16)]),
        out_shape=jax.ShapeDtypeStruct((n_rows, D_MODEL), BF16),
        compiler_params=_cparams(("arbitrary", "arbitrary")),
        name="moe_experts",
    )(*tabs, xs, wg, wu, wd)


def _combine_kernel(blk_ref, use_ref, goff_ref, x_ref, gate_ref, rank_ref, *rest, final_norm):
    y_refs, (gfin_ref, o_ref, acc_ref) = rest[:2 * N_EXPERTS], rest[2 * N_EXPERTS:]
    i = pl.program_id(0)
    tt = x_ref.shape[0]
    tb = y_refs[0].shape[0]
    lane = lax.broadcasted_iota(jnp.int32, (tt, LANES), 1)
    col = lax.broadcasted_iota(jnp.int32, (tt, tb), 1)
    gate, rank = gate_ref[...], rank_ref[...]
    acc_ref[...] = x_ref[...]
    for e in range(N_EXPERTS):
        ge = jnp.sum(jnp.where(lane == e, gate, 0.0), axis=-1, keepdims=True)
        rk = jnp.sum(jnp.where(lane == e, rank, 0.0), axis=-1, keepdims=True).astype(jnp.int32)
        pos = jnp.where(rk >= 0, rk + goff_ref[e], -1)
        for j in range(2):
            idx = (i * N_EXPERTS + e) * 2 + j

            @pl.when(use_ref[idx] == 1)
            def _(idx=idx, y_ref=y_refs[2 * e + j], ge=ge, pos=pos):
                onehot = jnp.where(pos - blk_ref[idx] * tb == col, 1.0, 0.0).astype(BF16)
                acc_ref[...] += ge * _dot(onehot, y_ref[...])

    y = acc_ref[...]
    o_ref[...] = _rms(y, gfin_ref[...]) if final_norm else y


def _combine(tabs, x2, gate, rank, ys, gfin, tt, tb, final_norm):
    n = x2.shape[0]

    def yspec(slot):
        return pl.BlockSpec((tb, D_MODEL), lambda i, blk, use, goff: (blk[i * 2 * N_EXPERTS + slot], 0))

    tok = lambda w: pl.BlockSpec((tt, w), lambda i, blk, use, goff: (i, 0))
    return pl.pallas_call(
        functools.partial(_combine_kernel, final_norm=final_norm),
        grid_spec=pltpu.PrefetchScalarGridSpec(
            num_scalar_prefetch=3,
            grid=(n // tt,),
            in_specs=[tok(D_MODEL), tok(LANES), tok(LANES)] + [yspec(k) for k in range(2 * N_EXPERTS)]
                     + [pl.BlockSpec((1, D_MODEL), lambda i, blk, use, goff: (0, 0))],
            out_specs=tok(D_MODEL),
            scratch_shapes=[pltpu.VMEM((tt, D_MODEL), F32)]),
        out_shape=jax.ShapeDtypeStruct((n, D_MODEL), F32),
        compiler_params=_cparams(("arbitrary",)),
        name="moe_combine",
    )(*tabs, x2, gate, rank, *([ys] * (2 * N_EXPERTS)), gfin)


def _moe(x2, g_ffn, w_router, wg, wu, wd, gfin, final_norm):
    n = x2.shape[0]
    tb = min(256, n)
    rs = min(2048, n)
    tg = min(512, rs)
    wr = jnp.pad(w_router, ((0, 0), (0, LANES - N_EXPERTS)))
    h, gate, rank, rank_t, cnt, tot = _router(x2, g_ffn, wr, tb)
    gather_tabs, expert_tabs, combine_tabs = _moe_plan(cnt, tot, n, tb, rs, tg)
    n_rows = (2 * n // rs + N_EXPERTS) * rs
    xs = _gather(gather_tabs, rank_t.reshape(N_EXPERTS, n // tb, tb), h, n_rows, tg, tb)
    ys = _experts(expert_tabs, xs, wg, wu, wd, rs, tb, 512)
    return _combine(combine_tabs, x2, gate, rank, ys, gfin, tb, tb, final_norm)


def _final_norm_kernel(x_ref, g_ref, o_ref):
    o_ref[...] = _rms(x_ref[...], g_ref[...])


def _final_norm(x2, g, tm):
    n = x2.shape[0]
    return pl.pallas_call(
        _final_norm_kernel,
        grid=(n // tm,),
        in_specs=[pl.BlockSpec((tm, D_MODEL), lambda i: (i, 0)), _const_spec((1, D_MODEL))],
        out_specs=pl.BlockSpec((tm, D_MODEL), lambda i: (i, 0)),
        out_shape=jax.ShapeDtypeStruct((n, D_MODEL), F32),
        compiler_params=_cparams(("parallel",)),
        name="final_norm",
    )(x2, g)


def _rope_tables(t_len):
    rows = t_len // GRID_W
    row = jnp.repeat(jnp.arange(rows, dtype=F32), GRID_W)
    col = jnp.tile(jnp.arange(GRID_W, dtype=F32), rows)
    lane = np.arange(LANES)

    def table(rot_dim, lane_in_slice, active):
        n_freq = rot_dim // 4
        inv_freq = ROPE_THETA ** (-jnp.arange(n_freq, dtype=F32) / n_freq)
        freq = inv_freq[lane_in_slice % n_freq]
        use_row = (lane_in_slice % rot_dim) < (rot_dim // 2)
        ang = jnp.where(use_row[None, :], row[:, None], col[:, None]) * freq[None, :]
        sign = np.where((lane_in_slice % (rot_dim // 2)) < n_freq, -1.0, 1.0).astype(np.float32)
        cos = jnp.where(active[None, :], jnp.cos(ang), 1.0)
        sin = jnp.where(active[None, :], jnp.sin(ang) * sign[None, :], 0.0)
        return cos.astype(F32), sin.astype(F32)

    cos_a, sin_a = table(A_HEAD_DIM, lane % A_HEAD_DIM, np.ones(LANES, bool))
    in_rope = (lane >= B_NOPE_DIM) & (lane < B_NOPE_DIM + B_ROPE_DIM)
    cos_b, sin_b = table(B_ROPE_DIM, (lane - B_NOPE_DIM) % B_ROPE_DIM, in_rope)
    return cos_a, sin_a, cos_b, sin_b


def _pack_w_in(w):
    d = A_HEAD_DIM
    o_ak, o_av, o_bc, o_kr, o_c, o_dec, o_gate = 512, 640, 768, 1280, 1312, 3360, 3392
    dup = lambda base: [w[:, base + hd * d:base + (hd + 1) * d] for hd in (0, 0, 1, 1)]
    zeros = lambda k: jnp.zeros((D_MODEL, k), w.dtype)
    cols = [w[:, :512], *dup(o_ak), *dup(o_av), w[:, o_bc:o_kr],
            w[:, o_kr:o_c], zeros(LANES - B_ROPE_DIM),
            w[:, o_c:o_dec],
            w[:, o_dec:o_gate], zeros(LANES - 2 * C_GATE_RANK),
            w[:, o_gate:]]
    return jnp.concatenate(cols, axis=1).astype(BF16)


def _pack_b_weights(w_q_up, w_kv_up):
    hq = w_q_up.reshape(B_Q_RANK, B_HEADS, B_NOPE_DIM + B_ROPE_DIM)
    wq = jnp.pad(hq, ((0, 0), (0, 0), (0, LANES - B_NOPE_DIM - B_ROPE_DIM))).reshape(B_Q_RANK, B_HEADS * LANES)
    hkv = w_kv_up.reshape(B_KV_RANK, B_HEADS, B_NOPE_DIM + B_V_DIM)
    wk = jnp.pad(hkv[:, :, :B_NOPE_DIM], ((0, 0), (0, 0), (0, LANES - B_NOPE_DIM))).reshape(B_KV_RANK, B_HEADS * LANES)
    wv = hkv[:, :, B_NOPE_DIM:].reshape(B_KV_RANK, B_HEADS * B_V_DIM)
    return wq.astype(BF16), wk.astype(BF16), wv.astype(BF16)


def _rope_placement():
    e = np.zeros((LANES, LANES), np.float32)
    e[np.arange(B_ROPE_DIM), B_NOPE_DIM + np.arange(B_ROPE_DIM)] = 1.0
    return jnp.asarray(e, BF16)


def kernel(x, w_in, b_gate, g_mix, g_a_q, g_a_k, g_b_q, w_b_q_up, g_b_kv, w_b_kv_up, w_c_af_up, b_c_af, w_c_ab_up, b_c_ab, g_c_out, w_pa, w_pb, w_pc, w_out, g_ffn, w_ff_gate, w_ff_up, w_ff_down, w_router, w_e_gate, w_e_up, w_e_down, g_final):
    bsz, t_len, _ = x.shape
    n = bsz * t_len
    depth = w_in.shape[0]
    tm = min(512, t_len)
    tq_a = min(128, t_len)
    tq_b = min(512, t_len)
    gla_tile = min(256, t_len)

    cos_a, sin_a, cos_b, sin_b = _rope_tables(t_len)
    e_mat = _rope_placement()
    row = lambda v: v.reshape(1, -1).astype(F32)
    x2 = x.reshape(n, D_MODEL)

    for i in range(depth):
        w_packed = _pack_w_in(w_in[i])
        aq, ak2, av2, bc, kr, cqkvg, dec, gates = _inproj(x2, row(g_mix[i]), w_packed, tm)

        qa, ka = _prep_a(aq, ak2, row(jnp.tile(g_a_q[i], 2)), row(jnp.tile(g_a_k[i], 2)), cos_a, sin_a, tm, t_len)
        ya = _attn_a(qa, ka, av2, bsz, t_len, tq_a)

        wq, wk, wv = _pack_b_weights(w_b_q_up[i], w_b_kv_up[i])
        qb, kb, vb = _prep_b(bc, kr, row(g_b_q[i]), row(g_b_kv[i]), wq, wk, wv, e_mat, cos_b, sin_b, tm, t_len)
        yb = _attn_b(qb, kb, vb, bsz, t_len, tq_b)

        wf = jnp.pad(w_c_af_up[i], ((0, LANES - C_GATE_RANK), (0, 0)))
        wb = jnp.pad(w_c_ab_up[i], ((C_GATE_RANK, LANES - 2 * C_GATE_RANK), (0, 0)))
        yc = _gla(cqkvg, dec, wf, wb, row(b_c_af[i]), row(b_c_ab[i]), row(g_c_out[i]), bsz, t_len, gla_tile)

        x2 = _merge(x2, ya, yb, yc, gates, row(b_gate[i]), w_pa[i].astype(BF16), w_pb[i].astype(BF16),
                    w_pc[i].astype(BF16), w_out[i].astype(BF16), tm)

        j = i // 2
        last = i == depth - 1
        if i % 2 == 0:
            x2 = _ffn(x2, row(g_ffn[i]), w_ff_gate[j].astype(BF16), w_ff_up[j].astype(BF16),
                      w_ff_down[j].astype(BF16), tm)
            if last:
                x2 = _final_norm(x2, row(g_final), tm)
        else:
            x2 = _moe(x2, row(g_ffn[i]), w_router[j], w_e_gate[j], w_e_up[j], w_e_down[j], row(g_final), last)
    return x2.reshape(bsz, t_len, D_MODEL)
```

```python
import functools

import jax
import jax.numpy as jnp
import numpy as np
from jax import lax
from jax.experimental import pallas as pl
from jax.experimental.pallas import tpu as pltpu

F32 = jnp.float32
BF16 = jnp.bfloat16

D_MODEL = 1024
GRID_W = 64
ROPE_THETA = 10000.0
EPS = 1e-6
A_HEADS, A_KV_HEADS, A_HEAD_DIM = 8, 2, 64
B_HEADS, B_NOPE_DIM, B_ROPE_DIM, B_V_DIM = 8, 64, 32, 64
B_Q_RANK = B_KV_RANK = 256
C_HEADS, C_DK, C_DV, C_GATE_RANK, C_TAU, C_CHUNK = 4, 128, 128, 16, 16.0, 64
N_BRANCH = 3
D_FF = 2816
N_EXPERTS, TOP_K, D_FF_EXPERT = 8, 2, 3584

LANES = 128
VMEM_LIMIT = 56 * 1024 * 1024

P_AQ, P_AK, P_AV, P_BC, P_KR, P_C, P_DEC, P_GATE = 0, 512, 768, 1024, 1536, 1664, 3712, 3840
P_WIDTH = P_GATE + N_BRANCH * D_MODEL


def _cparams(sem):
    return pltpu.CompilerParams(dimension_semantics=sem, vmem_limit_bytes=VMEM_LIMIT)


def _const_spec(shape):
    nd = len(shape)
    return pl.BlockSpec(shape, lambda *_: (0,) * nd, pipeline_mode=pl.Buffered(1))


def _rms(xf, g):
    return xf * lax.rsqrt(jnp.mean(xf * xf, axis=-1, keepdims=True) + EPS) * g


def _sigmoid(x):
    return 1.0 / (1.0 + jnp.exp(-x))


def _split_bf16(x):
    hi = x.astype(BF16)
    lo = (x - hi.astype(F32)).astype(BF16)
    return hi, lo


def _dot(a, b):
    return jnp.dot(a, b, preferred_element_type=F32)


def _dot_nt(a, b):
    return lax.dot_general(a, b, (((1,), (1,)), ((), ())), preferred_element_type=F32)


def _dot_tn(a, b):
    return lax.dot_general(a, b, (((0,), (0,)), ((), ())), preferred_element_type=F32)


def _inproj_kernel(x_ref, g_ref, w_ref, aq_ref, ak_ref, av_ref, bc_ref, kr_ref, c_ref, dec_ref, gate_ref):
    h = _rms(x_ref[...], g_ref[...]).astype(BF16)

    def proj(o_ref, off, width, chunk=512):
        for c0 in range(0, width, chunk):
            cw = min(chunk, width - c0)
            o_ref[:, c0:c0 + cw] = _dot(h, w_ref[:, off + c0:off + c0 + cw]).astype(o_ref.dtype)

    proj(aq_ref, P_AQ, 512)
    proj(ak_ref, P_AK, 256)
    proj(av_ref, P_AV, 256)
    proj(bc_ref, P_BC, 512)
    proj(kr_ref, P_KR, 128)
    proj(c_ref, P_C, 2048)
    proj(dec_ref, P_DEC, 128)
    proj(gate_ref, P_GATE, N_BRANCH * D_MODEL)


def _inproj(x2, g, w_packed, tm):
    n = x2.shape[0]
    widths = (512, 256, 256, 512, 128, 2048, 128, N_BRANCH * D_MODEL)
    dtypes = (BF16,) * len(widths)
    return pl.pallas_call(
        _inproj_kernel,
        grid=(n // tm,),
        in_specs=[pl.BlockSpec((tm, D_MODEL), lambda i: (i, 0)),
                  _const_spec((1, D_MODEL)),
                  _const_spec((D_MODEL, P_WIDTH))],
        out_specs=[pl.BlockSpec((tm, w), lambda i: (i, 0)) for w in widths],
        out_shape=[jax.ShapeDtypeStruct((n, w), dt) for w, dt in zip(widths, dtypes)],
        compiler_params=_cparams(("parallel",)),
        name="inproj",
    )(x2, g, w_packed)


def _rope_group(x, cos, sin_signed, lane, half):
    fwd = pltpu.roll(x, LANES - half, axis=1)
    bwd = pltpu.roll(x, half, axis=1)
    swapped = jnp.where((lane % (2 * half)) < half, fwd, bwd)
    return x * cos + swapped * sin_signed


def _seg64_meansq(x, lane):
    sq = x * x
    lo = jnp.sum(jnp.where(lane < 64, sq, 0.0), axis=-1, keepdims=True)
    hi = jnp.sum(jnp.where(lane < 64, 0.0, sq), axis=-1, keepdims=True)
    return jnp.where(lane < 64, lo, hi) * (1.0 / 64.0)


def _prep_a_kernel(aq_ref, ak_ref, gq_ref, gk_ref, cos_ref, sin_ref, q_ref, k_ref):
    rows = aq_ref.shape[0]
    lane = lax.broadcasted_iota(jnp.int32, (rows, LANES), 1)
    cos, sin = cos_ref[...], sin_ref[...]
    gq, gk = gq_ref[...], gk_ref[...]
    scale = A_HEAD_DIM ** -0.5
    for gi in range(4):
        sl = slice(gi * LANES, (gi + 1) * LANES)
        x = aq_ref[:, sl].astype(F32)
        y = x * lax.rsqrt(_seg64_meansq(x, lane) + EPS) * gq
        q_ref[:, sl] = (_rope_group(y, cos, sin, lane, 16) * scale).astype(q_ref.dtype)
    for gi in range(2):
        sl = slice(gi * LANES, (gi + 1) * LANES)
        x = ak_ref[:, sl].astype(F32)
        y = x * lax.rsqrt(_seg64_meansq(x, lane) + EPS) * gk
        k_ref[:, sl] = _rope_group(y, cos, sin, lane, 16).astype(k_ref.dtype)


def _prep_a(aq, ak2, gq, gk, cos, sin, tm, t_len):
    n = aq.shape[0]
    nt = t_len // tm
    return pl.pallas_call(
        _prep_a_kernel,
        grid=(n // tm,),
        in_specs=[pl.BlockSpec((tm, 512), lambda i: (i, 0)),
                  pl.BlockSpec((tm, 256), lambda i: (i, 0)),
                  _const_spec((1, LANES)), _const_spec((1, LANES)),
                  pl.BlockSpec((tm, LANES), lambda i: (i % nt, 0)),
                  pl.BlockSpec((tm, LANES), lambda i: (i % nt, 0))],
        out_specs=[pl.BlockSpec((tm, 512), lambda i: (i, 0)),
                   pl.BlockSpec((tm, 256), lambda i: (i, 0))],
        out_shape=[jax.ShapeDtypeStruct((n, 512), BF16), jax.ShapeDtypeStruct((n, 256), BF16)],
        compiler_params=_cparams(("parallel",)),
        name="prep_a",
    )(aq, ak2, gq, gk, cos, sin)


def _prep_b_kernel(bc_ref, kr_ref, gq_ref, gkv_ref, wq_ref, wk_ref, wv_ref, e_ref, cos_ref, sin_ref,
                   q_ref, k_ref, v_ref):
    rows = bc_ref.shape[0]
    lane = lax.broadcasted_iota(jnp.int32, (rows, LANES), 1)
    cos, sin = cos_ref[...], sin_ref[...]
    cq = _rms(bc_ref[:, :B_Q_RANK].astype(F32), gq_ref[...]).astype(BF16)
    ckv = _rms(bc_ref[:, B_Q_RANK:].astype(F32), gkv_ref[...]).astype(BF16)
    kr = kr_ref[...]
    scale = (B_NOPE_DIM + B_ROPE_DIM) ** -0.5
    for h in range(B_HEADS):
        sl = slice(h * LANES, (h + 1) * LANES)
        q = _dot(cq, wq_ref[:, sl])
        q_ref[:, sl] = (_rope_group(q, cos, sin, lane, 8) * scale).astype(q_ref.dtype)
        k = _dot(ckv, wk_ref[:, sl]) + _dot(kr, e_ref[...])
        k_ref[:, sl] = _rope_group(k, cos, sin, lane, 8).astype(k_ref.dtype)
    v_ref[...] = _dot(ckv, wv_ref[...]).astype(v_ref.dtype)


def _prep_b(bc, kr, gq, gkv, wq, wk, wv, e_mat, cos, sin, tm, t_len):
    n = bc.shape[0]
    nt = t_len // tm
    return pl.pallas_call(
        _prep_b_kernel,
        grid=(n // tm,),
        in_specs=[pl.BlockSpec((tm, 512), lambda i: (i, 0)),
                  pl.BlockSpec((tm, LANES), lambda i: (i, 0)),
                  _const_spec((1, B_Q_RANK)), _const_spec((1, B_KV_RANK)),
                  _const_spec((B_Q_RANK, B_HEADS * LANES)),
                  _const_spec((B_KV_RANK, B_HEADS * LANES)),
                  _const_spec((B_KV_RANK, B_HEADS * B_V_DIM)),
                  _const_spec((LANES, LANES)),
                  pl.BlockSpec((tm, LANES), lambda i: (i % nt, 0)),
                  pl.BlockSpec((tm, LANES), lambda i: (i % nt, 0))],
        out_specs=[pl.BlockSpec((tm, 1024), lambda i: (i, 0)),
                   pl.BlockSpec((tm, 1024), lambda i: (i, 0)),
                   pl.BlockSpec((tm, 512), lambda i: (i, 0))],
        out_shape=[jax.ShapeDtypeStruct((n, 1024), BF16), jax.ShapeDtypeStruct((n, 1024), BF16),
                   jax.ShapeDtypeStruct((n, 512), BF16)],
        compiler_params=_cparams(("parallel",)),
        name="prep_b",
    )(bc, kr, gq, gkv, wq, wk, wv, e_mat, cos, sin)


def _softmax_pv(s, v3):
    m = jnp.max(s, axis=-1, keepdims=True)
    p = jnp.exp(s - m).astype(BF16)
    r = _dot(p, v3)
    return r[:, :LANES] / r[:, LANES:]


def _run_skewed(chains):
    pending, live = list(chains), []
    while pending or live:
        if pending:
            live.append(pending.pop(0))
        for g in reversed(list(live)):
            try:
                next(g)
            except StopIteration:
                live.remove(g)


def _attend(q, k, v3_ref, out):
    s = _dot_nt(q(), k())
    yield
    m = jnp.max(s, axis=-1, keepdims=True)
    p = jnp.exp(s - m).astype(BF16)
    r = _dot(p, v3_ref[...])
    yield
    out.append(r[:, :LANES] / r[:, LANES:])


def _attn_a_kernel(q_ref, k_ref, v_ref, o_ref, v3_ref):
    tq = q_ref.shape[1]
    n_kv = v3_ref.shape[0]

    @pl.when(pl.program_id(2) == 0)
    def _():
        for kv in range(n_kv):
            v3_ref[kv, :, :LANES] = v_ref[0, :, kv * LANES:(kv + 1) * LANES]
            v3_ref[kv, :, LANES:] = jnp.ones((v_ref.shape[1], LANES), BF16)

    lane = lax.broadcasted_iota(jnp.int32, (tq, LANES), 1)
    first = lane < 64
    outs, chains = [], []
    for gi in range(2 * n_kv):
        kv = gi // 2
        k = lambda kv=kv: k_ref[0, :, kv * LANES:(kv + 1) * LANES]
        for keep_first in (True, False):
            def q(gi=gi, keep_first=keep_first):
                qp = q_ref[0, :, gi * LANES:(gi + 1) * LANES].astype(F32)
                return jnp.where(first == keep_first, qp, 0.0).astype(BF16)
            chains.append(_attend(q, k, v3_ref.at[kv], outs))
    _run_skewed(chains)
    for gi in range(2 * n_kv):
        o_ref[0, :, gi * LANES:(gi + 1) * LANES] = jnp.where(first, outs[2 * gi], outs[2 * gi + 1]).astype(o_ref.dtype)


def _attn_a(q, k2, v2, bsz, t_len, tq, n_kv=2):
    q3, k3, v3 = (a.reshape(bsz, t_len, a.shape[-1]) for a in (q, k2, v2))
    out = pl.pallas_call(
        _attn_a_kernel,
        grid=(bsz, A_KV_HEADS // n_kv, t_len // tq),
        in_specs=[pl.BlockSpec((1, tq, 256 * n_kv), lambda b, g, i: (b, i, g)),
                  pl.BlockSpec((1, t_len, LANES * n_kv), lambda b, g, i: (b, 0, g)),
                  pl.BlockSpec((1, t_len, LANES * n_kv), lambda b, g, i: (b, 0, g))],
        out_specs=pl.BlockSpec((1, tq, 256 * n_kv), lambda b, g, i: (b, i, g)),
        out_shape=jax.ShapeDtypeStruct((bsz, t_len, 512), BF16),
        scratch_shapes=[pltpu.VMEM((n_kv, t_len, 2 * LANES), BF16)],
        compiler_params=_cparams(("parallel", "parallel", "arbitrary")),
        name="attn_a",
    )(q3, k3, v3)
    return out.reshape(bsz * t_len, 512)


def _attn_b_kernel(q_ref, k_ref, v_ref, o_ref, v3_ref):
    tq = q_ref.shape[1]
    n_pairs = v3_ref.shape[0]

    @pl.when(pl.program_id(2) == 0)
    def _():
        for pi in range(n_pairs):
            v3_ref[pi, :, :LANES] = v_ref[0, :, pi * LANES:(pi + 1) * LANES]
            v3_ref[pi, :, LANES:] = jnp.ones((v_ref.shape[1], LANES), BF16)

    lane = lax.broadcasted_iota(jnp.int32, (tq, LANES), 1)
    outs, chains = [], []
    for hd in range(2 * n_pairs):
        sl = slice(hd * LANES, (hd + 1) * LANES)
        q = lambda sl=sl: q_ref[0, :, sl]
        k = lambda sl=sl: k_ref[0, :, sl]
        chains.append(_attend(q, k, v3_ref.at[hd // 2], outs))
    _run_skewed(chains)
    for pi in range(n_pairs):
        o_ref[0, :, pi * LANES:(pi + 1) * LANES] = jnp.where(lane < 64, outs[2 * pi], outs[2 * pi + 1]).astype(o_ref.dtype)


def _attn_b(q, k, v, bsz, t_len, tq, n_pairs=4):
    q3, k3, v3 = (a.reshape(bsz, t_len, a.shape[-1]) for a in (q, k, v))
    out = pl.pallas_call(
        _attn_b_kernel,
        grid=(bsz, B_HEADS // (2 * n_pairs), t_len // tq),
        in_specs=[pl.BlockSpec((1, tq, 256 * n_pairs), lambda b, p, i: (b, i, p)),
                  pl.BlockSpec((1, t_len, 256 * n_pairs), lambda b, p, i: (b, 0, p)),
                  pl.BlockSpec((1, t_len, LANES * n_pairs), lambda b, p, i: (b, 0, p))],
        out_specs=pl.BlockSpec((1, tq, LANES * n_pairs), lambda b, p, i: (b, i, p)),
        out_shape=jax.ShapeDtypeStruct((bsz, t_len, 512), BF16),
        scratch_shapes=[pltpu.VMEM((n_pairs, t_len, 2 * LANES), BF16)],
        compiler_params=_cparams(("parallel", "parallel", "arbitrary")),
        name="attn_b",
    )(q3, k3, v3)
    return out.reshape(bsz * t_len, 512)


def _log_sigmoid(z):
    return jnp.minimum(z, 0.0) - jnp.log(1.0 + jnp.exp(-jnp.abs(z)))


def _dot3(a, b):
    ah, al = _split_bf16(a)
    bh, bl = _split_bf16(b)
    return _dot(ah, bh) + (_dot(ah, bl) + _dot(al, bh))


def _gla_kernel(q_ref, k_ref, v_ref, g_ref, dec_ref, wf_ref, wb_ref, bf_ref, bb_ref, gco_ref,
                o_ref, acc_ref, st_ref, *, tile):
    t_len = q_ref.shape[1]
    n_tiles = t_len // tile
    n_chunks = tile // C_CHUNK
    row = lax.broadcasted_iota(jnp.int32, (tile, tile), 0)
    col = lax.broadcasted_iota(jnp.int32, (tile, tile), 1)
    same = (row // C_CHUNK) == (col // C_CHUNK)
    row_chunk = lax.broadcasted_iota(jnp.int32, (tile, LANES), 0) // C_CHUNK
    keep_f = same & (col <= row)
    keep_b = same & (col >= row)
    lower = jnp.where(keep_f, 1.0, 0.0).astype(BF16)
    upper = jnp.where(keep_b, 1.0, 0.0).astype(BF16)
    q_scale = C_DK ** -0.5
    st_ref[...] = jnp.zeros_like(st_ref)

    n_heads = q_ref.shape[2] // LANES

    def tile_step(j, d, hd):
        forward = d == 0
        w_ref, b_ref = (wf_ref, bf_ref) if forward else (wb_ref, bb_ref)
        cum, keep = (lower, keep_f) if forward else (upper, keep_b)
        rows = pl.ds(pl.multiple_of(j * tile, tile), tile)
        ls = slice(hd * LANES, (hd + 1) * LANES)
        si = d * n_heads + hd
        q = q_ref[0, rows, ls].astype(F32) * q_scale
        k = k_ref[0, rows, ls].astype(F32)
        vb = v_ref[0, rows, ls]
        z = _dot(dec_ref[0, rows, :], w_ref[:, ls])
        yield
        la = _log_sigmoid(z + b_ref[:, ls]) * (1.0 / C_TAU)
        r = _dot(cum, jnp.concatenate(_split_bf16(la), axis=1))
        yield
        b = r[:, :LANES] + r[:, LANES:]
        b3 = b.reshape(n_chunks, C_CHUNK, LANES)
        edge = b3[:, C_CHUNK - 1:, :] if forward else b3[:, :1, :]
        tot = jnp.broadcast_to(edge, b3.shape).reshape(tile, LANES)
        qf = q * jnp.exp(b)
        qt = qf.astype(BF16)
        kt = (k * jnp.exp(-b)).astype(BF16)
        k2 = k * jnp.exp(tot - b)
        raw = _dot_nt(qt, kt)
        k2_blk = jnp.concatenate([jnp.where(row_chunk == c, k2, 0.0).astype(BF16) for c in range(n_chunks)], axis=1)
        kv = _dot_tn(vb, k2_blk)
        yield
        intra = _dot(jnp.where(keep, raw, 0.0).astype(BF16), vb)
        dec = jnp.exp(edge)
        q_blk = jnp.concatenate([jnp.where(row_chunk == c, qf, 0.0).astype(BF16) for c in range(n_chunks)], axis=1)
        st = st_ref[si]
        entering = [None] * n_chunks
        for c in (range(n_chunks) if forward else range(n_chunks - 1, -1, -1)):
            entering[c] = st.astype(BF16)
            st = st * dec[c] + kv[:, c * LANES:(c + 1) * LANES]
        st_ref[si] = st
        inter = _dot_nt(q_blk, jnp.concatenate(entering, axis=1))
        yield
        acc_ref[d, rows, ls] = intra + inter

    def body(it, carry):
        _run_skewed([tile_step(it if d == 0 else n_tiles - 1 - it, d, hd)
                     for hd in range(n_heads) for d in (0, 1)])
        return carry

    lax.fori_loop(0, n_tiles, body, 0)
    for hd in range(n_heads):
        ls = slice(hd * LANES, (hd + 1) * LANES)
        y = _rms(acc_ref[0, :, ls] + acc_ref[1, :, ls], gco_ref[...])
        gate = g_ref[0, :, ls].astype(F32)
        o_ref[0, :, ls] = (y * (gate * _sigmoid(gate))).astype(o_ref.dtype)


def _gla(cqkvg, dec, wf, wb, bf, bb, gco, bsz, t_len, tile, hp=4):
    c3 = cqkvg.reshape(bsz, t_len, 4 * C_HEADS * LANES)
    d3 = dec.reshape(bsz, t_len, LANES)
    groups = C_HEADS // hp
    wl = hp * LANES

    def cspec(j):
        return pl.BlockSpec((1, t_len, wl), lambda b, h: (b, 0, j * groups + h))

    out = pl.pallas_call(
        functools.partial(_gla_kernel, tile=tile),
        grid=(bsz, groups),
        in_specs=[cspec(0), cspec(1), cspec(2), cspec(3),
                  pl.BlockSpec((1, t_len, LANES), lambda b, h: (b, 0, 0)),
                  pl.BlockSpec((LANES, wl), lambda b, h: (0, h)),
                  pl.BlockSpec((LANES, wl), lambda b, h: (0, h)),
                  pl.BlockSpec((1, wl), lambda b, h: (0, h)),
                  pl.BlockSpec((1, wl), lambda b, h: (0, h)),
                  pl.BlockSpec((1, LANES), lambda b, h: (0, 0))],
        out_specs=pl.BlockSpec((1, t_len, wl), lambda b, h: (b, 0, h)),
        out_shape=jax.ShapeDtypeStruct((bsz, t_len, C_HEADS * LANES), BF16),
        scratch_shapes=[pltpu.VMEM((2, t_len, wl), F32), pltpu.VMEM((2 * hp, C_DV, C_DK), F32)],
        compiler_params=_cparams(("parallel", "parallel")),
        name="gla",
    )(c3, c3, c3, c3, d3, wf, wb, bf, bb, gco)
    return out.reshape(bsz * t_len, C_HEADS * LANES)


def _merge_kernel(x_ref, ya_ref, yb_ref, yc_ref, gate_ref, bg_ref, wpa_ref, wpb_ref, wpc_ref, wo_ref, o_ref):
    m = None
    for j, (y_ref, w_ref) in enumerate(((ya_ref, wpa_ref), (yb_ref, wpb_ref), (yc_ref, wpc_ref))):
        sl = slice(j * D_MODEL, (j + 1) * D_MODEL)
        gb = _sigmoid(gate_ref[:, sl].astype(F32) + bg_ref[:, sl])
        term = gb * _dot(y_ref[...], w_ref[...])
        m = term if m is None else m + term
    o_ref[...] = x_ref[...] + _dot(m.astype(BF16), wo_ref[...])


def _merge(x2, ya, yb, yc, gates, b_gate, wpa, wpb, wpc, wo, tm):
    n = x2.shape[0]
    return pl.pallas_call(
        _merge_kernel,
        grid=(n // tm,),
        in_specs=[pl.BlockSpec((tm, D_MODEL), lambda i: (i, 0)),
                  pl.BlockSpec((tm, 512), lambda i: (i, 0)),
                  pl.BlockSpec((tm, 512), lambda i: (i, 0)),
                  pl.BlockSpec((tm, 512), lambda i: (i, 0)),
                  pl.BlockSpec((tm, N_BRANCH * D_MODEL), lambda i: (i, 0)),
                  _const_spec((1, N_BRANCH * D_MODEL)),
                  _const_spec((512, D_MODEL)), _const_spec((512, D_MODEL)), _const_spec((512, D_MODEL)),
                  _const_spec((D_MODEL, D_MODEL))],
        out_specs=pl.BlockSpec((tm, D_MODEL), lambda i: (i, 0)),
        out_shape=jax.ShapeDtypeStruct((n, D_MODEL), F32),
        compiler_params=_cparams(("parallel",)),
        name="merge",
    )(x2, ya, yb, yc, gates, b_gate, wpa, wpb, wpc, wo)


def _ffn_kernel(x_ref, g_ref, wg_ref, wu_ref, wd_ref, o_ref, *, chunk):
    x = x_ref[...]
    h = _rms(x, g_ref[...]).astype(BF16)
    o_ref[...] = x
    for c0 in range(0, D_FF, chunk):
        gt = _dot(h, wg_ref[:, c0:c0 + chunk])
        up = _dot(h, wu_ref[:, c0:c0 + chunk])
        a = (gt * _sigmoid(gt) * up).astype(BF16)
        o_ref[...] += _dot(a, wd_ref[c0:c0 + chunk, :])


def _ffn(x2, g, wg, wu, wd, tm, chunk=256):
    n = x2.shape[0]
    return pl.pallas_call(
        functools.partial(_ffn_kernel, chunk=chunk),
        grid=(n // tm,),
        in_specs=[pl.BlockSpec((tm, D_MODEL), lambda i: (i, 0)),
                  _const_spec((1, D_MODEL)),
                  _const_spec((D_MODEL, D_FF)), _const_spec((D_MODEL, D_FF)), _const_spec((D_FF, D_MODEL))],
        out_specs=pl.BlockSpec((tm, D_MODEL), lambda i: (i, 0)),
        out_shape=jax.ShapeDtypeStruct((n, D_MODEL), F32),
        compiler_params=_cparams(("parallel",)),
        name="ffn",
    )(x2, g, wg, wu, wd)


def _router_kernel(x_ref, g_ref, wr_ref, h_ref, gate_ref, rank_ref, rank_t_ref, cnt_ref, tot_ref, carry_ref):
    tm = x_ref.shape[0]

    @pl.when(pl.program_id(0) == 0)
    def _():
        carry_ref[...] = jnp.zeros_like(carry_ref)

    hf = _rms(x_ref[...], g_ref[...])
    h_ref[...] = hf.astype(h_ref.dtype)
    logits = _dot3(hf, wr_ref[...])
    lane = lax.broadcasted_iota(jnp.int32, logits.shape, 1).astype(F32)
    neg = jnp.float32(-jnp.inf)
    logits = jnp.where(lane < N_EXPERTS, logits, neg)
    v1 = jnp.max(logits, axis=-1, keepdims=True)
    i1 = jnp.min(jnp.where(logits == v1, lane, float(LANES)), axis=-1, keepdims=True)
    rest = jnp.where(lane == i1, neg, logits)
    v2 = jnp.max(rest, axis=-1, keepdims=True)
    i2 = jnp.min(jnp.where(rest == v2, lane, float(LANES)), axis=-1, keepdims=True)
    e2 = jnp.exp(v2 - v1)
    w1 = 1.0 / (1.0 + e2)
    w2 = e2 / (1.0 + e2)
    gate_ref[...] = jnp.where(lane == i1, w1, 0.0) + jnp.where(lane == i2, w2, 0.0)

    chosen = (lane == i1) | (lane == i2)
    assign = jnp.where(chosen, 1.0, 0.0)
    row = lax.broadcasted_iota(jnp.int32, (tm, tm), 0)
    col = lax.broadcasted_iota(jnp.int32, (tm, tm), 1)
    earlier = jnp.where(col < row, 1.0, 0.0).astype(BF16)
    carry = carry_ref[...]
    before = _dot(earlier, assign.astype(BF16)) + carry[0:1, :]
    rank = jnp.where(chosen, before, -1.0)
    rank_ref[...] = rank
    rank_t_ref[...] = rank.T[:N_EXPERTS, :]
    cnt_ref[0] = carry
    carry = carry + jnp.sum(assign, axis=0, keepdims=True)
    carry_ref[...] = carry
    tot_ref[...] = carry


def _router(x2, g, wr_pad, tm):
    n = x2.shape[0]
    nt = n // tm
    return pl.pallas_call(
        _router_kernel,
        grid=(nt,),
        in_specs=[pl.BlockSpec((tm, D_MODEL), lambda i: (i, 0)),
                  _const_spec((1, D_MODEL)),
                  _const_spec((D_MODEL, LANES))],
        out_specs=[pl.BlockSpec((tm, D_MODEL), lambda i: (i, 0)),
                   pl.BlockSpec((tm, LANES), lambda i: (i, 0)),
                   pl.BlockSpec((tm, LANES), lambda i: (i, 0)),
                   pl.BlockSpec((N_EXPERTS, tm), lambda i: (0, i)),
                   pl.BlockSpec((1, 8, LANES), lambda i: (i, 0, 0)),
                   pl.BlockSpec((8, LANES), lambda i: (0, 0))],
        out_shape=[jax.ShapeDtypeStruct((n, D_MODEL), BF16),
                   jax.ShapeDtypeStruct((n, LANES), F32),
                   jax.ShapeDtypeStruct((n, LANES), F32),
                   jax.ShapeDtypeStruct((N_EXPERTS, n), F32),
                   jax.ShapeDtypeStruct((nt, 8, LANES), F32),
                   jax.ShapeDtypeStruct((8, LANES), F32)],
        scratch_shapes=[pltpu.VMEM((8, LANES), F32)],
        compiler_params=_cparams(("arbitrary",)),
        name="router",
    )(x2, g, wr_pad)


def _moe_plan(cnt, tot, n, tb, rs, tg):
    i32 = jnp.int32
    counts = tot[0, :N_EXPERTS].astype(i32)
    cum = jnp.concatenate([cnt[:, 0, :N_EXPERTS], tot[:1, :N_EXPERTS]], axis=0).astype(i32)
    n_sb = (counts + rs - 1) // rs
    sb_end = jnp.cumsum(n_sb)
    sb_start = sb_end - n_sb
    goff = sb_start * rs
    n_valid_sb = sb_end[-1]
    s_max = 2 * n // rs + N_EXPERTS
    s_ids = jnp.arange(s_max, dtype=i32)
    last_sb = n_valid_sb - 1
    s_eff = jnp.minimum(s_ids, last_sb)
    sb_expert = jnp.minimum(jnp.sum(s_eff[:, None] >= sb_end[None, :], axis=1), N_EXPERTS - 1).astype(i32)
    sb_rows = jnp.clip(counts[sb_expert] - (s_eff - sb_start[sb_expert]) * rs, 0, rs)
    sb_tiles = ((sb_rows + tb - 1) // tb).astype(i32)

    t_ids = jnp.arange(s_max * rs // tb, dtype=i32)
    t_sb = t_ids * tb // rs
    t_exp = sb_expert[jnp.minimum(t_sb, last_sb)]
    t_r0 = t_ids * tb - goff[t_exp]
    t_valid = (t_sb <= last_sb) & (t_r0 < counts[t_exp])
    t_r1 = jnp.minimum(t_r0 + tb, counts[t_exp])
    cum_e = cum[:, t_exp]
    t_lo = jnp.sum(cum_e[1:] <= t_r0[None, :], axis=0)
    t_hi = jnp.sum(cum_e[:-1] < t_r1[None, :], axis=0)
    t_last = (last_sb + 1) * (rs // tg) - 1
    gather_tabs = tuple(a.astype(i32) for a in (t_exp, t_r0, t_lo, t_hi, t_valid, t_last.reshape(1)))

    start = goff[None, :] + cum[:-1]
    num = cum[1:] - cum[:-1]
    last_blk = (last_sb + 1) * (rs // tb) - 1
    b0 = jnp.minimum(start // tb, last_blk)
    b1 = jnp.minimum((start + jnp.maximum(num, 1) - 1) // tb, last_blk)
    b1 = jnp.where(b1 == b0, jnp.where(b0 < last_blk, b0 + 1, b0 - 1), b1)
    blk = jnp.stack([b0, b1], axis=-1).reshape(-1).astype(i32)
    expert_tabs = (sb_expert, sb_tiles, last_sb.reshape(1).astype(i32))
    return gather_tabs, expert_tabs, (blk, goff.astype(i32))


def _gather_kernel(exp_ref, r0_ref, lo_ref, hi_ref, valid_ref, last_ref, rank_ref, h_ref, o_ref, acc_ref, *, tb):
    step = pl.program_id(0)
    n_sub = o_ref.shape[0] // tb
    n_blk = rank_ref.shape[1]
    row = lax.broadcasted_iota(jnp.int32, (tb, tb), 0)

    for u in range(n_sub):
        t = step * n_sub + u
        rows = slice(u * tb, (u + 1) * tb)
        in_range = step <= last_ref[0]

        @pl.when(in_range & (valid_ref[t] == 0))
        def _(rows=rows):
            o_ref[rows, :] = jnp.zeros((tb, D_MODEL), o_ref.dtype)

        @pl.when(in_range & (valid_ref[t] == 1))
        def _(rows=rows, t=t):
            e = exp_ref[t]
            want = row + r0_ref[t]
            lo, hi = lo_ref[t], hi_ref[t]
            acc_ref[rows, :] = jnp.zeros((tb, D_MODEL), F32)

            def onehot_rows(kb, live):
                rk = rank_ref[e, pl.ds(kb, 1), :].astype(jnp.int32)
                rk = jnp.where(live, rk, -1)
                onehot = jnp.where(rk == want, 1.0, 0.0).astype(BF16)
                return _dot(onehot, h_ref[pl.ds(pl.multiple_of(kb * tb, tb), tb), :])

            def body(p, carry):
                kb = lo + 2 * p
                kb2 = jnp.minimum(kb + 1, n_blk - 1)
                acc_ref[rows, :] += onehot_rows(kb, True) + onehot_rows(kb2, kb + 1 < hi)
                return carry

            lax.fori_loop(0, (hi - lo + 1) // 2, body, 0)
            o_ref[rows, :] = acc_ref[rows, :].astype(o_ref.dtype)


def _gather(tabs, rank_t3, h, n_rows, tg, tb):
    n = h.shape[0]
    clamp = lambda t, tabs_last: jnp.minimum(t, tabs_last[0])
    return pl.pallas_call(
        functools.partial(_gather_kernel, tb=tb),
        grid_spec=pltpu.PrefetchScalarGridSpec(
            num_scalar_prefetch=6,
            grid=(n_rows // tg,),
            in_specs=[_const_spec((N_EXPERTS, n // tb, tb)),
                      _const_spec((n, D_MODEL))],
            out_specs=pl.BlockSpec((tg, D_MODEL), lambda t, e, r, lo, hi, v, last: (clamp(t, last), 0)),
            scratch_shapes=[pltpu.VMEM((tg, D_MODEL), F32)]),
        out_shape=jax.ShapeDtypeStruct((n_rows, D_MODEL), BF16),
        compiler_params=_cparams(("arbitrary",)),
        name="moe_gather",
    )(*tabs, rank_t3, h)


def _experts_kernel(exp_ref, tiles_ref, last_ref, xs_ref, wg_ref, wu_ref, wd_ref, o_ref,
                    acc_ref, wgb_ref, wub_ref, wdb_ref, *, tb):
    s, c = pl.program_id(0), pl.program_id(1)
    nc = pl.num_programs(1)

    @pl.when(s <= last_ref[0])
    def _():
        @pl.when(c == 0)
        def _():
            acc_ref[...] = jnp.zeros_like(acc_ref)

        wgb_ref[...] = wg_ref[0].astype(BF16)
        wub_ref[...] = wu_ref[0].astype(BF16)
        wdb_ref[...] = wd_ref[0].astype(BF16)

        def tile_step(t):
            rows = pl.ds(pl.multiple_of(t * tb, tb), tb)
            x = xs_ref[rows, :]
            gt = _dot(x, wgb_ref[...])
            up = _dot(x, wub_ref[...])
            yield
            a = (gt * _sigmoid(gt) * up).astype(BF16)
            y = _dot(a, wdb_ref[...])
            yield
            acc_ref[rows, :] += y

        def body(t2, carry):
            _run_skewed([tile_step(2 * t2), tile_step(2 * t2 + 1)])
            return carry

        lax.fori_loop(0, (tiles_ref[s] + 1) // 2, body, 0)

        @pl.when(c == nc - 1)
        def _():
            o_ref[...] = acc_ref[...].astype(o_ref.dtype)


def _experts(tabs, xs, wg, wu, wd, rs, tb, fc):
    n_rows = xs.shape[0]
    nc = D_FF_EXPERT // fc

    def sb(s, last):
        return jnp.minimum(s, last[0])

    def chunk(s, c, last):
        return jnp.where(s <= last[0], c, nc - 1)

    return pl.pallas_call(
        functools.partial(_experts_kernel, tb=tb),
        grid_spec=pltpu.PrefetchScalarGridSpec(
            num_scalar_prefetch=3,
            grid=(n_rows // rs, nc),
            in_specs=[pl.BlockSpec((rs, D_MODEL), lambda s, c, ex, tl, last: (sb(s, last), 0)),
                      pl.BlockSpec((1, D_MODEL, fc), lambda s, c, ex, tl, last: (ex[sb(s, last)], 0, chunk(s, c, last))),
                      pl.BlockSpec((1, D_MODEL, fc), lambda s, c, ex, tl, last: (ex[sb(s, last)], 0, chunk(s, c, last))),
                      pl.BlockSpec((1, fc, D_MODEL), lambda s, c, ex, tl, last: (ex[sb(s, last)], chunk(s, c, last), 0))],
            out_specs=pl.BlockSpec((rs, D_MODEL), lambda s, c, ex, tl, last: (sb(s, last), 0)),
            scratch_shapes=[pltpu.VMEM((rs, D_MODEL), F32),
                            pltpu.VMEM((D_MODEL, fc), BF16), pltpu.VMEM((D_MODEL, fc), BF16),
                            pltpu.VMEM((fc, D_MODEL), BF16)]),
        out_shape=jax.ShapeDtypeStruct((n_rows, D_MODEL), BF16),
        compiler_params=_cparams(("arbitrary", "arbitrary")),
        name="moe_experts",
    )(*tabs, xs, wg, wu, wd)


def _combine_kernel(blk_ref, goff_ref, x_ref, gate_ref, rank_ref, *rest, final_norm):
    y_refs, (gfin_ref, o_ref) = rest[:2 * N_EXPERTS], rest[2 * N_EXPERTS:]
    i = pl.program_id(0)
    tt = x_ref.shape[0]
    tb = y_refs[0].shape[0]
    lane = lax.broadcasted_iota(jnp.int32, (tt, LANES), 1)
    col = lax.broadcasted_iota(jnp.int32, (tt, tb), 1)
    gate, rank = gate_ref[...], rank_ref[...]
    y = x_ref[...]
    for e in range(N_EXPERTS):
        ge = jnp.sum(jnp.where(lane == e, gate, 0.0), axis=-1, keepdims=True)
        rk = jnp.sum(jnp.where(lane == e, rank, 0.0), axis=-1, keepdims=True).astype(jnp.int32)
        pos = jnp.where(rk >= 0, rk + goff_ref[e], -1)
        parts = [_dot(jnp.where(pos - blk_ref[(i * N_EXPERTS + e) * 2 + j] * tb == col, 1.0, 0.0).astype(BF16),
                      y_refs[2 * e + j][...]) for j in range(2)]
        y = y + ge * (parts[0] + parts[1])
    o_ref[...] = _rms(y, gfin_ref[...]) if final_norm else y


def _combine(tabs, x2, gate, rank, ys, gfin, tt, tb, final_norm):
    n = x2.shape[0]

    def yspec(slot):
        return pl.BlockSpec((tb, D_MODEL), lambda i, blk, goff: (blk[i * 2 * N_EXPERTS + slot], 0))

    tok = lambda w: pl.BlockSpec((tt, w), lambda i, blk, goff: (i, 0))
    return pl.pallas_call(
        functools.partial(_combine_kernel, final_norm=final_norm),
        grid_spec=pltpu.PrefetchScalarGridSpec(
            num_scalar_prefetch=2,
            grid=(n // tt,),
            in_specs=[tok(D_MODEL), tok(LANES), tok(LANES)] + [yspec(k) for k in range(2 * N_EXPERTS)]
                     + [pl.BlockSpec((1, D_MODEL), lambda i, blk, goff: (0, 0))],
            out_specs=tok(D_MODEL)),
        out_shape=jax.ShapeDtypeStruct((n, D_MODEL), F32),
        compiler_params=_cparams(("arbitrary",)),
        name="moe_combine",
    )(*tabs, x2, gate, rank, *([ys] * (2 * N_EXPERTS)), gfin)


def _moe(x2, g_ffn, w_router, wg, wu, wd, gfin, final_norm):
    n = x2.shape[0]
    rs = min(2048, n)
    tb = min(256, rs // 2)
    tg = min(512, rs)
    wr = jnp.pad(w_router, ((0, 0), (0, LANES - N_EXPERTS)))
    h, gate, rank, rank_t, cnt, tot = _router(x2, g_ffn, wr, tb)
    gather_tabs, expert_tabs, combine_tabs = _moe_plan(cnt, tot, n, tb, rs, tg)
    n_rows = (2 * n // rs + N_EXPERTS) * rs
    xs = _gather(gather_tabs, rank_t.reshape(N_EXPERTS, n // tb, tb), h, n_rows, tg, tb)
    ys = _experts(expert_tabs, xs, wg, wu, wd, rs, tb, 512)
    return _combine(combine_tabs, x2, gate, rank, ys, gfin, tb, tb, final_norm)


def _final_norm_kernel(x_ref, g_ref, o_ref):
    o_ref[...] = _rms(x_ref[...], g_ref[...])


def _final_norm(x2, g, tm):
    n = x2.shape[0]
    return pl.pallas_call(
        _final_norm_kernel,
        grid=(n // tm,),
        in_specs=[pl.BlockSpec((tm, D_MODEL), lambda i: (i, 0)), _const_spec((1, D_MODEL))],
        out_specs=pl.BlockSpec((tm, D_MODEL), lambda i: (i, 0)),
        out_shape=jax.ShapeDtypeStruct((n, D_MODEL), F32),
        compiler_params=_cparams(("parallel",)),
        name="final_norm",
    )(x2, g)


def _rope_tables(t_len):
    rows = t_len // GRID_W
    row = jnp.repeat(jnp.arange(rows, dtype=F32), GRID_W)
    col = jnp.tile(jnp.arange(GRID_W, dtype=F32), rows)
    lane = np.arange(LANES)

    def table(rot_dim, lane_in_slice, active):
        n_freq = rot_dim // 4
        inv_freq = ROPE_THETA ** (-jnp.arange(n_freq, dtype=F32) / n_freq)
        freq = inv_freq[lane_in_slice % n_freq]
        use_row = (lane_in_slice % rot_dim) < (rot_dim // 2)
        ang = jnp.where(use_row[None, :], row[:, None], col[:, None]) * freq[None, :]
        sign = np.where((lane_in_slice % (rot_dim // 2)) < n_freq, -1.0, 1.0).astype(np.float32)
        cos = jnp.where(active[None, :], jnp.cos(ang), 1.0)
        sin = jnp.where(active[None, :], jnp.sin(ang) * sign[None, :], 0.0)
        return cos.astype(F32), sin.astype(F32)

    cos_a, sin_a = table(A_HEAD_DIM, lane % A_HEAD_DIM, np.ones(LANES, bool))
    in_rope = (lane >= B_NOPE_DIM) & (lane < B_NOPE_DIM + B_ROPE_DIM)
    cos_b, sin_b = table(B_ROPE_DIM, (lane - B_NOPE_DIM) % B_ROPE_DIM, in_rope)
    return cos_a, sin_a, cos_b, sin_b


def _pack_w_in(w):
    d = A_HEAD_DIM
    o_ak, o_av, o_bc, o_kr, o_c, o_dec, o_gate = 512, 640, 768, 1280, 1312, 3360, 3392
    dup = lambda base: [w[:, base + hd * d:base + (hd + 1) * d] for hd in (0, 0, 1, 1)]
    zeros = lambda k: jnp.zeros((D_MODEL, k), w.dtype)
    cols = [w[:, :512], *dup(o_ak), *dup(o_av), w[:, o_bc:o_kr],
            w[:, o_kr:o_c], zeros(LANES - B_ROPE_DIM),
            w[:, o_c:o_dec],
            w[:, o_dec:o_gate], zeros(LANES - 2 * C_GATE_RANK),
            w[:, o_gate:]]
    return jnp.concatenate(cols, axis=1).astype(BF16)


def _pack_b_weights(w_q_up, w_kv_up):
    hq = w_q_up.reshape(B_Q_RANK, B_HEADS, B_NOPE_DIM + B_ROPE_DIM)
    wq = jnp.pad(hq, ((0, 0), (0, 0), (0, LANES - B_NOPE_DIM - B_ROPE_DIM))).reshape(B_Q_RANK, B_HEADS * LANES)
    hkv = w_kv_up.reshape(B_KV_RANK, B_HEADS, B_NOPE_DIM + B_V_DIM)
    wk = jnp.pad(hkv[:, :, :B_NOPE_DIM], ((0, 0), (0, 0), (0, LANES - B_NOPE_DIM))).reshape(B_KV_RANK, B_HEADS * LANES)
    wv = hkv[:, :, B_NOPE_DIM:].reshape(B_KV_RANK, B_HEADS * B_V_DIM)
    return wq.astype(BF16), wk.astype(BF16), wv.astype(BF16)


def _rope_placement():
    e = np.zeros((LANES, LANES), np.float32)
    e[np.arange(B_ROPE_DIM), B_NOPE_DIM + np.arange(B_ROPE_DIM)] = 1.0
    return jnp.asarray(e, BF16)


def kernel(x, w_in, b_gate, g_mix, g_a_q, g_a_k, g_b_q, w_b_q_up, g_b_kv, w_b_kv_up, w_c_af_up, b_c_af, w_c_ab_up, b_c_ab, g_c_out, w_pa, w_pb, w_pc, w_out, g_ffn, w_ff_gate, w_ff_up, w_ff_down, w_router, w_e_gate, w_e_up, w_e_down, g_final):
    bsz, t_len, _ = x.shape
    n = bsz * t_len
    depth = w_in.shape[0]
    tm = min(512, t_len)
    tq_a = min(512, t_len)
    tq_b = min(512, t_len)
    gla_tile = min(256, t_len)

    cos_a, sin_a, cos_b, sin_b = _rope_tables(t_len)
    e_mat = _rope_placement()
    row = lambda v: v.reshape(1, -1).astype(F32)
    x2 = x.reshape(n, D_MODEL)

    for i in range(depth):
        w_packed = _pack_w_in(w_in[i])
        aq, ak2, av2, bc, kr, cqkvg, dec, gates = _inproj(x2, row(g_mix[i]), w_packed, tm)

        qa, ka = _prep_a(aq, ak2, row(jnp.tile(g_a_q[i], 2)), row(jnp.tile(g_a_k[i], 2)), cos_a, sin_a, tm, t_len)
        ya = _attn_a(qa, ka, av2, bsz, t_len, tq_a)

        wq, wk, wv = _pack_b_weights(w_b_q_up[i], w_b_kv_up[i])
        qb, kb, vb = _prep_b(bc, kr, row(g_b_q[i]), row(g_b_kv[i]), wq, wk, wv, e_mat, cos_b, sin_b, tm, t_len)
        yb = _attn_b(qb, kb, vb, bsz, t_len, tq_b)

        wf = jnp.pad(w_c_af_up[i], ((0, LANES - C_GATE_RANK), (0, 0))).astype(BF16)
        wb = jnp.pad(w_c_ab_up[i], ((C_GATE_RANK, LANES - 2 * C_GATE_RANK), (0, 0))).astype(BF16)
        yc = _gla(cqkvg, dec, wf, wb, row(b_c_af[i]), row(b_c_ab[i]), row(g_c_out[i]), bsz, t_len, gla_tile)

        x2 = _merge(x2, ya, yb, yc, gates, row(b_gate[i]), w_pa[i].astype(BF16), w_pb[i].astype(BF16),
                    w_pc[i].astype(BF16), w_out[i].astype(BF16), tm)

        j = i // 2
        last = i == depth - 1
        if i % 2 == 0:
            x2 = _ffn(x2, row(g_ffn[i]), w_ff_gate[j].astype(BF16), w_ff_up[j].astype(BF16),
                      w_ff_down[j].astype(BF16), tm)
            if last:
                x2 = _final_norm(x2, row(g_final), tm)
        else:
            x2 = _moe(x2, row(g_ffn[i]), w_router[j], w_e_gate[j], w_e_up[j], w_e_down[j], row(g_final), last)
    return x2.reshape(bsz, t_len, D_MODEL)
```

```python
import functools

import jax
import jax.numpy as jnp
import numpy as np
from jax import lax
from jax.experimental import pallas as pl
from jax.experimental.pallas import tpu as pltpu

F32 = jnp.float32
BF16 = jnp.bfloat16

D_MODEL = 1024
GRID_W = 64
ROPE_THETA = 10000.0
EPS = 1e-6
A_HEADS, A_KV_HEADS, A_HEAD_DIM = 8, 2, 64
B_HEADS, B_NOPE_DIM, B_ROPE_DIM, B_V_DIM = 8, 64, 32, 64
B_Q_RANK = B_KV_RANK = 256
C_HEADS, C_DK, C_DV, C_GATE_RANK, C_TAU, C_CHUNK = 4, 128, 128, 16, 16.0, 64
N_BRANCH = 3
D_FF = 2816
N_EXPERTS, TOP_K, D_FF_EXPERT = 8, 2, 3584

LANES = 128
VMEM_LIMIT = 56 * 1024 * 1024


def _cparams(sem):
    return pltpu.CompilerParams(dimension_semantics=sem, vmem_limit_bytes=VMEM_LIMIT)


def _const_spec(shape):
    nd = len(shape)
    return pl.BlockSpec(shape, lambda *_: (0,) * nd, pipeline_mode=pl.Buffered(1))


def _rms(xf, g):
    return xf * lax.rsqrt(jnp.mean(xf * xf, axis=-1, keepdims=True) + EPS) * g


def _sigmoid(x):
    return 1.0 / (1.0 + jnp.exp(-x))


def _split_bf16(x):
    hi = x.astype(BF16)
    lo = (x - hi.astype(F32)).astype(BF16)
    return hi, lo


def _dot(a, b):
    return jnp.dot(a, b, preferred_element_type=F32)


def _dot_nt(a, b):
    return lax.dot_general(a, b, (((1,), (1,)), ((), ())), preferred_element_type=F32)


def _dot_tn(a, b):
    return lax.dot_general(a, b, (((0,), (0,)), ((), ())), preferred_element_type=F32)


def _rope_group(x, cos, sin_signed, lane, half):
    fwd = pltpu.roll(x, LANES - half, axis=1)
    bwd = pltpu.roll(x, half, axis=1)
    swapped = jnp.where((lane % (2 * half)) < half, fwd, bwd)
    return x * cos + swapped * sin_signed


def _seg64_meansq(x, lane):
    sq = x * x
    lo = jnp.sum(jnp.where(lane < 64, sq, 0.0), axis=-1, keepdims=True)
    hi = jnp.sum(jnp.where(lane < 64, 0.0, sq), axis=-1, keepdims=True)
    return jnp.where(lane < 64, lo, hi) * (1.0 / 64.0)


F_AQ, F_AKV, F_B, F_C, F_DEC, F_GATE = 0, 512, 768, 1408, 3456, 3584
F_WIDTH = F_GATE + N_BRANCH * D_MODEL


def _dup_half(x, lane, low):
    other = pltpu.roll(x, 64, axis=1)
    return jnp.where(lane < 64, x, other) if low else jnp.where(lane < 64, other, x)


def _fused_in_kernel(x_ref, g_ref, w_ref, gaq_ref, gak_ref, cosa_ref, sina_ref,
                     gbq_ref, gbkv_ref, wq_ref, wk_ref, wv_ref, e_ref, cosb_ref, sinb_ref,
                     qa_ref, ka_ref, va_ref, qb_ref, kb_ref, vb_ref, c_ref, dec_ref, gate_ref):
    rows = x_ref.shape[0]
    lane = lax.broadcasted_iota(jnp.int32, (rows, LANES), 1)
    h = _rms(x_ref[...], g_ref[...]).astype(BF16)

    def proj(off, width):
        return _dot(h, w_ref[:, off:off + width])

    def chain_aq():
        y = proj(F_AQ, 512)
        yield
        cos, sin, gq = cosa_ref[...], sina_ref[...], gaq_ref[...]
        for gi in range(4):
            sl = slice(gi * LANES, (gi + 1) * LANES)
            x = y[:, sl]
            x = x * lax.rsqrt(_seg64_meansq(x, lane) + EPS) * gq
            qa_ref[:, sl] = (_rope_group(x, cos, sin, lane, 16) * (A_HEAD_DIM ** -0.5)).astype(qa_ref.dtype)

    def chain_akv():
        y = proj(F_AKV, 256)
        yield
        k, v = y[:, :LANES], y[:, LANES:]
        k = k * lax.rsqrt(_seg64_meansq(k, lane) + EPS) * gak_ref[...]
        k = _rope_group(k, cosa_ref[...], sina_ref[...], lane, 16)
        for kv in range(A_KV_HEADS):
            sl = slice(kv * LANES, (kv + 1) * LANES)
            ka_ref[:, sl] = _dup_half(k, lane, kv == 0).astype(ka_ref.dtype)
            va_ref[:, sl] = _dup_half(v, lane, kv == 0).astype(va_ref.dtype)

    def chain_b():
        y = proj(F_B, 640)
        yield
        cq = _rms(y[:, :B_Q_RANK], gbq_ref[...]).astype(BF16)
        ckv = _rms(y[:, B_Q_RANK:2 * B_Q_RANK], gbkv_ref[...]).astype(BF16)
        q = _dot(cq, wq_ref[...])
        k = _dot(ckv, wk_ref[...])
        v = _dot(ckv, wv_ref[...])
        kr = _dot(y[:, 2 * B_Q_RANK:].astype(BF16), e_ref[...])
        yield
        cos, sin = cosb_ref[...], sinb_ref[...]
        kr = _rope_group(kr, cos, sin, lane, 8)
        scale = (B_NOPE_DIM + B_ROPE_DIM) ** -0.5
        for hd in range(B_HEADS):
            sl = slice(hd * LANES, (hd + 1) * LANES)
            qb_ref[:, sl] = (_rope_group(q[:, sl], cos, sin, lane, 8) * scale).astype(qb_ref.dtype)
            kb_ref[:, sl] = (k[:, sl] + kr).astype(kb_ref.dtype)
        vb_ref[...] = v.astype(vb_ref.dtype)

    def chain_plain(o_ref, off, c0, cw):
        y = proj(off + c0, cw)
        yield
        o_ref[:, c0:c0 + cw] = y.astype(o_ref.dtype)

    chains = [chain_aq(), chain_akv(), chain_b()]
    chains += [chain_plain(c_ref, F_C, c0, 512) for c0 in range(0, 2048, 512)]
    chains += [chain_plain(dec_ref, F_DEC, 0, LANES)]
    chains += [chain_plain(gate_ref, F_GATE, c0, 512) for c0 in range(0, N_BRANCH * D_MODEL, 512)]
    _run_skewed(chains)


def _fused_in(x2, g, w_packed, gaq, gak, cos_a, sin_a, gbq, gbkv, wq, wk, wv, e_mat, cos_b, sin_b, tm, t_len):
    n = x2.shape[0]
    nt = t_len // tm
    widths = (512, 256, 256, 1024, 1024, 512, 2048, LANES, N_BRANCH * D_MODEL)
    table = pl.BlockSpec((tm, LANES), lambda i: (i % nt, 0))
    return pl.pallas_call(
        _fused_in_kernel,
        grid=(n // tm,),
        in_specs=[pl.BlockSpec((tm, D_MODEL), lambda i: (i, 0)),
                  _const_spec((1, D_MODEL)),
                  _const_spec((D_MODEL, F_WIDTH)),
                  _const_spec((1, LANES)), _const_spec((1, LANES)), table, table,
                  _const_spec((1, B_Q_RANK)), _const_spec((1, B_KV_RANK)),
                  _const_spec((B_Q_RANK, B_HEADS * LANES)),
                  _const_spec((B_KV_RANK, B_HEADS * LANES)),
                  _const_spec((B_KV_RANK, B_HEADS * B_V_DIM)),
                  _const_spec((LANES, LANES)), table, table],
        out_specs=[pl.BlockSpec((tm, w), lambda i: (i, 0)) for w in widths],
        out_shape=[jax.ShapeDtypeStruct((n, w), BF16) for w in widths],
        compiler_params=_cparams(("parallel",)),
        name="fused_in",
    )(x2, g, w_packed, gaq, gak, cos_a, sin_a, gbq, gbkv, wq, wk, wv, e_mat, cos_b, sin_b)


def _run_skewed(chains):
    pending, live = list(chains), []
    while pending or live:
        if pending:
            live.append(pending.pop(0))
        for g in reversed(list(live)):
            try:
                next(g)
            except StopIteration:
                live.remove(g)


def _attend(q, k, v3_ref, out):
    s = _dot_nt(q(), k())
    yield
    m = jnp.max(s, axis=-1, keepdims=True)
    p = jnp.exp(s - m).astype(BF16)
    r = _dot(p, v3_ref[...])
    yield
    out.append(r[:, :LANES] / r[:, LANES:])


def _attn_a_kernel(q_ref, k_ref, v_ref, o_ref, v3_ref):
    tq = q_ref.shape[1]
    n_kv = v3_ref.shape[0]

    @pl.when(pl.program_id(2) == 0)
    def _():
        for kv in range(n_kv):
            v3_ref[kv, :, :LANES] = v_ref[0, :, kv * LANES:(kv + 1) * LANES]
            v3_ref[kv, :, LANES:] = jnp.ones((v_ref.shape[1], LANES), BF16)

    lane = lax.broadcasted_iota(jnp.int32, (tq, LANES), 1)
    first = lane < 64
    outs, chains = [], []
    for gi in range(2 * n_kv):
        kv = gi // 2
        k = lambda kv=kv: k_ref[0, :, kv * LANES:(kv + 1) * LANES]
        for keep_first in (True, False):
            def q(gi=gi, keep_first=keep_first):
                qp = q_ref[0, :, gi * LANES:(gi + 1) * LANES].astype(F32)
                return jnp.where(first == keep_first, qp, 0.0).astype(BF16)
            chains.append(_attend(q, k, v3_ref.at[kv], outs))
    _run_skewed(chains)
    for gi in range(2 * n_kv):
        o_ref[0, :, gi * LANES:(gi + 1) * LANES] = jnp.where(first, outs[2 * gi], outs[2 * gi + 1]).astype(o_ref.dtype)


def _attn_a(q, k2, v2, bsz, t_len, tq, n_kv=2):
    q3, k3, v3 = (a.reshape(bsz, t_len, a.shape[-1]) for a in (q, k2, v2))
    out = pl.pallas_call(
        _attn_a_kernel,
        grid=(bsz, A_KV_HEADS // n_kv, t_len // tq),
        in_specs=[pl.BlockSpec((1, tq, 256 * n_kv), lambda b, g, i: (b, i, g)),
                  pl.BlockSpec((1, t_len, LANES * n_kv), lambda b, g, i: (b, 0, g)),
                  pl.BlockSpec((1, t_len, LANES * n_kv), lambda b, g, i: (b, 0, g))],
        out_specs=pl.BlockSpec((1, tq, 256 * n_kv), lambda b, g, i: (b, i, g)),
        out_shape=jax.ShapeDtypeStruct((bsz, t_len, 512), BF16),
        scratch_shapes=[pltpu.VMEM((n_kv, t_len, 2 * LANES), BF16)],
        compiler_params=_cparams(("parallel", "parallel", "arbitrary")),
        name="attn_a",
    )(q3, k3, v3)
    return out.reshape(bsz * t_len, 512)


def _attn_b_kernel(q_ref, k_ref, v_ref, o_ref, v3_ref):
    tq = q_ref.shape[1]
    n_pairs = v3_ref.shape[0]

    @pl.when(pl.program_id(2) == 0)
    def _():
        for pi in range(n_pairs):
            v3_ref[pi, :, :LANES] = v_ref[0, :, pi * LANES:(pi + 1) * LANES]
            v3_ref[pi, :, LANES:] = jnp.ones((v_ref.shape[1], LANES), BF16)

    lane = lax.broadcasted_iota(jnp.int32, (tq, LANES), 1)
    outs, chains = [], []
    for hd in range(2 * n_pairs):
        sl = slice(hd * LANES, (hd + 1) * LANES)
        q = lambda sl=sl: q_ref[0, :, sl]
        k = lambda sl=sl: k_ref[0, :, sl]
        chains.append(_attend(q, k, v3_ref.at[hd // 2], outs))
    _run_skewed(chains)
    for pi in range(n_pairs):
        o_ref[0, :, pi * LANES:(pi + 1) * LANES] = jnp.where(lane < 64, outs[2 * pi], outs[2 * pi + 1]).astype(o_ref.dtype)


def _attn_b(q, k, v, bsz, t_len, tq, n_pairs=4):
    q3, k3, v3 = (a.reshape(bsz, t_len, a.shape[-1]) for a in (q, k, v))
    out = pl.pallas_call(
        _attn_b_kernel,
        grid=(bsz, B_HEADS // (2 * n_pairs), t_len // tq),
        in_specs=[pl.BlockSpec((1, tq, 256 * n_pairs), lambda b, p, i: (b, i, p)),
                  pl.BlockSpec((1, t_len, 256 * n_pairs), lambda b, p, i: (b, 0, p)),
                  pl.BlockSpec((1, t_len, LANES * n_pairs), lambda b, p, i: (b, 0, p))],
        out_specs=pl.BlockSpec((1, tq, LANES * n_pairs), lambda b, p, i: (b, i, p)),
        out_shape=jax.ShapeDtypeStruct((bsz, t_len, 512), BF16),
        scratch_shapes=[pltpu.VMEM((n_pairs, t_len, 2 * LANES), BF16)],
        compiler_params=_cparams(("parallel", "parallel", "arbitrary")),
        name="attn_b",
    )(q3, k3, v3)
    return out.reshape(bsz * t_len, 512)


def _log_sigmoid(z):
    return jnp.minimum(z, 0.0) - jnp.log(1.0 + jnp.exp(-jnp.abs(z)))


def _dot3(a, b):
    ah, al = _split_bf16(a)
    bh, bl = _split_bf16(b)
    return _dot(ah, bh) + (_dot(ah, bl) + _dot(al, bh))


def _gla_kernel(q_ref, k_ref, v_ref, g_ref, dec_ref, wf_ref, wb_ref, bf_ref, bb_ref, gco_ref,
                o_ref, acc_ref, st_ref, *, tile):
    t_len = q_ref.shape[1]
    n_tiles = t_len // tile
    n_chunks = tile // C_CHUNK
    row = lax.broadcasted_iota(jnp.int32, (tile, tile), 0)
    col = lax.broadcasted_iota(jnp.int32, (tile, tile), 1)
    same = (row // C_CHUNK) == (col // C_CHUNK)
    row_chunk = lax.broadcasted_iota(jnp.int32, (tile, LANES), 0) // C_CHUNK
    keep_f = same & (col <= row)
    keep_b = same & (col >= row)
    lower = jnp.where(keep_f, 1.0, 0.0).astype(BF16)
    upper = jnp.where(keep_b, 1.0, 0.0).astype(BF16)
    q_scale = C_DK ** -0.5
    st_ref[...] = jnp.zeros_like(st_ref)

    n_heads = q_ref.shape[2] // LANES

    def tile_step(j, d, hd):
        forward = d == 0
        w_ref, b_ref = (wf_ref, bf_ref) if forward else (wb_ref, bb_ref)
        cum, keep = (lower, keep_f) if forward else (upper, keep_b)
        rows = pl.ds(pl.multiple_of(j * tile, tile), tile)
        ls = slice(hd * LANES, (hd + 1) * LANES)
        si = d * n_heads + hd
        q = q_ref[0, rows, ls].astype(F32) * q_scale
        k = k_ref[0, rows, ls].astype(F32)
        vb = v_ref[0, rows, ls]
        z = _dot(dec_ref[0, rows, :], w_ref[:, ls])
        yield
        la = _log_sigmoid(z + b_ref[:, ls]) * (1.0 / C_TAU)
        r = _dot(cum, jnp.concatenate(_split_bf16(la), axis=1))
        yield
        b = r[:, :LANES] + r[:, LANES:]
        b3 = b.reshape(n_chunks, C_CHUNK, LANES)
        edge = b3[:, C_CHUNK - 1:, :] if forward else b3[:, :1, :]
        tot = jnp.broadcast_to(edge, b3.shape).reshape(tile, LANES)
        qf = q * jnp.exp(b)
        qt = qf.astype(BF16)
        kt = (k * jnp.exp(-b)).astype(BF16)
        k2 = k * jnp.exp(tot - b)
        raw = _dot_nt(qt, kt)
        k2_blk = jnp.concatenate([jnp.where(row_chunk == c, k2, 0.0).astype(BF16) for c in range(n_chunks)], axis=1)
        kv = _dot_tn(vb, k2_blk)
        yield
        intra = _dot(jnp.where(keep, raw, 0.0).astype(BF16), vb)
        dec = jnp.exp(edge)
        q_blk = jnp.concatenate([jnp.where(row_chunk == c, qf, 0.0).astype(BF16) for c in range(n_chunks)], axis=1)
        st = st_ref[si]
        entering = [None] * n_chunks
        for c in (range(n_chunks) if forward else range(n_chunks - 1, -1, -1)):
            entering[c] = st.astype(BF16)
            st = st * dec[c] + kv[:, c * LANES:(c + 1) * LANES]
        st_ref[si] = st
        inter = _dot_nt(q_blk, jnp.concatenate(entering, axis=1))
        yield
        acc_ref[d, rows, ls] = intra + inter

    def body(it, carry):
        _run_skewed([tile_step(it if d == 0 else n_tiles - 1 - it, d, hd)
                     for hd in range(n_heads) for d in (0, 1)])
        return carry

    lax.fori_loop(0, n_tiles, body, 0)
    for hd in range(n_heads):
        ls = slice(hd * LANES, (hd + 1) * LANES)
        y = _rms(acc_ref[0, :, ls] + acc_ref[1, :, ls], gco_ref[...])
        gate = g_ref[0, :, ls].astype(F32)
        o_ref[0, :, ls] = (y * (gate * _sigmoid(gate))).astype(o_ref.dtype)


def _gla(cqkvg, dec, wf, wb, bf, bb, gco, bsz, t_len, tile, hp=4):
    c3 = cqkvg.reshape(bsz, t_len, 4 * C_HEADS * LANES)
    d3 = dec.reshape(bsz, t_len, LANES)
    groups = C_HEADS // hp
    wl = hp * LANES

    def cspec(j):
        return pl.BlockSpec((1, t_len, wl), lambda b, h: (b, 0, j * groups + h))

    out = pl.pallas_call(
        functools.partial(_gla_kernel, tile=tile),
        grid=(bsz, groups),
        in_specs=[cspec(0), cspec(1), cspec(2), cspec(3),
                  pl.BlockSpec((1, t_len, LANES), lambda b, h: (b, 0, 0)),
                  pl.BlockSpec((LANES, wl), lambda b, h: (0, h)),
                  pl.BlockSpec((LANES, wl), lambda b, h: (0, h)),
                  pl.BlockSpec((1, wl), lambda b, h: (0, h)),
                  pl.BlockSpec((1, wl), lambda b, h: (0, h)),
                  pl.BlockSpec((1, LANES), lambda b, h: (0, 0))],
        out_specs=pl.BlockSpec((1, t_len, wl), lambda b, h: (b, 0, h)),
        out_shape=jax.ShapeDtypeStruct((bsz, t_len, C_HEADS * LANES), BF16),
        scratch_shapes=[pltpu.VMEM((2, t_len, wl), F32), pltpu.VMEM((2 * hp, C_DV, C_DK), F32)],
        compiler_params=_cparams(("parallel", "parallel")),
        name="gla",
    )(c3, c3, c3, c3, d3, wf, wb, bf, bb, gco)
    return out.reshape(bsz * t_len, C_HEADS * LANES)


def _merge_kernel(x_ref, ya_ref, yb_ref, yc_ref, gate_ref, bg_ref, wpa_ref, wpb_ref, wpc_ref, wo_ref, o_ref):
    m = None
    for j, (y_ref, w_ref) in enumerate(((ya_ref, wpa_ref), (yb_ref, wpb_ref), (yc_ref, wpc_ref))):
        sl = slice(j * D_MODEL, (j + 1) * D_MODEL)
        gb = _sigmoid(gate_ref[:, sl].astype(F32) + bg_ref[:, sl])
        term = gb * _dot(y_ref[...], w_ref[...])
        m = term if m is None else m + term
    o_ref[...] = x_ref[...] + _dot(m.astype(BF16), wo_ref[...])


def _merge(x2, ya, yb, yc, gates, b_gate, wpa, wpb, wpc, wo, tm):
    n = x2.shape[0]
    return pl.pallas_call(
        _merge_kernel,
        grid=(n // tm,),
        in_specs=[pl.BlockSpec((tm, D_MODEL), lambda i: (i, 0)),
                  pl.BlockSpec((tm, 512), lambda i: (i, 0)),
                  pl.BlockSpec((tm, 512), lambda i: (i, 0)),
                  pl.BlockSpec((tm, 512), lambda i: (i, 0)),
                  pl.BlockSpec((tm, N_BRANCH * D_MODEL), lambda i: (i, 0)),
                  _const_spec((1, N_BRANCH * D_MODEL)),
                  _const_spec((512, D_MODEL)), _const_spec((512, D_MODEL)), _const_spec((512, D_MODEL)),
                  _const_spec((D_MODEL, D_MODEL))],
        out_specs=pl.BlockSpec((tm, D_MODEL), lambda i: (i, 0)),
        out_shape=jax.ShapeDtypeStruct((n, D_MODEL), F32),
        compiler_params=_cparams(("parallel",)),
        name="merge",
    )(x2, ya, yb, yc, gates, b_gate, wpa, wpb, wpc, wo)


def _ffn_kernel(x_ref, g_ref, wg_ref, wu_ref, wd_ref, o_ref, *, chunk):
    x = x_ref[...]
    h = _rms(x, g_ref[...]).astype(BF16)
    o_ref[...] = x
    for c0 in range(0, D_FF, chunk):
        gt = _dot(h, wg_ref[:, c0:c0 + chunk])
        up = _dot(h, wu_ref[:, c0:c0 + chunk])
        a = (gt * _sigmoid(gt) * up).astype(BF16)
        o_ref[...] += _dot(a, wd_ref[c0:c0 + chunk, :])


def _ffn(x2, g, wg, wu, wd, tm, chunk=256):
    n = x2.shape[0]
    return pl.pallas_call(
        functools.partial(_ffn_kernel, chunk=chunk),
        grid=(n // tm,),
        in_specs=[pl.BlockSpec((tm, D_MODEL), lambda i: (i, 0)),
                  _const_spec((1, D_MODEL)),
                  _const_spec((D_MODEL, D_FF)), _const_spec((D_MODEL, D_FF)), _const_spec((D_FF, D_MODEL))],
        out_specs=pl.BlockSpec((tm, D_MODEL), lambda i: (i, 0)),
        out_shape=jax.ShapeDtypeStruct((n, D_MODEL), F32),
        compiler_params=_cparams(("parallel",)),
        name="ffn",
    )(x2, g, wg, wu, wd)


def _router_kernel(x_ref, g_ref, wr_ref, h_ref, gate_ref, rank_ref, rank_t_ref, cnt_ref, tot_ref, carry_ref):
    tm = x_ref.shape[0]

    @pl.when(pl.program_id(0) == 0)
    def _():
        carry_ref[...] = jnp.zeros_like(carry_ref)

    hf = _rms(x_ref[...], g_ref[...])
    h_ref[...] = hf.astype(h_ref.dtype)
    logits = _dot3(hf, wr_ref[...])
    lane = lax.broadcasted_iota(jnp.int32, logits.shape, 1).astype(F32)
    neg = jnp.float32(-jnp.inf)
    logits = jnp.where(lane < N_EXPERTS, logits, neg)
    v1 = jnp.max(logits, axis=-1, keepdims=True)
    i1 = jnp.min(jnp.where(logits == v1, lane, float(LANES)), axis=-1, keepdims=True)
    rest = jnp.where(lane == i1, neg, logits)
    v2 = jnp.max(rest, axis=-1, keepdims=True)
    i2 = jnp.min(jnp.where(rest == v2, lane, float(LANES)), axis=-1, keepdims=True)
    e2 = jnp.exp(v2 - v1)
    w1 = 1.0 / (1.0 + e2)
    w2 = e2 / (1.0 + e2)
    gate_ref[...] = jnp.where(lane == i1, w1, 0.0) + jnp.where(lane == i2, w2, 0.0)

    chosen = (lane == i1) | (lane == i2)
    assign = jnp.where(chosen, 1.0, 0.0)
    row = lax.broadcasted_iota(jnp.int32, (tm, tm), 0)
    col = lax.broadcasted_iota(jnp.int32, (tm, tm), 1)
    earlier = jnp.where(col < row, 1.0, 0.0).astype(BF16)
    carry = carry_ref[...]
    before = _dot(earlier, assign.astype(BF16)) + carry[0:1, :]
    rank = jnp.where(chosen, before, -1.0)
    rank_ref[...] = rank
    rank_t_ref[...] = rank.T[:N_EXPERTS, :]
    cnt_ref[0] = carry
    carry = carry + jnp.sum(assign, axis=0, keepdims=True)
    carry_ref[...] = carry
    tot_ref[...] = carry


def _router(x2, g, wr_pad, tm):
    n = x2.shape[0]
    nt = n // tm
    return pl.pallas_call(
        _router_kernel,
        grid=(nt,),
        in_specs=[pl.BlockSpec((tm, D_MODEL), lambda i: (i, 0)),
                  _const_spec((1, D_MODEL)),
                  _const_spec((D_MODEL, LANES))],
        out_specs=[pl.BlockSpec((tm, D_MODEL), lambda i: (i, 0)),
                   pl.BlockSpec((tm, LANES), lambda i: (i, 0)),
                   pl.BlockSpec((tm, LANES), lambda i: (i, 0)),
                   pl.BlockSpec((N_EXPERTS, tm), lambda i: (0, i)),
                   pl.BlockSpec((1, 8, LANES), lambda i: (i, 0, 0)),
                   pl.BlockSpec((8, LANES), lambda i: (0, 0))],
        out_shape=[jax.ShapeDtypeStruct((n, D_MODEL), BF16),
                   jax.ShapeDtypeStruct((n, LANES), F32),
                   jax.ShapeDtypeStruct((n, LANES), F32),
                   jax.ShapeDtypeStruct((N_EXPERTS, n), F32),
                   jax.ShapeDtypeStruct((nt, 8, LANES), F32),
                   jax.ShapeDtypeStruct((8, LANES), F32)],
        scratch_shapes=[pltpu.VMEM((8, LANES), F32)],
        compiler_params=_cparams(("arbitrary",)),
        name="router",
    )(x2, g, wr_pad)


def _moe_plan(cnt, tot, n, tb, rs, tg):
    i32 = jnp.int32
    counts = tot[0, :N_EXPERTS].astype(i32)
    cum = jnp.concatenate([cnt[:, 0, :N_EXPERTS], tot[:1, :N_EXPERTS]], axis=0).astype(i32)
    n_sb = (counts + rs - 1) // rs
    sb_end = jnp.cumsum(n_sb)
    sb_start = sb_end - n_sb
    goff = sb_start * rs
    n_valid_sb = sb_end[-1]
    s_max = 2 * n // rs + N_EXPERTS
    s_ids = jnp.arange(s_max, dtype=i32)
    last_sb = n_valid_sb - 1
    s_eff = jnp.minimum(s_ids, last_sb)
    sb_expert = jnp.minimum(jnp.sum(s_eff[:, None] >= sb_end[None, :], axis=1), N_EXPERTS - 1).astype(i32)
    sb_rows = jnp.clip(counts[sb_expert] - (s_eff - sb_start[sb_expert]) * rs, 0, rs)
    sb_tiles = ((sb_rows + tb - 1) // tb).astype(i32)

    t_ids = jnp.arange(s_max * rs // tb, dtype=i32)
    t_sb = t_ids * tb // rs
    t_exp = sb_expert[jnp.minimum(t_sb, last_sb)]
    t_r0 = t_ids * tb - goff[t_exp]
    t_valid = (t_sb <= last_sb) & (t_r0 < counts[t_exp])
    t_r1 = jnp.minimum(t_r0 + tb, counts[t_exp])
    cum_e = cum[:, t_exp]
    t_lo = jnp.sum(cum_e[1:] <= t_r0[None, :], axis=0)
    t_hi = jnp.sum(cum_e[:-1] < t_r1[None, :], axis=0)
    t_last = (last_sb + 1) * (rs // tg) - 1
    gather_tabs = tuple(a.astype(i32) for a in (t_exp, t_r0, t_lo, t_hi, t_valid, t_last.reshape(1)))

    start = goff[None, :] + cum[:-1]
    num = cum[1:] - cum[:-1]
    last_blk = (last_sb + 1) * (rs // tb) - 1
    b0 = jnp.minimum(start // tb, last_blk)
    b1 = jnp.minimum((start + jnp.maximum(num, 1) - 1) // tb, last_blk)
    b1 = jnp.where(b1 == b0, jnp.where(b0 < last_blk, b0 + 1, b0 - 1), b1)
    blk = jnp.stack([b0, b1], axis=-1).reshape(-1).astype(i32)
    expert_tabs = (sb_expert, sb_tiles, last_sb.reshape(1).astype(i32))
    return gather_tabs, expert_tabs, (blk, goff.astype(i32))


def _gather_kernel(exp_ref, r0_ref, lo_ref, hi_ref, valid_ref, last_ref, rank_ref, h_ref, o_ref, acc_ref, *, tb):
    step = pl.program_id(0)
    n_sub = o_ref.shape[0] // tb
    n_blk = rank_ref.shape[1]
    row = lax.broadcasted_iota(jnp.int32, (tb, tb), 0)

    @pl.when(step > last_ref[0])
    def _():
        o_ref[...] = jnp.zeros_like(o_ref)

    for u in range(n_sub):
        t = step * n_sub + u
        rows = slice(u * tb, (u + 1) * tb)
        in_range = step <= last_ref[0]

        @pl.when(in_range & (valid_ref[t] == 0))
        def _(rows=rows):
            o_ref[rows, :] = jnp.zeros((tb, D_MODEL), o_ref.dtype)

        @pl.when(in_range & (valid_ref[t] == 1))
        def _(rows=rows, t=t):
            e = exp_ref[t]
            want = row + r0_ref[t]
            lo, hi = lo_ref[t], hi_ref[t]
            acc_ref[rows, :] = jnp.zeros((tb, D_MODEL), F32)

            def onehot_rows(kb, live):
                rk = rank_ref[e, pl.ds(kb, 1), :].astype(jnp.int32)
                rk = jnp.where(live, rk, -1)
                onehot = jnp.where(rk == want, 1.0, 0.0).astype(BF16)
                return _dot(onehot, h_ref[pl.ds(pl.multiple_of(kb * tb, tb), tb), :])

            def body(p, carry):
                kb = lo + 2 * p
                kb2 = jnp.minimum(kb + 1, n_blk - 1)
                acc_ref[rows, :] += onehot_rows(kb, True) + onehot_rows(kb2, kb + 1 < hi)
                return carry

            lax.fori_loop(0, (hi - lo + 1) // 2, body, 0)
            o_ref[rows, :] = acc_ref[rows, :].astype(o_ref.dtype)


def _gather(tabs, rank_t3, h, n_rows, tg, tb):
    n = h.shape[0]
    return pl.pallas_call(
        functools.partial(_gather_kernel, tb=tb),
        grid_spec=pltpu.PrefetchScalarGridSpec(
            num_scalar_prefetch=6,
            grid=(n_rows // tg,),
            in_specs=[_const_spec((N_EXPERTS, n // tb, tb)),
                      _const_spec((n, D_MODEL))],
            out_specs=pl.BlockSpec((tg, D_MODEL), lambda t, e, r, lo, hi, v, last: (t, 0)),
            scratch_shapes=[pltpu.VMEM((tg, D_MODEL), F32)]),
        out_shape=jax.ShapeDtypeStruct((n_rows, D_MODEL), BF16),
        compiler_params=_cparams(("arbitrary",)),
        name="moe_gather",
    )(*tabs, rank_t3, h)


def _experts_kernel(exp_ref, tiles_ref, last_ref, xs_ref, wg_ref, wu_ref, wd_ref, o_ref,
                    acc_ref, wgb_ref, wub_ref, wdb_ref, *, tb):
    s, c = pl.program_id(0), pl.program_id(1)
    nc = pl.num_programs(1)

    @pl.when((s > last_ref[0]) & (c == 0))
    def _():
        o_ref[...] = jnp.zeros_like(o_ref)

    @pl.when(s <= last_ref[0])
    def _():
        @pl.when(c == 0)
        def _():
            acc_ref[...] = jnp.zeros_like(acc_ref)

        wgb_ref[...] = wg_ref[0].astype(BF16)
        wub_ref[...] = wu_ref[0].astype(BF16)
        wdb_ref[...] = wd_ref[0].astype(BF16)

        def tile_step(t):
            rows = pl.ds(pl.multiple_of(t * tb, tb), tb)
            x = xs_ref[rows, :]
            gt = _dot(x, wgb_ref[...])
            up = _dot(x, wub_ref[...])
            yield
            a = (gt * _sigmoid(gt) * up).astype(BF16)
            y = _dot(a, wdb_ref[...])
            yield
            acc_ref[rows, :] += y

        def body(t2, carry):
            _run_skewed([tile_step(2 * t2), tile_step(2 * t2 + 1)])
            return carry

        n_tiles = tiles_ref[s]
        lax.fori_loop(0, n_tiles >> 1, body, 0)

        @pl.when((n_tiles & 1) == 1)
        def _():
            _run_skewed([tile_step(n_tiles - 1)])

        @pl.when(c == nc - 1)
        def _():
            o_ref[...] = acc_ref[...].astype(o_ref.dtype)


def _experts(tabs, xs, wg, wu, wd, rs, tb, fc):
    n_rows = xs.shape[0]
    nc = D_FF_EXPERT // fc

    def sb(s, last):
        return jnp.minimum(s, last[0])

    def chunk(s, c, last):
        return jnp.where(s <= last[0], c, nc - 1)

    return pl.pallas_call(
        functools.partial(_experts_kernel, tb=tb),
        grid_spec=pltpu.PrefetchScalarGridSpec(
            num_scalar_prefetch=3,
            grid=(n_rows // rs, nc),
            in_specs=[pl.BlockSpec((rs, D_MODEL), lambda s, c, ex, tl, last: (sb(s, last), 0)),
                      pl.BlockSpec((1, D_MODEL, fc), lambda s, c, ex, tl, last: (ex[sb(s, last)], 0, chunk(s, c, last))),
                      pl.BlockSpec((1, D_MODEL, fc), lambda s, c, ex, tl, last: (ex[sb(s, last)], 0, chunk(s, c, last))),
                      pl.BlockSpec((1, fc, D_MODEL), lambda s, c, ex, tl, last: (ex[sb(s, last)], chunk(s, c, last), 0))],
            out_specs=pl.BlockSpec((rs, D_MODEL), lambda s, c, ex, tl, last: (s, 0)),
            scratch_shapes=[pltpu.VMEM((rs, D_MODEL), F32),
                            pltpu.VMEM((D_MODEL, fc), BF16), pltpu.VMEM((D_MODEL, fc), BF16),
                            pltpu.VMEM((fc, D_MODEL), BF16)]),
        out_shape=jax.ShapeDtypeStruct((n_rows, D_MODEL), BF16),
        compiler_params=_cparams(("arbitrary", "arbitrary")),
        name="moe_experts",
    )(*tabs, xs, wg, wu, wd)


def _combine_kernel(blk_ref, goff_ref, x_ref, gate_ref, rank_ref, *rest, final_norm):
    y_refs, (gfin_ref, o_ref) = rest[:2 * N_EXPERTS], rest[2 * N_EXPERTS:]
    i = pl.program_id(0)
    tt = x_ref.shape[0]
    tb = y_refs[0].shape[0]
    lane = lax.broadcasted_iota(jnp.int32, (tt, LANES), 1)
    col = lax.broadcasted_iota(jnp.int32, (tt, tb), 1)
    gate, rank = gate_ref[...], rank_ref[...]
    y = x_ref[...]
    for e in range(N_EXPERTS):
        ge = jnp.sum(jnp.where(lane == e, gate, 0.0), axis=-1, keepdims=True)
        rk = jnp.sum(jnp.where(lane == e, rank, 0.0), axis=-1, keepdims=True).astype(jnp.int32)
        pos = jnp.where(rk >= 0, rk + goff_ref[e], -1)
        parts = [_dot(jnp.where(pos - blk_ref[(i * N_EXPERTS + e) * 2 + j] * tb == col, 1.0, 0.0).astype(BF16),
                      y_refs[2 * e + j][...]) for j in range(2)]
        y = y + ge * (parts[0] + parts[1])
    o_ref[...] = _rms(y, gfin_ref[...]) if final_norm else y


def _combine(tabs, x2, gate, rank, ys, gfin, tt, tb, final_norm):
    n = x2.shape[0]

    def yspec(slot):
        return pl.BlockSpec((tb, D_MODEL), lambda i, blk, goff: (blk[i * 2 * N_EXPERTS + slot], 0))

    tok = lambda w: pl.BlockSpec((tt, w), lambda i, blk, goff: (i, 0))
    return pl.pallas_call(
        functools.partial(_combine_kernel, final_norm=final_norm),
        grid_spec=pltpu.PrefetchScalarGridSpec(
            num_scalar_prefetch=2,
            grid=(n // tt,),
            in_specs=[tok(D_MODEL), tok(LANES), tok(LANES)] + [yspec(k) for k in range(2 * N_EXPERTS)]
                     + [pl.BlockSpec((1, D_MODEL), lambda i, blk, goff: (0, 0))],
            out_specs=tok(D_MODEL)),
        out_shape=jax.ShapeDtypeStruct((n, D_MODEL), F32),
        compiler_params=_cparams(("arbitrary",)),
        name="moe_combine",
    )(*tabs, x2, gate, rank, *([ys] * (2 * N_EXPERTS)), gfin)


def _moe(x2, g_ffn, w_router, wg, wu, wd, gfin, final_norm):
    n = x2.shape[0]
    rs = min(2048, n)
    tb = min(256, rs // 2)
    tg = min(512, rs)
    wr = jnp.pad(w_router, ((0, 0), (0, LANES - N_EXPERTS)))
    h, gate, rank, rank_t, cnt, tot = _router(x2, g_ffn, wr, tb)
    gather_tabs, expert_tabs, combine_tabs = _moe_plan(cnt, tot, n, tb, rs, tg)
    n_rows = (2 * n // rs + N_EXPERTS) * rs
    xs = _gather(gather_tabs, rank_t.reshape(N_EXPERTS, n // tb, tb), h, n_rows, tg, tb)
    ys = _experts(expert_tabs, xs, wg, wu, wd, rs, tb, 512)
    return _combine(combine_tabs, x2, gate, rank, ys, gfin, tb, tb, final_norm)


def _final_norm_kernel(x_ref, g_ref, o_ref):
    o_ref[...] = _rms(x_ref[...], g_ref[...])


def _final_norm(x2, g, tm):
    n = x2.shape[0]
    return pl.pallas_call(
        _final_norm_kernel,
        grid=(n // tm,),
        in_specs=[pl.BlockSpec((tm, D_MODEL), lambda i: (i, 0)), _const_spec((1, D_MODEL))],
        out_specs=pl.BlockSpec((tm, D_MODEL), lambda i: (i, 0)),
        out_shape=jax.ShapeDtypeStruct((n, D_MODEL), F32),
        compiler_params=_cparams(("parallel",)),
        name="final_norm",
    )(x2, g)


def _rope_tables(t_len):
    rows = t_len // GRID_W
    row = np.repeat(np.arange(rows, dtype=np.float64), GRID_W)
    col = np.tile(np.arange(GRID_W, dtype=np.float64), rows)
    lane = np.arange(LANES)

    def table(rot_dim, lane_in_slice, active):
        n_freq = rot_dim // 4
        inv_freq = ROPE_THETA ** (-np.arange(n_freq, dtype=np.float64) / n_freq)
        freq = inv_freq[lane_in_slice % n_freq]
        use_row = (lane_in_slice % rot_dim) < (rot_dim // 2)
        ang = np.where(use_row[None, :], row[:, None], col[:, None]) * freq[None, :]
        sign = np.where((lane_in_slice % (rot_dim // 2)) < n_freq, -1.0, 1.0)
        cos = np.where(active[None, :], np.cos(ang), 1.0)
        sin = np.where(active[None, :], np.sin(ang) * sign[None, :], 0.0)
        return jnp.asarray(cos, F32), jnp.asarray(sin, F32)

    cos_a, sin_a = table(A_HEAD_DIM, lane % A_HEAD_DIM, np.ones(LANES, bool))
    in_rope = (lane >= B_NOPE_DIM) & (lane < B_NOPE_DIM + B_ROPE_DIM)
    cos_b, sin_b = table(B_ROPE_DIM, (lane - B_NOPE_DIM) % B_ROPE_DIM, in_rope)
    return cos_a, sin_a, cos_b, sin_b


def _pack_w_in(w):
    o_c, o_dec, o_gate = 1312, 3360, 3392
    zeros = lambda k: jnp.zeros((D_MODEL, k), w.dtype)
    cols = [w[:, :o_c], zeros(LANES - B_ROPE_DIM), w[:, o_c:o_gate], zeros(LANES - 2 * C_GATE_RANK), w[:, o_gate:]]
    return jnp.concatenate(cols, axis=1).astype(BF16)


def _pack_b_weights(w_q_up, w_kv_up):
    hq = w_q_up.reshape(B_Q_RANK, B_HEADS, B_NOPE_DIM + B_ROPE_DIM)
    wq = jnp.pad(hq, ((0, 0), (0, 0), (0, LANES - B_NOPE_DIM - B_ROPE_DIM))).reshape(B_Q_RANK, B_HEADS * LANES)
    hkv = w_kv_up.reshape(B_KV_RANK, B_HEADS, B_NOPE_DIM + B_V_DIM)
    wk = jnp.pad(hkv[:, :, :B_NOPE_DIM], ((0, 0), (0, 0), (0, LANES - B_NOPE_DIM))).reshape(B_KV_RANK, B_HEADS * LANES)
    wv = hkv[:, :, B_NOPE_DIM:].reshape(B_KV_RANK, B_HEADS * B_V_DIM)
    return wq.astype(BF16), wk.astype(BF16), wv.astype(BF16)


def _rope_placement():
    e = np.zeros((LANES, LANES), np.float32)
    e[np.arange(B_ROPE_DIM), B_NOPE_DIM + np.arange(B_ROPE_DIM)] = 1.0
    return jnp.asarray(e, BF16)


def kernel(x, w_in, b_gate, g_mix, g_a_q, g_a_k, g_b_q, w_b_q_up, g_b_kv, w_b_kv_up, w_c_af_up, b_c_af, w_c_ab_up, b_c_ab, g_c_out, w_pa, w_pb, w_pc, w_out, g_ffn, w_ff_gate, w_ff_up, w_ff_down, w_router, w_e_gate, w_e_up, w_e_down, g_final):
    bsz, t_len, _ = x.shape
    n = bsz * t_len
    depth = w_in.shape[0]
    tm = min(512, t_len)
    tq_a = min(512, t_len)
    tq_b = min(512, t_len)
    gla_tile = min(256, t_len)

    cos_a, sin_a, cos_b, sin_b = _rope_tables(t_len)
    e_mat = _rope_placement()
    row = lambda v: v.reshape(1, -1).astype(F32)
    x2 = x.reshape(n, D_MODEL)

    for i in range(depth):
        wq, wk, wv = _pack_b_weights(w_b_q_up[i], w_b_kv_up[i])
        qa, ka, av2, qb, kb, vb, cqkvg, dec, gates = _fused_in(
            x2, row(g_mix[i]), _pack_w_in(w_in[i]), row(jnp.tile(g_a_q[i], 2)), row(jnp.tile(g_a_k[i], 2)),
            cos_a, sin_a, row(g_b_q[i]), row(g_b_kv[i]), wq, wk, wv, e_mat, cos_b, sin_b, tm, t_len)
        ya = _attn_a(qa, ka, av2, bsz, t_len, tq_a)
        yb = _attn_b(qb, kb, vb, bsz, t_len, tq_b)

        wf = jnp.pad(w_c_af_up[i], ((0, LANES - C_GATE_RANK), (0, 0))).astype(BF16)
        wb = jnp.pad(w_c_ab_up[i], ((C_GATE_RANK, LANES - 2 * C_GATE_RANK), (0, 0))).astype(BF16)
        yc = _gla(cqkvg, dec, wf, wb, row(b_c_af[i]), row(b_c_ab[i]), row(g_c_out[i]), bsz, t_len, gla_tile)

        x2 = _merge(x2, ya, yb, yc, gates, row(b_gate[i]), w_pa[i].astype(BF16), w_pb[i].astype(BF16),
                    w_pc[i].astype(BF16), w_out[i].astype(BF16), tm)

        j = i // 2
        last = i == depth - 1
        if i % 2 == 0:
            x2 = _ffn(x2, row(g_ffn[i]), w_ff_gate[j].astype(BF16), w_ff_up[j].astype(BF16),
                      w_ff_down[j].astype(BF16), tm)
            if last:
                x2 = _final_norm(x2, row(g_final), tm)
        else:
            x2 = _moe(x2, row(g_ffn[i]), w_router[j], w_e_gate[j], w_e_up[j], w_e_down[j], row(g_final), last)
    return x2.reshape(bsz, t_len, D_MODEL)
```

```python
import functools

import jax
import jax.numpy as jnp
import numpy as np
from jax import lax
from jax.experimental import pallas as pl
from jax.experimental.pallas import tpu as pltpu

F32 = jnp.float32
BF16 = jnp.bfloat16

D_MODEL = 1024
GRID_W = 64
ROPE_THETA = 10000.0
EPS = 1e-6
A_HEADS, A_KV_HEADS, A_HEAD_DIM = 8, 2, 64
B_HEADS, B_NOPE_DIM, B_ROPE_DIM, B_V_DIM = 8, 64, 32, 64
B_Q_RANK = B_KV_RANK = 256
C_HEADS, C_DK, C_DV, C_GATE_RANK, C_TAU, C_CHUNK = 4, 128, 128, 16, 16.0, 64
N_BRANCH = 3
D_FF = 2816
N_EXPERTS, TOP_K, D_FF_EXPERT = 8, 2, 3584

LANES = 128
VMEM_LIMIT = 56 * 1024 * 1024


def _cparams(sem):
    return pltpu.CompilerParams(dimension_semantics=sem, vmem_limit_bytes=VMEM_LIMIT)


def _const_spec(shape):
    nd = len(shape)
    return pl.BlockSpec(shape, lambda *_: (0,) * nd, pipeline_mode=pl.Buffered(1))


def _rms(xf, g):
    return xf * lax.rsqrt(jnp.mean(xf * xf, axis=-1, keepdims=True) + EPS) * g


def _sigmoid(x):
    return 1.0 / (1.0 + jnp.exp(-x))


def _split_bf16(x):
    hi = x.astype(BF16)
    lo = (x - hi.astype(F32)).astype(BF16)
    return hi, lo


def _dot(a, b):
    return jnp.dot(a, b, preferred_element_type=F32)


def _dot_nt(a, b):
    return lax.dot_general(a, b, (((1,), (1,)), ((), ())), preferred_element_type=F32)


def _dot_tn(a, b):
    return lax.dot_general(a, b, (((0,), (0,)), ((), ())), preferred_element_type=F32)


def _rope_group(x, cos, sin_signed, lane, half):
    fwd = pltpu.roll(x, LANES - half, axis=1)
    bwd = pltpu.roll(x, half, axis=1)
    swapped = jnp.where((lane % (2 * half)) < half, fwd, bwd)
    return x * cos + swapped * sin_signed


def _seg64_meansq(x, lane):
    sq = x * x
    lo = jnp.sum(jnp.where(lane < 64, sq, 0.0), axis=-1, keepdims=True)
    hi = jnp.sum(jnp.where(lane < 64, 0.0, sq), axis=-1, keepdims=True)
    return jnp.where(lane < 64, lo, hi) * (1.0 / 64.0)


F_AQ, F_AKV, F_B, F_C, F_DEC, F_WIDTH = 0, 512, 768, 1408, 3456, 3584


def _dup_half(x, lane, low):
    other = pltpu.roll(x, 64, axis=1)
    return jnp.where(lane < 64, x, other) if low else jnp.where(lane < 64, other, x)


def _fused_in_kernel(x_ref, g_ref, w_ref, gaq_ref, gak_ref, cosa_ref, sina_ref,
                     gbq_ref, gbkv_ref, wq_ref, wk_ref, wv_ref, e_ref, cosb_ref, sinb_ref,
                     qa_ref, ka_ref, va_ref, qb_ref, kb_ref, vb_ref, c_ref, dec_ref):
    rows = x_ref.shape[0]
    lane = lax.broadcasted_iota(jnp.int32, (rows, LANES), 1)
    h = _rms(x_ref[...], g_ref[...]).astype(BF16)

    def proj(off, width):
        return _dot(h, w_ref[:, off:off + width])

    def chain_aq():
        y = proj(F_AQ, 512)
        yield
        cos, sin, gq = cosa_ref[...], sina_ref[...], gaq_ref[...]
        for gi in range(4):
            sl = slice(gi * LANES, (gi + 1) * LANES)
            x = y[:, sl]
            x = x * lax.rsqrt(_seg64_meansq(x, lane) + EPS) * gq
            qa_ref[:, sl] = (_rope_group(x, cos, sin, lane, 16) * (A_HEAD_DIM ** -0.5)).astype(qa_ref.dtype)

    def chain_akv():
        y = proj(F_AKV, 256)
        yield
        k, v = y[:, :LANES], y[:, LANES:]
        k = k * lax.rsqrt(_seg64_meansq(k, lane) + EPS) * gak_ref[...]
        k = _rope_group(k, cosa_ref[...], sina_ref[...], lane, 16)
        for kv in range(A_KV_HEADS):
            sl = slice(kv * LANES, (kv + 1) * LANES)
            ka_ref[:, sl] = _dup_half(k, lane, kv == 0).astype(ka_ref.dtype)
            va_ref[:, sl] = _dup_half(v, lane, kv == 0).astype(va_ref.dtype)

    def chain_b():
        y = proj(F_B, 640)
        yield
        cq = _rms(y[:, :B_Q_RANK], gbq_ref[...]).astype(BF16)
        ckv = _rms(y[:, B_Q_RANK:2 * B_Q_RANK], gbkv_ref[...]).astype(BF16)
        q = _dot(cq, wq_ref[...])
        k = _dot(ckv, wk_ref[...])
        v = _dot(ckv, wv_ref[...])
        kr = _dot(y[:, 2 * B_Q_RANK:].astype(BF16), e_ref[...])
        yield
        cos, sin = cosb_ref[...], sinb_ref[...]
        kr = _rope_group(kr, cos, sin, lane, 8)
        scale = (B_NOPE_DIM + B_ROPE_DIM) ** -0.5
        for hd in range(B_HEADS):
            sl = slice(hd * LANES, (hd + 1) * LANES)
            qb_ref[:, sl] = (_rope_group(q[:, sl], cos, sin, lane, 8) * scale).astype(qb_ref.dtype)
            kb_ref[:, sl] = (k[:, sl] + kr).astype(kb_ref.dtype)
        vb_ref[...] = v.astype(vb_ref.dtype)

    def chain_plain(o_ref, off, c0, cw):
        y = proj(off + c0, cw)
        yield
        o_ref[:, c0:c0 + cw] = y.astype(o_ref.dtype)

    chains = [chain_aq(), chain_akv(), chain_b()]
    chains += [chain_plain(c_ref, F_C, c0, 512) for c0 in range(0, 2048, 512)]
    chains += [chain_plain(dec_ref, F_DEC, 0, LANES)]
    _run_skewed(chains)


def _fused_in(x2, g, w_packed, gaq, gak, cos_a, sin_a, gbq, gbkv, wq, wk, wv, e_mat, cos_b, sin_b, tm, t_len):
    n = x2.shape[0]
    nt = t_len // tm
    widths = (512, 256, 256, 1024, 1024, 512, 2048, LANES)
    table = pl.BlockSpec((tm, LANES), lambda i: (i % nt, 0))
    return pl.pallas_call(
        _fused_in_kernel,
        grid=(n // tm,),
        in_specs=[pl.BlockSpec((tm, D_MODEL), lambda i: (i, 0)),
                  _const_spec((1, D_MODEL)),
                  _const_spec((D_MODEL, F_WIDTH)),
                  _const_spec((1, LANES)), _const_spec((1, LANES)), table, table,
                  _const_spec((1, B_Q_RANK)), _const_spec((1, B_KV_RANK)),
                  _const_spec((B_Q_RANK, B_HEADS * LANES)),
                  _const_spec((B_KV_RANK, B_HEADS * LANES)),
                  _const_spec((B_KV_RANK, B_HEADS * B_V_DIM)),
                  _const_spec((LANES, LANES)), table, table],
        out_specs=[pl.BlockSpec((tm, w), lambda i: (i, 0)) for w in widths],
        out_shape=[jax.ShapeDtypeStruct((n, w), BF16) for w in widths],
        compiler_params=_cparams(("parallel",)),
        name="fused_in",
    )(x2, g, w_packed, gaq, gak, cos_a, sin_a, gbq, gbkv, wq, wk, wv, e_mat, cos_b, sin_b)


def _run_skewed(chains):
    pending, live = list(chains), []
    while pending or live:
        if pending:
            live.append(pending.pop(0))
        for g in reversed(list(live)):
            try:
                next(g)
            except StopIteration:
                live.remove(g)


def _attend(q, k, v3_ref, out):
    s = _dot_nt(q(), k())
    yield
    m = jnp.max(s, axis=-1, keepdims=True)
    p = jnp.exp(s - m).astype(BF16)
    r = _dot(p, v3_ref[...])
    yield
    out.append(r[:, :LANES] / r[:, LANES:])


def _attn_a_kernel(q_ref, k_ref, v_ref, o_ref, v3_ref):
    tq = q_ref.shape[1]
    n_kv = v3_ref.shape[0]

    @pl.when(pl.program_id(2) == 0)
    def _():
        for kv in range(n_kv):
            v3_ref[kv, :, :LANES] = v_ref[0, :, kv * LANES:(kv + 1) * LANES]
            v3_ref[kv, :, LANES:] = jnp.ones((v_ref.shape[1], LANES), BF16)

    lane = lax.broadcasted_iota(jnp.int32, (tq, LANES), 1)
    first = lane < 64
    outs, chains = [], []
    for gi in range(2 * n_kv):
        kv = gi // 2
        k = lambda kv=kv: k_ref[0, :, kv * LANES:(kv + 1) * LANES]
        for keep_first in (True, False):
            def q(gi=gi, keep_first=keep_first):
                qp = q_ref[0, :, gi * LANES:(gi + 1) * LANES].astype(F32)
                return jnp.where(first == keep_first, qp, 0.0).astype(BF16)
            chains.append(_attend(q, k, v3_ref.at[kv], outs))
    _run_skewed(chains)
    for gi in range(2 * n_kv):
        o_ref[0, :, gi * LANES:(gi + 1) * LANES] = jnp.where(first, outs[2 * gi], outs[2 * gi + 1]).astype(o_ref.dtype)


def _attn_a(q, k2, v2, bsz, t_len, tq, n_kv=2):
    q3, k3, v3 = (a.reshape(bsz, t_len, a.shape[-1]) for a in (q, k2, v2))
    out = pl.pallas_call(
        _attn_a_kernel,
        grid=(bsz, A_KV_HEADS // n_kv, t_len // tq),
        in_specs=[pl.BlockSpec((1, tq, 256 * n_kv), lambda b, g, i: (b, i, g)),
                  pl.BlockSpec((1, t_len, LANES * n_kv), lambda b, g, i: (b, 0, g)),
                  pl.BlockSpec((1, t_len, LANES * n_kv), lambda b, g, i: (b, 0, g))],
        out_specs=pl.BlockSpec((1, tq, 256 * n_kv), lambda b, g, i: (b, i, g)),
        out_shape=jax.ShapeDtypeStruct((bsz, t_len, 512), BF16),
        scratch_shapes=[pltpu.VMEM((n_kv, t_len, 2 * LANES), BF16)],
        compiler_params=_cparams(("parallel", "parallel", "arbitrary")),
        name="attn_a",
    )(q3, k3, v3)
    return out.reshape(bsz * t_len, 512)


def _attn_b_kernel(q_ref, k_ref, v_ref, o_ref, v3_ref):
    tq = q_ref.shape[1]
    n_pairs = v3_ref.shape[0]

    @pl.when(pl.program_id(2) == 0)
    def _():
        for pi in range(n_pairs):
            v3_ref[pi, :, :LANES] = v_ref[0, :, pi * LANES:(pi + 1) * LANES]
            v3_ref[pi, :, LANES:] = jnp.ones((v_ref.shape[1], LANES), BF16)

    lane = lax.broadcasted_iota(jnp.int32, (tq, LANES), 1)
    outs, chains = [], []
    for hd in range(2 * n_pairs):
        sl = slice(hd * LANES, (hd + 1) * LANES)
        q = lambda sl=sl: q_ref[0, :, sl]
        k = lambda sl=sl: k_ref[0, :, sl]
        chains.append(_attend(q, k, v3_ref.at[hd // 2], outs))
    _run_skewed(chains)
    for pi in range(n_pairs):
        o_ref[0, :, pi * LANES:(pi + 1) * LANES] = jnp.where(lane < 64, outs[2 * pi], outs[2 * pi + 1]).astype(o_ref.dtype)


def _attn_b(q, k, v, bsz, t_len, tq, n_pairs=4):
    q3, k3, v3 = (a.reshape(bsz, t_len, a.shape[-1]) for a in (q, k, v))
    out = pl.pallas_call(
        _attn_b_kernel,
        grid=(bsz, B_HEADS // (2 * n_pairs), t_len // tq),
        in_specs=[pl.BlockSpec((1, tq, 256 * n_pairs), lambda b, p, i: (b, i, p)),
                  pl.BlockSpec((1, t_len, 256 * n_pairs), lambda b, p, i: (b, 0, p)),
                  pl.BlockSpec((1, t_len, LANES * n_pairs), lambda b, p, i: (b, 0, p))],
        out_specs=pl.BlockSpec((1, tq, LANES * n_pairs), lambda b, p, i: (b, i, p)),
        out_shape=jax.ShapeDtypeStruct((bsz, t_len, 512), BF16),
        scratch_shapes=[pltpu.VMEM((n_pairs, t_len, 2 * LANES), BF16)],
        compiler_params=_cparams(("parallel", "parallel", "arbitrary")),
        name="attn_b",
    )(q3, k3, v3)
    return out.reshape(bsz * t_len, 512)


def _log_sigmoid(z):
    return jnp.minimum(z, 0.0) - jnp.log(1.0 + jnp.exp(-jnp.abs(z)))


def _dot3(a, b):
    ah, al = _split_bf16(a)
    bh, bl = _split_bf16(b)
    return _dot(ah, bh) + (_dot(ah, bl) + _dot(al, bh))


def _gla_kernel(q_ref, k_ref, v_ref, g_ref, dec_ref, wf_ref, wb_ref, bf_ref, bb_ref, gco_ref,
                o_ref, acc_ref, st_ref, *, tile):
    t_len = q_ref.shape[1]
    n_tiles = t_len // tile
    n_chunks = tile // C_CHUNK
    row = lax.broadcasted_iota(jnp.int32, (tile, tile), 0)
    col = lax.broadcasted_iota(jnp.int32, (tile, tile), 1)
    same = (row // C_CHUNK) == (col // C_CHUNK)
    row_chunk = lax.broadcasted_iota(jnp.int32, (tile, LANES), 0) // C_CHUNK
    keep_f = same & (col <= row)
    keep_b = same & (col >= row)
    lower = jnp.where(keep_f, 1.0, 0.0).astype(BF16)
    upper = jnp.where(keep_b, 1.0, 0.0).astype(BF16)
    q_scale = C_DK ** -0.5
    st_ref[...] = jnp.zeros_like(st_ref)

    n_heads = q_ref.shape[2] // LANES

    def tile_step(j, d, hd):
        forward = d == 0
        w_ref, b_ref = (wf_ref, bf_ref) if forward else (wb_ref, bb_ref)
        cum, keep = (lower, keep_f) if forward else (upper, keep_b)
        rows = pl.ds(pl.multiple_of(j * tile, tile), tile)
        ls = slice(hd * LANES, (hd + 1) * LANES)
        si = d * n_heads + hd
        q = q_ref[0, rows, ls].astype(F32) * q_scale
        k = k_ref[0, rows, ls].astype(F32)
        vb = v_ref[0, rows, ls]
        z = _dot(dec_ref[0, rows, :], w_ref[:, ls])
        yield
        la = _log_sigmoid(z + b_ref[:, ls]) * (1.0 / C_TAU)
        r = _dot(cum, jnp.concatenate(_split_bf16(la), axis=1))
        yield
        b = r[:, :LANES] + r[:, LANES:]
        b3 = b.reshape(n_chunks, C_CHUNK, LANES)
        edge = b3[:, C_CHUNK - 1:, :] if forward else b3[:, :1, :]
        tot = jnp.broadcast_to(edge, b3.shape).reshape(tile, LANES)
        qf = q * jnp.exp(b)
        qt = qf.astype(BF16)
        kt = (k * jnp.exp(-b)).astype(BF16)
        k2 = k * jnp.exp(tot - b)
        raw = _dot_nt(qt, kt)
        k2_blk = jnp.concatenate([jnp.where(row_chunk == c, k2, 0.0).astype(BF16) for c in range(n_chunks)], axis=1)
        kv = _dot_tn(vb, k2_blk)
        yield
        intra = _dot(jnp.where(keep, raw, 0.0).astype(BF16), vb)
        dec = jnp.exp(edge)
        q_blk = jnp.concatenate([jnp.where(row_chunk == c, qf, 0.0).astype(BF16) for c in range(n_chunks)], axis=1)
        st = st_ref[si]
        entering = [None] * n_chunks
        for c in (range(n_chunks) if forward else range(n_chunks - 1, -1, -1)):
            entering[c] = st.astype(BF16)
            st = st * dec[c] + kv[:, c * LANES:(c + 1) * LANES]
        st_ref[si] = st
        inter = _dot_nt(q_blk, jnp.concatenate(entering, axis=1))
        yield
        acc_ref[d, rows, ls] = intra + inter

    def body(it, carry):
        _run_skewed([tile_step(it if d == 0 else n_tiles - 1 - it, d, hd)
                     for hd in range(n_heads) for d in (0, 1)])
        return carry

    lax.fori_loop(0, n_tiles, body, 0)
    for hd in range(n_heads):
        ls = slice(hd * LANES, (hd + 1) * LANES)
        y = _rms(acc_ref[0, :, ls] + acc_ref[1, :, ls], gco_ref[...])
        gate = g_ref[0, :, ls].astype(F32)
        o_ref[0, :, ls] = (y * (gate * _sigmoid(gate))).astype(o_ref.dtype)


def _gla(cqkvg, dec, wf, wb, bf, bb, gco, bsz, t_len, tile, hp=4):
    c3 = cqkvg.reshape(bsz, t_len, 4 * C_HEADS * LANES)
    d3 = dec.reshape(bsz, t_len, LANES)
    groups = C_HEADS // hp
    wl = hp * LANES

    def cspec(j):
        return pl.BlockSpec((1, t_len, wl), lambda b, h: (b, 0, j * groups + h))

    out = pl.pallas_call(
        functools.partial(_gla_kernel, tile=tile),
        grid=(bsz, groups),
        in_specs=[cspec(0), cspec(1), cspec(2), cspec(3),
                  pl.BlockSpec((1, t_len, LANES), lambda b, h: (b, 0, 0)),
                  pl.BlockSpec((LANES, wl), lambda b, h: (0, h)),
                  pl.BlockSpec((LANES, wl), lambda b, h: (0, h)),
                  pl.BlockSpec((1, wl), lambda b, h: (0, h)),
                  pl.BlockSpec((1, wl), lambda b, h: (0, h)),
                  pl.BlockSpec((1, LANES), lambda b, h: (0, 0))],
        out_specs=pl.BlockSpec((1, t_len, wl), lambda b, h: (b, 0, h)),
        out_shape=jax.ShapeDtypeStruct((bsz, t_len, C_HEADS * LANES), BF16),
        scratch_shapes=[pltpu.VMEM((2, t_len, wl), F32), pltpu.VMEM((2 * hp, C_DV, C_DK), F32)],
        compiler_params=_cparams(("parallel", "parallel")),
        name="gla",
    )(c3, c3, c3, c3, d3, wf, wb, bf, bb, gco)
    return out.reshape(bsz * t_len, C_HEADS * LANES)


def _merge_tile(x, gmix_ref, wgate_ref, bg_ref, y_refs, wp_refs, wo_ref):
    h = _rms(x, gmix_ref[...]).astype(BF16)
    terms = []

    def branch(j):
        sl = slice(j * D_MODEL, (j + 1) * D_MODEL)
        gate = _dot(h, wgate_ref[:, sl])
        proj = _dot(y_refs[j][...], wp_refs[j][...])
        yield
        terms.append(_sigmoid(gate + bg_ref[:, sl]) * proj)

    _run_skewed([branch(j) for j in range(N_BRANCH)])
    m = (terms[0] + terms[1]) + terms[2]
    return x + _dot(m.astype(BF16), wo_ref[...])


def _merge_specs(tm):
    tok = lambda w: pl.BlockSpec((tm, w), lambda i: (i, 0))
    return [tok(512), tok(512), tok(512),
            _const_spec((1, D_MODEL)),
            _const_spec((D_MODEL, N_BRANCH * D_MODEL)),
            _const_spec((1, N_BRANCH * D_MODEL)),
            _const_spec((512, D_MODEL)), _const_spec((512, D_MODEL)), _const_spec((512, D_MODEL)),
            _const_spec((D_MODEL, D_MODEL))]


def _merge_ffn_kernel(x_ref, ya_ref, yb_ref, yc_ref, gmix_ref, wgate_ref, bg_ref, wpa_ref, wpb_ref, wpc_ref, wo_ref,
                      g_ref, wg_ref, wu_ref, wd_ref, o_ref, *, chunk):
    x = _merge_tile(x_ref[...], gmix_ref, wgate_ref, bg_ref, (ya_ref, yb_ref, yc_ref),
                    (wpa_ref, wpb_ref, wpc_ref), wo_ref)
    h = _rms(x, g_ref[...]).astype(BF16)
    o_ref[...] = x
    for c0 in range(0, D_FF, chunk):
        gt = _dot(h, wg_ref[:, c0:c0 + chunk])
        up = _dot(h, wu_ref[:, c0:c0 + chunk])
        a = (gt * _sigmoid(gt) * up).astype(BF16)
        o_ref[...] += _dot(a, wd_ref[c0:c0 + chunk, :])


def _merge_ffn(x2, merge_args, g, wg, wu, wd, tm, chunk=256):
    n = x2.shape[0]
    return pl.pallas_call(
        functools.partial(_merge_ffn_kernel, chunk=chunk),
        grid=(n // tm,),
        in_specs=[pl.BlockSpec((tm, D_MODEL), lambda i: (i, 0))] + _merge_specs(tm)
                 + [_const_spec((1, D_MODEL)),
                    _const_spec((D_MODEL, D_FF)), _const_spec((D_MODEL, D_FF)), _const_spec((D_FF, D_MODEL))],
        out_specs=pl.BlockSpec((tm, D_MODEL), lambda i: (i, 0)),
        out_shape=jax.ShapeDtypeStruct((n, D_MODEL), F32),
        compiler_params=_cparams(("parallel",)),
        name="merge_ffn",
    )(x2, *merge_args, g, wg, wu, wd)


def _merge_router_kernel(x_ref, ya_ref, yb_ref, yc_ref, gmix_ref, wgate_ref, bg_ref, wpa_ref, wpb_ref, wpc_ref, wo_ref,
                         g_ref, wr_ref, xo_ref, h_ref, gate_ref, rank_ref, rank_t_ref, cnt_ref, tot_ref, carry_ref):
    tm = x_ref.shape[0]

    @pl.when(pl.program_id(0) == 0)
    def _():
        carry_ref[...] = jnp.zeros_like(carry_ref)

    x = _merge_tile(x_ref[...], gmix_ref, wgate_ref, bg_ref, (ya_ref, yb_ref, yc_ref),
                    (wpa_ref, wpb_ref, wpc_ref), wo_ref)
    xo_ref[...] = x
    hf = _rms(x, g_ref[...])
    h_ref[...] = hf.astype(h_ref.dtype)
    logits = _dot3(hf, wr_ref[...])
    lane = lax.broadcasted_iota(jnp.int32, logits.shape, 1).astype(F32)
    neg = jnp.float32(-jnp.inf)
    logits = jnp.where(lane < N_EXPERTS, logits, neg)
    v1 = jnp.max(logits, axis=-1, keepdims=True)
    i1 = jnp.min(jnp.where(logits == v1, lane, float(LANES)), axis=-1, keepdims=True)
    rest = jnp.where(lane == i1, neg, logits)
    v2 = jnp.max(rest, axis=-1, keepdims=True)
    i2 = jnp.min(jnp.where(rest == v2, lane, float(LANES)), axis=-1, keepdims=True)
    e2 = jnp.exp(v2 - v1)
    w1 = 1.0 / (1.0 + e2)
    w2 = e2 / (1.0 + e2)
    gate_ref[...] = jnp.where(lane == i1, w1, 0.0) + jnp.where(lane == i2, w2, 0.0)

    chosen = (lane == i1) | (lane == i2)
    assign = jnp.where(chosen, 1.0, 0.0)
    row = lax.broadcasted_iota(jnp.int32, (tm, tm), 0)
    col = lax.broadcasted_iota(jnp.int32, (tm, tm), 1)
    earlier = jnp.where(col < row, 1.0, 0.0).astype(BF16)
    carry = carry_ref[...]
    before = _dot(earlier, assign.astype(BF16)) + carry[0:1, :]
    rank = jnp.where(chosen, before, -1.0)
    rank_ref[...] = rank
    rank_t_ref[...] = rank.T[:N_EXPERTS, :]
    cnt_ref[0] = carry
    carry = carry + jnp.sum(assign, axis=0, keepdims=True)
    carry_ref[...] = carry
    tot_ref[...] = carry


def _merge_router(x2, merge_args, g, wr_pad, tm):
    n = x2.shape[0]
    nt = n // tm
    return pl.pallas_call(
        _merge_router_kernel,
        grid=(nt,),
        in_specs=[pl.BlockSpec((tm, D_MODEL), lambda i: (i, 0))] + _merge_specs(tm)
                 + [_const_spec((1, D_MODEL)),
                    _const_spec((D_MODEL, LANES))],
        out_specs=[pl.BlockSpec((tm, D_MODEL), lambda i: (i, 0)),
                   pl.BlockSpec((tm, D_MODEL), lambda i: (i, 0)),
                   pl.BlockSpec((tm, LANES), lambda i: (i, 0)),
                   pl.BlockSpec((tm, LANES), lambda i: (i, 0)),
                   pl.BlockSpec((N_EXPERTS, tm), lambda i: (0, i)),
                   pl.BlockSpec((1, 8, LANES), lambda i: (i, 0, 0)),
                   pl.BlockSpec((8, LANES), lambda i: (0, 0))],
        out_shape=[jax.ShapeDtypeStruct((n, D_MODEL), F32),
                   jax.ShapeDtypeStruct((n, D_MODEL), BF16),
                   jax.ShapeDtypeStruct((n, LANES), F32),
                   jax.ShapeDtypeStruct((n, LANES), F32),
                   jax.ShapeDtypeStruct((N_EXPERTS, n), F32),
                   jax.ShapeDtypeStruct((nt, 8, LANES), F32),
                   jax.ShapeDtypeStruct((8, LANES), F32)],
        scratch_shapes=[pltpu.VMEM((8, LANES), F32)],
        compiler_params=_cparams(("arbitrary",)),
        name="merge_router",
    )(x2, *merge_args, g, wr_pad)


def _moe_plan(cnt, tot, n, tb, rs, tg):
    i32 = jnp.int32
    counts = tot[0, :N_EXPERTS].astype(i32)
    cum = jnp.concatenate([cnt[:, 0, :N_EXPERTS], tot[:1, :N_EXPERTS]], axis=0).astype(i32)
    n_sb = (counts + rs - 1) // rs
    sb_end = jnp.cumsum(n_sb)
    sb_start = sb_end - n_sb
    goff = sb_start * rs
    n_valid_sb = sb_end[-1]
    s_max = 2 * n // rs + N_EXPERTS
    s_ids = jnp.arange(s_max, dtype=i32)
    last_sb = n_valid_sb - 1
    s_eff = jnp.minimum(s_ids, last_sb)
    sb_expert = jnp.minimum(jnp.sum(s_eff[:, None] >= sb_end[None, :], axis=1), N_EXPERTS - 1).astype(i32)
    sb_rows = jnp.clip(counts[sb_expert] - (s_eff - sb_start[sb_expert]) * rs, 0, rs)
    sb_tiles = ((sb_rows + tb - 1) // tb).astype(i32)

    t_ids = jnp.arange(s_max * rs // tb, dtype=i32)
    t_sb = t_ids * tb // rs
    t_exp = sb_expert[jnp.minimum(t_sb, last_sb)]
    t_r0 = t_ids * tb - goff[t_exp]
    t_valid = (t_sb <= last_sb) & (t_r0 < counts[t_exp])
    t_r1 = jnp.minimum(t_r0 + tb, counts[t_exp])
    cum_e = cum[:, t_exp]
    t_lo = jnp.sum(cum_e[1:] <= t_r0[None, :], axis=0)
    t_hi = jnp.sum(cum_e[:-1] < t_r1[None, :], axis=0)
    t_last = (last_sb + 1) * (rs // tg) - 1
    gather_tabs = tuple(a.astype(i32) for a in (t_exp, t_r0, t_lo, t_hi, t_valid, t_last.reshape(1)))

    start = goff[None, :] + cum[:-1]
    num = cum[1:] - cum[:-1]
    last_blk = (last_sb + 1) * (rs // tb) - 1
    b0 = jnp.minimum(start // tb, last_blk)
    b1 = jnp.minimum((start + jnp.maximum(num, 1) - 1) // tb, last_blk)
    b1 = jnp.where(b1 == b0, jnp.where(b0 < last_blk, b0 + 1, b0 - 1), b1)
    blk = jnp.stack([b0, b1], axis=-1).reshape(-1).astype(i32)
    expert_tabs = (sb_expert, sb_tiles, last_sb.reshape(1).astype(i32))
    return gather_tabs, expert_tabs, (blk, goff.astype(i32))


def _gather_kernel(exp_ref, r0_ref, lo_ref, hi_ref, valid_ref, last_ref, rank_ref, h_ref, o_ref, acc_ref, *, tb):
    step = pl.program_id(0)
    n_sub = o_ref.shape[0] // tb
    n_blk = rank_ref.shape[1]
    row = lax.broadcasted_iota(jnp.int32, (tb, tb), 0)

    @pl.when(step > last_ref[0])
    def _():
        o_ref[...] = jnp.zeros_like(o_ref)

    for u in range(n_sub):
        t = step * n_sub + u
        rows = slice(u * tb, (u + 1) * tb)
        in_range = step <= last_ref[0]

        @pl.when(in_range & (valid_ref[t] == 0))
        def _(rows=rows):
            o_ref[rows, :] = jnp.zeros((tb, D_MODEL), o_ref.dtype)

        @pl.when(in_range & (valid_ref[t] == 1))
        def _(rows=rows, t=t):
            e = exp_ref[t]
            want = row + r0_ref[t]
            lo, hi = lo_ref[t], hi_ref[t]
            acc_ref[rows, :] = jnp.zeros((tb, D_MODEL), F32)

            def onehot_rows(kb, live):
                rk = rank_ref[e, pl.ds(kb, 1), :].astype(jnp.int32)
                rk = jnp.where(live, rk, -1)
                onehot = jnp.where(rk == want, 1.0, 0.0).astype(BF16)
                return _dot(onehot, h_ref[pl.ds(pl.multiple_of(kb * tb, tb), tb), :])

            def body(p, carry):
                kb = lo + 2 * p
                kb2 = jnp.minimum(kb + 1, n_blk - 1)
                acc_ref[rows, :] += onehot_rows(kb, True) + onehot_rows(kb2, kb + 1 < hi)
                return carry

            lax.fori_loop(0, (hi - lo + 1) // 2, body, 0)
            o_ref[rows, :] = acc_ref[rows, :].astype(o_ref.dtype)


def _gather(tabs, rank_t3, h, n_rows, tg, tb):
    n = h.shape[0]
    return pl.pallas_call(
        functools.partial(_gather_kernel, tb=tb),
        grid_spec=pltpu.PrefetchScalarGridSpec(
            num_scalar_prefetch=6,
            grid=(n_rows // tg,),
            in_specs=[_const_spec((N_EXPERTS, n // tb, tb)),
                      _const_spec((n, D_MODEL))],
            out_specs=pl.BlockSpec((tg, D_MODEL), lambda t, e, r, lo, hi, v, last: (t, 0)),
            scratch_shapes=[pltpu.VMEM((tg, D_MODEL), F32)]),
        out_shape=jax.ShapeDtypeStruct((n_rows, D_MODEL), BF16),
        compiler_params=_cparams(("arbitrary",)),
        name="moe_gather",
    )(*tabs, rank_t3, h)


def _experts_kernel(exp_ref, tiles_ref, last_ref, xs_ref, wg_ref, wu_ref, wd_ref, o_ref,
                    acc_ref, wgb_ref, wub_ref, wdb_ref, *, tb):
    s, c = pl.program_id(0), pl.program_id(1)
    nc = pl.num_programs(1)

    @pl.when((s > last_ref[0]) & (c == 0))
    def _():
        o_ref[...] = jnp.zeros_like(o_ref)

    @pl.when(s <= last_ref[0])
    def _():
        @pl.when(c == 0)
        def _():
            acc_ref[...] = jnp.zeros_like(acc_ref)

        wgb_ref[...] = wg_ref[0].astype(BF16)
        wub_ref[...] = wu_ref[0].astype(BF16)
        wdb_ref[...] = wd_ref[0].astype(BF16)

        def tile_step(t):
            rows = pl.ds(pl.multiple_of(t * tb, tb), tb)
            x = xs_ref[rows, :]
            gt = _dot(x, wgb_ref[...])
            up = _dot(x, wub_ref[...])
            yield
            a = (gt * _sigmoid(gt) * up).astype(BF16)
            y = _dot(a, wdb_ref[...])
            yield
            acc_ref[rows, :] += y

        def body(t4, carry):
            _run_skewed([tile_step(4 * t4 + u) for u in range(4)])
            return carry

        n_tiles = tiles_ref[s]
        lax.fori_loop(0, n_tiles >> 2, body, 0)
        done = n_tiles & ~3

        @pl.when((n_tiles & 2) == 2)
        def _():
            _run_skewed([tile_step(done), tile_step(done + 1)])

        @pl.when((n_tiles & 1) == 1)
        def _():
            _run_skewed([tile_step(n_tiles - 1)])

        @pl.when(c == nc - 1)
        def _():
            o_ref[...] = acc_ref[...].astype(o_ref.dtype)


def _experts(tabs, xs, wg, wu, wd, rs, tb, fc):
    n_rows = xs.shape[0]
    nc = D_FF_EXPERT // fc

    def sb(s, last):
        return jnp.minimum(s, last[0])

    def chunk(s, c, last):
        return jnp.where(s <= last[0], c, nc - 1)

    return pl.pallas_call(
        functools.partial(_experts_kernel, tb=tb),
        grid_spec=pltpu.PrefetchScalarGridSpec(
            num_scalar_prefetch=3,
            grid=(n_rows // rs, nc),
            in_specs=[pl.BlockSpec((rs, D_MODEL), lambda s, c, ex, tl, last: (sb(s, last), 0)),
                      pl.BlockSpec((1, D_MODEL, fc), lambda s, c, ex, tl, last: (ex[sb(s, last)], 0, chunk(s, c, last))),
                      pl.BlockSpec((1, D_MODEL, fc), lambda s, c, ex, tl, last: (ex[sb(s, last)], 0, chunk(s, c, last))),
                      pl.BlockSpec((1, fc, D_MODEL), lambda s, c, ex, tl, last: (ex[sb(s, last)], chunk(s, c, last), 0))],
            out_specs=pl.BlockSpec((rs, D_MODEL), lambda s, c, ex, tl, last: (s, 0)),
            scratch_shapes=[pltpu.VMEM((rs, D_MODEL), F32),
                            pltpu.VMEM((D_MODEL, fc), BF16), pltpu.VMEM((D_MODEL, fc), BF16),
                            pltpu.VMEM((fc, D_MODEL), BF16)]),
        out_shape=jax.ShapeDtypeStruct((n_rows, D_MODEL), BF16),
        compiler_params=_cparams(("arbitrary", "arbitrary")),
        name="moe_experts",
    )(*tabs, xs, wg, wu, wd)


def _combine_kernel(blk_ref, goff_ref, x_ref, gate_ref, rank_ref, *rest, final_norm):
    y_refs, (gfin_ref, o_ref) = rest[:2 * N_EXPERTS], rest[2 * N_EXPERTS:]
    i = pl.program_id(0)
    tt = x_ref.shape[0]
    tb = y_refs[0].shape[0]
    lane = lax.broadcasted_iota(jnp.int32, (tt, LANES), 1)
    col = lax.broadcasted_iota(jnp.int32, (tt, tb), 1)
    gate, rank = gate_ref[...], rank_ref[...]
    y = x_ref[...]
    for e in range(N_EXPERTS):
        ge = jnp.sum(jnp.where(lane == e, gate, 0.0), axis=-1, keepdims=True)
        rk = jnp.sum(jnp.where(lane == e, rank, 0.0), axis=-1, keepdims=True).astype(jnp.int32)
        pos = jnp.where(rk >= 0, rk + goff_ref[e], -1)
        parts = [_dot(jnp.where(pos - blk_ref[(i * N_EXPERTS + e) * 2 + j] * tb == col, 1.0, 0.0).astype(BF16),
                      y_refs[2 * e + j][...]) for j in range(2)]
        y = y + ge * (parts[0] + parts[1])
    o_ref[...] = _rms(y, gfin_ref[...]) if final_norm else y


def _combine(tabs, x2, gate, rank, ys, gfin, tt, tb, final_norm):
    n = x2.shape[0]

    def yspec(slot):
        return pl.BlockSpec((tb, D_MODEL), lambda i, blk, goff: (blk[i * 2 * N_EXPERTS + slot], 0))

    tok = lambda w: pl.BlockSpec((tt, w), lambda i, blk, goff: (i, 0))
    return pl.pallas_call(
        functools.partial(_combine_kernel, final_norm=final_norm),
        grid_spec=pltpu.PrefetchScalarGridSpec(
            num_scalar_prefetch=2,
            grid=(n // tt,),
            in_specs=[tok(D_MODEL), tok(LANES), tok(LANES)] + [yspec(k) for k in range(2 * N_EXPERTS)]
                     + [pl.BlockSpec((1, D_MODEL), lambda i, blk, goff: (0, 0))],
            out_specs=tok(D_MODEL)),
        out_shape=jax.ShapeDtypeStruct((n, D_MODEL), F32),
        compiler_params=_cparams(("arbitrary",)),
        name="moe_combine",
    )(*tabs, x2, gate, rank, *([ys] * (2 * N_EXPERTS)), gfin)


def _merge_moe(x2, merge_args, g_ffn, w_router, wg, wu, wd, gfin, final_norm):
    n = x2.shape[0]
    rs = min(2048, n)
    tb = min(256, rs // 2)
    tg = min(512, rs)
    wr = jnp.pad(w_router, ((0, 0), (0, LANES - N_EXPERTS)))
    x2, h, gate, rank, rank_t, cnt, tot = _merge_router(x2, merge_args, g_ffn, wr, tb)
    gather_tabs, expert_tabs, combine_tabs = _moe_plan(cnt, tot, n, tb, rs, tg)
    n_rows = (2 * n // rs + N_EXPERTS) * rs
    xs = _gather(gather_tabs, rank_t.reshape(N_EXPERTS, n // tb, tb), h, n_rows, tg, tb)
    ys = _experts(expert_tabs, xs, wg, wu, wd, rs, tb, 512)
    return _combine(combine_tabs, x2, gate, rank, ys, gfin, tb, tb, final_norm)


def _final_norm_kernel(x_ref, g_ref, o_ref):
    o_ref[...] = _rms(x_ref[...], g_ref[...])


def _final_norm(x2, g, tm):
    n = x2.shape[0]
    return pl.pallas_call(
        _final_norm_kernel,
        grid=(n // tm,),
        in_specs=[pl.BlockSpec((tm, D_MODEL), lambda i: (i, 0)), _const_spec((1, D_MODEL))],
        out_specs=pl.BlockSpec((tm, D_MODEL), lambda i: (i, 0)),
        out_shape=jax.ShapeDtypeStruct((n, D_MODEL), F32),
        compiler_params=_cparams(("parallel",)),
        name="final_norm",
    )(x2, g)


def _rope_tables(t_len):
    rows = t_len // GRID_W
    row = np.repeat(np.arange(rows, dtype=np.float64), GRID_W)
    col = np.tile(np.arange(GRID_W, dtype=np.float64), rows)
    lane = np.arange(LANES)

    def table(rot_dim, lane_in_slice, active):
        n_freq = rot_dim // 4
        inv_freq = ROPE_THETA ** (-np.arange(n_freq, dtype=np.float64) / n_freq)
        freq = inv_freq[lane_in_slice % n_freq]
        use_row = (lane_in_slice % rot_dim) < (rot_dim // 2)
        ang = np.where(use_row[None, :], row[:, None], col[:, None]) * freq[None, :]
        sign = np.where((lane_in_slice % (rot_dim // 2)) < n_freq, -1.0, 1.0)
        cos = np.where(active[None, :], np.cos(ang), 1.0)
        sin = np.where(active[None, :], np.sin(ang) * sign[None, :], 0.0)
        return jnp.asarray(cos, F32), jnp.asarray(sin, F32)

    cos_a, sin_a = table(A_HEAD_DIM, lane % A_HEAD_DIM, np.ones(LANES, bool))
    in_rope = (lane >= B_NOPE_DIM) & (lane < B_NOPE_DIM + B_ROPE_DIM)
    cos_b, sin_b = table(B_ROPE_DIM, (lane - B_NOPE_DIM) % B_ROPE_DIM, in_rope)
    return cos_a, sin_a, cos_b, sin_b


def _pack_w_in(w):
    o_c, o_gate = 1312, 3392
    zeros = lambda k: jnp.zeros((D_MODEL, k), w.dtype)
    cols = [w[:, :o_c], zeros(LANES - B_ROPE_DIM), w[:, o_c:o_gate], zeros(LANES - 2 * C_GATE_RANK)]
    return jnp.concatenate(cols, axis=1).astype(BF16), w[:, o_gate:].astype(BF16)


def _pack_b_weights(w_q_up, w_kv_up):
    hq = w_q_up.reshape(B_Q_RANK, B_HEADS, B_NOPE_DIM + B_ROPE_DIM)
    wq = jnp.pad(hq, ((0, 0), (0, 0), (0, LANES - B_NOPE_DIM - B_ROPE_DIM))).reshape(B_Q_RANK, B_HEADS * LANES)
    hkv = w_kv_up.reshape(B_KV_RANK, B_HEADS, B_NOPE_DIM + B_V_DIM)
    wk = jnp.pad(hkv[:, :, :B_NOPE_DIM], ((0, 0), (0, 0), (0, LANES - B_NOPE_DIM))).reshape(B_KV_RANK, B_HEADS * LANES)
    wv = hkv[:, :, B_NOPE_DIM:].reshape(B_KV_RANK, B_HEADS * B_V_DIM)
    return wq.astype(BF16), wk.astype(BF16), wv.astype(BF16)


def _rope_placement():
    e = np.zeros((LANES, LANES), np.float32)
    e[np.arange(B_ROPE_DIM), B_NOPE_DIM + np.arange(B_ROPE_DIM)] = 1.0
    return jnp.asarray(e, BF16)


def kernel(x, w_in, b_gate, g_mix, g_a_q, g_a_k, g_b_q, w_b_q_up, g_b_kv, w_b_kv_up, w_c_af_up, b_c_af, w_c_ab_up, b_c_ab, g_c_out, w_pa, w_pb, w_pc, w_out, g_ffn, w_ff_gate, w_ff_up, w_ff_down, w_router, w_e_gate, w_e_up, w_e_down, g_final):
    bsz, t_len, _ = x.shape
    n = bsz * t_len
    depth = w_in.shape[0]
    tm = min(512, t_len)
    tq_a = min(512, t_len)
    tq_b = min(512, t_len)
    gla_tile = min(256, t_len)

    cos_a, sin_a, cos_b, sin_b = _rope_tables(t_len)
    e_mat = _rope_placement()
    row = lambda v: v.reshape(1, -1).astype(F32)
    x2 = x.reshape(n, D_MODEL)

    for i in range(depth):
        wq, wk, wv = _pack_b_weights(w_b_q_up[i], w_b_kv_up[i])
        w_packed, w_gate = _pack_w_in(w_in[i])
        qa, ka, av2, qb, kb, vb, cqkvg, dec = _fused_in(
            x2, row(g_mix[i]), w_packed, row(jnp.tile(g_a_q[i], 2)), row(jnp.tile(g_a_k[i], 2)),
            cos_a, sin_a, row(g_b_q[i]), row(g_b_kv[i]), wq, wk, wv, e_mat, cos_b, sin_b, tm, t_len)
        ya = _attn_a(qa, ka, av2, bsz, t_len, tq_a)
        yb = _attn_b(qb, kb, vb, bsz, t_len, tq_b)

        wf = jnp.pad(w_c_af_up[i], ((0, LANES - C_GATE_RANK), (0, 0))).astype(BF16)
        wb = jnp.pad(w_c_ab_up[i], ((C_GATE_RANK, LANES - 2 * C_GATE_RANK), (0, 0))).astype(BF16)
        yc = _gla(cqkvg, dec, wf, wb, row(b_c_af[i]), row(b_c_ab[i]), row(g_c_out[i]), bsz, t_len, gla_tile)

        merge_args = (ya, yb, yc, row(g_mix[i]), w_gate, row(b_gate[i]), w_pa[i].astype(BF16), w_pb[i].astype(BF16),
                      w_pc[i].astype(BF16), w_out[i].astype(BF16))
        j = i // 2
        last = i == depth - 1
        if i % 2 == 0:
            x2 = _merge_ffn(x2, merge_args, row(g_ffn[i]), w_ff_gate[j].astype(BF16), w_ff_up[j].astype(BF16),
                            w_ff_down[j].astype(BF16), tm)
            if last:
                x2 = _final_norm(x2, row(g_final), tm)
        else:
            x2 = _merge_moe(x2, merge_args, row(g_ffn[i]), w_router[j], w_e_gate[j], w_e_up[j], w_e_down[j],
                            row(g_final), last)
    return x2.reshape(bsz, t_len, D_MODEL)
```

```python
import functools

import jax
import jax.numpy as jnp
import numpy as np
from jax import lax
from jax.experimental import pallas as pl
from jax.experimental.pallas import tpu as pltpu

F32 = jnp.float32
BF16 = jnp.bfloat16

D_MODEL = 1024
GRID_W = 64
ROPE_THETA = 10000.0
EPS = 1e-6
A_HEADS, A_KV_HEADS, A_HEAD_DIM = 8, 2, 64
B_HEADS, B_NOPE_DIM, B_ROPE_DIM, B_V_DIM = 8, 64, 32, 64
B_Q_RANK = B_KV_RANK = 256
C_HEADS, C_DK, C_DV, C_GATE_RANK, C_TAU, C_CHUNK = 4, 128, 128, 16, 16.0, 64
N_BRANCH = 3
D_FF = 2816
N_EXPERTS, TOP_K, D_FF_EXPERT = 8, 2, 3584

LANES = 128
ROW_ALIGN = 16
VMEM_LIMIT = 56 * 1024 * 1024


def _cparams(sem):
    return pltpu.CompilerParams(dimension_semantics=sem, vmem_limit_bytes=VMEM_LIMIT)


def _const_spec(shape):
    nd = len(shape)
    return pl.BlockSpec(shape, lambda *_: (0,) * nd, pipeline_mode=pl.Buffered(1))


def _rms(xf, g):
    return xf * lax.rsqrt(jnp.mean(xf * xf, axis=-1, keepdims=True) + EPS) * g


def _sigmoid(x):
    return 1.0 / (1.0 + jnp.exp(-x))


def _split_bf16(x):
    hi = x.astype(BF16)
    lo = (x - hi.astype(F32)).astype(BF16)
    return hi, lo


def _dot(a, b):
    return jnp.dot(a, b, preferred_element_type=F32)


def _dot_nt(a, b):
    return lax.dot_general(a, b, (((1,), (1,)), ((), ())), preferred_element_type=F32)


def _dot_tn(a, b):
    return lax.dot_general(a, b, (((0,), (0,)), ((), ())), preferred_element_type=F32)


def _rope_group(x, cos, sin_signed, lane, half):
    fwd = pltpu.roll(x, LANES - half, axis=1)
    bwd = pltpu.roll(x, half, axis=1)
    swapped = jnp.where((lane % (2 * half)) < half, fwd, bwd)
    return x * cos + swapped * sin_signed


def _seg64_meansq(x, lane):
    sq = x * x
    lo = jnp.sum(jnp.where(lane < 64, sq, 0.0), axis=-1, keepdims=True)
    hi = jnp.sum(jnp.where(lane < 64, 0.0, sq), axis=-1, keepdims=True)
    return jnp.where(lane < 64, lo, hi) * (1.0 / 64.0)


F_AQ, F_AKV, F_B, F_C, F_DEC, F_WIDTH = 0, 512, 768, 1408, 3456, 3584


def _dup_half(x, lane, low):
    other = pltpu.roll(x, 64, axis=1)
    return jnp.where(lane < 64, x, other) if low else jnp.where(lane < 64, other, x)


def _fused_in_kernel(x_ref, g_ref, w_ref, gaq_ref, gak_ref, cosa_ref, sina_ref,
                     gbq_ref, gbkv_ref, wq_ref, wk_ref, wv_ref, e_ref, cosb_ref, sinb_ref,
                     qa_ref, ka_ref, va_ref, qb_ref, kb_ref, vb_ref, c_ref, dec_ref):
    rows = x_ref.shape[0]
    lane = lax.broadcasted_iota(jnp.int32, (rows, LANES), 1)
    h = _rms(x_ref[...], g_ref[...]).astype(BF16)

    def proj(off, width):
        return _dot(h, w_ref[:, off:off + width])

    def chain_aq():
        y = proj(F_AQ, 512)
        yield
        cos, sin, gq = cosa_ref[...], sina_ref[...], gaq_ref[...]
        for gi in range(4):
            sl = slice(gi * LANES, (gi + 1) * LANES)
            x = y[:, sl]
            x = x * lax.rsqrt(_seg64_meansq(x, lane) + EPS) * gq
            qa_ref[:, sl] = (_rope_group(x, cos, sin, lane, 16) * (A_HEAD_DIM ** -0.5)).astype(qa_ref.dtype)

    def chain_akv():
        y = proj(F_AKV, 256)
        yield
        k, v = y[:, :LANES], y[:, LANES:]
        k = k * lax.rsqrt(_seg64_meansq(k, lane) + EPS) * gak_ref[...]
        k = _rope_group(k, cosa_ref[...], sina_ref[...], lane, 16)
        for kv in range(A_KV_HEADS):
            sl = slice(kv * LANES, (kv + 1) * LANES)
            ka_ref[:, sl] = _dup_half(k, lane, kv == 0).astype(ka_ref.dtype)
            va_ref[:, sl] = _dup_half(v, lane, kv == 0).astype(va_ref.dtype)

    def chain_b():
        y = proj(F_B, 640)
        yield
        cq = _rms(y[:, :B_Q_RANK], gbq_ref[...]).astype(BF16)
        ckv = _rms(y[:, B_Q_RANK:2 * B_Q_RANK], gbkv_ref[...]).astype(BF16)
        q = _dot(cq, wq_ref[...])
        k = _dot(ckv, wk_ref[...])
        v = _dot(ckv, wv_ref[...])
        kr = _dot(y[:, 2 * B_Q_RANK:].astype(BF16), e_ref[...])
        yield
        cos, sin = cosb_ref[...], sinb_ref[...]
        kr = _rope_group(kr, cos, sin, lane, 8)
        scale = (B_NOPE_DIM + B_ROPE_DIM) ** -0.5
        for hd in range(B_HEADS):
            sl = slice(hd * LANES, (hd + 1) * LANES)
            qb_ref[:, sl] = (_rope_group(q[:, sl], cos, sin, lane, 8) * scale).astype(qb_ref.dtype)
            kb_ref[:, sl] = (k[:, sl] + kr).astype(kb_ref.dtype)
        vb_ref[...] = v.astype(vb_ref.dtype)

    def chain_plain(o_ref, off, c0, cw):
        y = proj(off + c0, cw)
        yield
        o_ref[:, c0:c0 + cw] = y.astype(o_ref.dtype)

    chains = [chain_aq(), chain_akv(), chain_b()]
    chains += [chain_plain(c_ref, F_C, c0, 512) for c0 in range(0, 2048, 512)]
    chains += [chain_plain(dec_ref, F_DEC, 0, LANES)]
    _run_skewed(chains)


def _fused_in(x2, g, w_packed, gaq, gak, cos_a, sin_a, gbq, gbkv, wq, wk, wv, e_mat, cos_b, sin_b, tm, t_len):
    n = x2.shape[0]
    nt = t_len // tm
    widths = (512, 256, 256, 1024, 1024, 512, 2048, LANES)
    table = pl.BlockSpec((tm, LANES), lambda i: (i % nt, 0))
    return pl.pallas_call(
        _fused_in_kernel,
        grid=(n // tm,),
        in_specs=[pl.BlockSpec((tm, D_MODEL), lambda i: (i, 0)),
                  _const_spec((1, D_MODEL)),
                  _const_spec((D_MODEL, F_WIDTH)),
                  _const_spec((1, LANES)), _const_spec((1, LANES)), table, table,
                  _const_spec((1, B_Q_RANK)), _const_spec((1, B_KV_RANK)),
                  _const_spec((B_Q_RANK, B_HEADS * LANES)),
                  _const_spec((B_KV_RANK, B_HEADS * LANES)),
                  _const_spec((B_KV_RANK, B_HEADS * B_V_DIM)),
                  _const_spec((LANES, LANES)), table, table],
        out_specs=[pl.BlockSpec((tm, w), lambda i: (i, 0)) for w in widths],
        out_shape=[jax.ShapeDtypeStruct((n, w), BF16) for w in widths],
        compiler_params=_cparams(("parallel",)),
        name="fused_in",
    )(x2, g, w_packed, gaq, gak, cos_a, sin_a, gbq, gbkv, wq, wk, wv, e_mat, cos_b, sin_b)


def _run_skewed(chains):
    pending, live = list(chains), []
    while pending or live:
        if pending:
            live.append(pending.pop(0))
        for g in reversed(list(live)):
            try:
                next(g)
            except StopIteration:
                live.remove(g)


def _attend(q, k, v3_ref, out):
    s = _dot_nt(q(), k())
    yield
    m = jnp.max(s, axis=-1, keepdims=True)
    p = jnp.exp(s - m).astype(BF16)
    r = _dot(p, v3_ref[...])
    yield
    out.append(r[:, :LANES] / r[:, LANES:])


def _attn_a_kernel(q_ref, k_ref, v_ref, o_ref, v3_ref):
    tq = q_ref.shape[1]
    n_kv = v3_ref.shape[0]

    @pl.when(pl.program_id(2) == 0)
    def _():
        for kv in range(n_kv):
            v3_ref[kv, :, :LANES] = v_ref[0, :, kv * LANES:(kv + 1) * LANES]
            v3_ref[kv, :, LANES:] = jnp.ones((v_ref.shape[1], LANES), BF16)

    lane = lax.broadcasted_iota(jnp.int32, (tq, LANES), 1)
    first = lane < 64
    outs, chains = [], []
    for gi in range(2 * n_kv):
        kv = gi // 2
        k = lambda kv=kv: k_ref[0, :, kv * LANES:(kv + 1) * LANES]
        for keep_first in (True, False):
            def q(gi=gi, keep_first=keep_first):
                qp = q_ref[0, :, gi * LANES:(gi + 1) * LANES].astype(F32)
                return jnp.where(first == keep_first, qp, 0.0).astype(BF16)
            chains.append(_attend(q, k, v3_ref.at[kv], outs))
    _run_skewed(chains)
    for gi in range(2 * n_kv):
        o_ref[0, :, gi * LANES:(gi + 1) * LANES] = jnp.where(first, outs[2 * gi], outs[2 * gi + 1]).astype(o_ref.dtype)


def _attn_a(q, k2, v2, bsz, t_len, tq, n_kv=2):
    q3, k3, v3 = (a.reshape(bsz, t_len, a.shape[-1]) for a in (q, k2, v2))
    out = pl.pallas_call(
        _attn_a_kernel,
        grid=(bsz, A_KV_HEADS // n_kv, t_len // tq),
        in_specs=[pl.BlockSpec((1, tq, 256 * n_kv), lambda b, g, i: (b, i, g)),
                  pl.BlockSpec((1, t_len, LANES * n_kv), lambda b, g, i: (b, 0, g)),
                  pl.BlockSpec((1, t_len, LANES * n_kv), lambda b, g, i: (b, 0, g))],
        out_specs=pl.BlockSpec((1, tq, 256 * n_kv), lambda b, g, i: (b, i, g)),
        out_shape=jax.ShapeDtypeStruct((bsz, t_len, 512), BF16),
        scratch_shapes=[pltpu.VMEM((n_kv, t_len, 2 * LANES), BF16)],
        compiler_params=_cparams(("parallel", "parallel", "arbitrary")),
        name="attn_a",
    )(q3, k3, v3)
    return out.reshape(bsz * t_len, 512)


def _attn_b_kernel(q_ref, k_ref, v_ref, o_ref, v3_ref):
    tq = q_ref.shape[1]
    n_pairs = v3_ref.shape[0]

    @pl.when(pl.program_id(2) == 0)
    def _():
        for pi in range(n_pairs):
            v3_ref[pi, :, :LANES] = v_ref[0, :, pi * LANES:(pi + 1) * LANES]
            v3_ref[pi, :, LANES:] = jnp.ones((v_ref.shape[1], LANES), BF16)

    lane = lax.broadcasted_iota(jnp.int32, (tq, LANES), 1)
    outs, chains = [], []
    for hd in range(2 * n_pairs):
        sl = slice(hd * LANES, (hd + 1) * LANES)
        q = lambda sl=sl: q_ref[0, :, sl]
        k = lambda sl=sl: k_ref[0, :, sl]
        chains.append(_attend(q, k, v3_ref.at[hd // 2], outs))
    _run_skewed(chains)
    for pi in range(n_pairs):
        o_ref[0, :, pi * LANES:(pi + 1) * LANES] = jnp.where(lane < 64, outs[2 * pi], outs[2 * pi + 1]).astype(o_ref.dtype)


def _attn_b(q, k, v, bsz, t_len, tq, n_pairs=4):
    q3, k3, v3 = (a.reshape(bsz, t_len, a.shape[-1]) for a in (q, k, v))
    out = pl.pallas_call(
        _attn_b_kernel,
        grid=(bsz, B_HEADS // (2 * n_pairs), t_len // tq),
        in_specs=[pl.BlockSpec((1, tq, 256 * n_pairs), lambda b, p, i: (b, i, p)),
                  pl.BlockSpec((1, t_len, 256 * n_pairs), lambda b, p, i: (b, 0, p)),
                  pl.BlockSpec((1, t_len, LANES * n_pairs), lambda b, p, i: (b, 0, p))],
        out_specs=pl.BlockSpec((1, tq, LANES * n_pairs), lambda b, p, i: (b, i, p)),
        out_shape=jax.ShapeDtypeStruct((bsz, t_len, 512), BF16),
        scratch_shapes=[pltpu.VMEM((n_pairs, t_len, 2 * LANES), BF16)],
        compiler_params=_cparams(("parallel", "parallel", "arbitrary")),
        name="attn_b",
    )(q3, k3, v3)
    return out.reshape(bsz * t_len, 512)


def _log_sigmoid(z):
    return jnp.minimum(z, 0.0) - jnp.log(1.0 + jnp.exp(-jnp.abs(z)))


def _dot3(a, b):
    ah, al = _split_bf16(a)
    bh, bl = _split_bf16(b)
    return _dot(ah, bh) + (_dot(ah, bl) + _dot(al, bh))


def _gla_kernel(q_ref, k_ref, v_ref, g_ref, dec_ref, wf_ref, wb_ref, bf_ref, bb_ref, gco_ref,
                o_ref, acc_ref, st_ref, *, tile):
    t_len = q_ref.shape[1]
    n_tiles = t_len // tile
    n_chunks = tile // C_CHUNK
    row = lax.broadcasted_iota(jnp.int32, (tile, tile), 0)
    col = lax.broadcasted_iota(jnp.int32, (tile, tile), 1)
    same = (row // C_CHUNK) == (col // C_CHUNK)
    row_chunk = lax.broadcasted_iota(jnp.int32, (tile, LANES), 0) // C_CHUNK
    keep_f = same & (col <= row)
    keep_b = same & (col >= row)
    lower = jnp.where(keep_f, 1.0, 0.0).astype(BF16)
    upper = jnp.where(keep_b, 1.0, 0.0).astype(BF16)
    q_scale = C_DK ** -0.5
    st_ref[...] = jnp.zeros_like(st_ref)

    n_heads = q_ref.shape[2] // LANES

    def tile_step(j, d, hd):
        forward = d == 0
        w_ref, b_ref = (wf_ref, bf_ref) if forward else (wb_ref, bb_ref)
        cum, keep = (lower, keep_f) if forward else (upper, keep_b)
        rows = pl.ds(pl.multiple_of(j * tile, tile), tile)
        ls = slice(hd * LANES, (hd + 1) * LANES)
        si = d * n_heads + hd
        q = q_ref[0, rows, ls].astype(F32) * q_scale
        k = k_ref[0, rows, ls].astype(F32)
        vb = v_ref[0, rows, ls]
        z = _dot(dec_ref[0, rows, :], w_ref[:, ls])
        yield
        la = _log_sigmoid(z + b_ref[:, ls]) * (1.0 / C_TAU)
        r = _dot(cum, jnp.concatenate(_split_bf16(la), axis=1))
        yield
        b = r[:, :LANES] + r[:, LANES:]
        b3 = b.reshape(n_chunks, C_CHUNK, LANES)
        edge = b3[:, C_CHUNK - 1:, :] if forward else b3[:, :1, :]
        tot = jnp.broadcast_to(edge, b3.shape).reshape(tile, LANES)
        qf = q * jnp.exp(b)
        qt = qf.astype(BF16)
        kt = (k * jnp.exp(-b)).astype(BF16)
        k2 = k * jnp.exp(tot - b)
        raw = _dot_nt(qt, kt)
        k2_blk = jnp.concatenate([jnp.where(row_chunk == c, k2, 0.0).astype(BF16) for c in range(n_chunks)], axis=1)
        kv = _dot_tn(vb, k2_blk)
        yield
        intra = _dot(jnp.where(keep, raw, 0.0).astype(BF16), vb)
        dec = jnp.exp(edge)
        q_blk = jnp.concatenate([jnp.where(row_chunk == c, qf, 0.0).astype(BF16) for c in range(n_chunks)], axis=1)
        st = st_ref[si]
        entering = [None] * n_chunks
        for c in (range(n_chunks) if forward else range(n_chunks - 1, -1, -1)):
            entering[c] = st.astype(BF16)
            st = st * dec[c] + kv[:, c * LANES:(c + 1) * LANES]
        st_ref[si] = st
        inter = _dot_nt(q_blk, jnp.concatenate(entering, axis=1))
        yield
        acc_ref[d, rows, ls] = intra + inter

    def body(it, carry):
        _run_skewed([tile_step(it if d == 0 else n_tiles - 1 - it, d, hd)
                     for hd in range(n_heads) for d in (0, 1)])
        return carry

    lax.fori_loop(0, n_tiles, body, 0)
    for hd in range(n_heads):
        ls = slice(hd * LANES, (hd + 1) * LANES)
        y = _rms(acc_ref[0, :, ls] + acc_ref[1, :, ls], gco_ref[...])
        gate = g_ref[0, :, ls].astype(F32)
        o_ref[0, :, ls] = (y * (gate * _sigmoid(gate))).astype(o_ref.dtype)


def _gla(cqkvg, dec, wf, wb, bf, bb, gco, bsz, t_len, tile, hp=4):
    c3 = cqkvg.reshape(bsz, t_len, 4 * C_HEADS * LANES)
    d3 = dec.reshape(bsz, t_len, LANES)
    groups = C_HEADS // hp
    wl = hp * LANES

    def cspec(j):
        return pl.BlockSpec((1, t_len, wl), lambda b, h: (b, 0, j * groups + h))

    out = pl.pallas_call(
        functools.partial(_gla_kernel, tile=tile),
        grid=(bsz, groups),
        in_specs=[cspec(0), cspec(1), cspec(2), cspec(3),
                  pl.BlockSpec((1, t_len, LANES), lambda b, h: (b, 0, 0)),
                  pl.BlockSpec((LANES, wl), lambda b, h: (0, h)),
                  pl.BlockSpec((LANES, wl), lambda b, h: (0, h)),
                  pl.BlockSpec((1, wl), lambda b, h: (0, h)),
                  pl.BlockSpec((1, wl), lambda b, h: (0, h)),
                  pl.BlockSpec((1, LANES), lambda b, h: (0, 0))],
        out_specs=pl.BlockSpec((1, t_len, wl), lambda b, h: (b, 0, h)),
        out_shape=jax.ShapeDtypeStruct((bsz, t_len, C_HEADS * LANES), BF16),
        scratch_shapes=[pltpu.VMEM((2, t_len, wl), F32), pltpu.VMEM((2 * hp, C_DV, C_DK), F32)],
        compiler_params=_cparams(("parallel", "parallel")),
        name="gla",
    )(c3, c3, c3, c3, d3, wf, wb, bf, bb, gco)
    return out.reshape(bsz * t_len, C_HEADS * LANES)


def _merge_stages(x, rows, gmix_ref, wgate_ref, bg_ref, y_refs, wp_refs, wo_ref, out):
    h = _rms(x, gmix_ref[...]).astype(BF16)
    dots = []
    for j in range(N_BRANCH):
        sl = slice(j * D_MODEL, (j + 1) * D_MODEL)
        dots.append((_dot(h, wgate_ref[:, sl]), _dot(y_refs[j][rows, :], wp_refs[j][...])))
    yield
    m = None
    for j, (gate, proj) in enumerate(dots):
        term = _sigmoid(gate + bg_ref[:, j * D_MODEL:(j + 1) * D_MODEL]) * proj
        m = term if m is None else m + term
    y = _dot(m.astype(BF16), wo_ref[...])
    yield
    out.append(x + y)


def _merge_tile(x, gmix_ref, wgate_ref, bg_ref, y_refs, wp_refs, wo_ref):
    out = []
    for _ in _merge_stages(x, slice(None), gmix_ref, wgate_ref, bg_ref, y_refs, wp_refs, wo_ref, out):
        pass
    return out[0]


def _merge_specs(tm):
    tok = lambda w: pl.BlockSpec((tm, w), lambda i: (i, 0))
    return [tok(512), tok(512), tok(512),
            _const_spec((1, D_MODEL)),
            _const_spec((D_MODEL, N_BRANCH * D_MODEL)),
            _const_spec((1, N_BRANCH * D_MODEL)),
            _const_spec((512, D_MODEL)), _const_spec((512, D_MODEL)), _const_spec((512, D_MODEL)),
            _const_spec((D_MODEL, D_MODEL))]


def _merge_ffn_kernel(x_ref, ya_ref, yb_ref, yc_ref, gmix_ref, wgate_ref, bg_ref, wpa_ref, wpb_ref, wpc_ref, wo_ref,
                      g_ref, wg_ref, wu_ref, wd_ref, o_ref, *, chunk):
    x = _merge_tile(x_ref[...], gmix_ref, wgate_ref, bg_ref, (ya_ref, yb_ref, yc_ref),
                    (wpa_ref, wpb_ref, wpc_ref), wo_ref)
    h = _rms(x, g_ref[...]).astype(BF16)
    o_ref[...] = x
    for c0 in range(0, D_FF, chunk):
        gt = _dot(h, wg_ref[:, c0:c0 + chunk])
        up = _dot(h, wu_ref[:, c0:c0 + chunk])
        a = (gt * _sigmoid(gt) * up).astype(BF16)
        o_ref[...] += _dot(a, wd_ref[c0:c0 + chunk, :])


def _merge_ffn(x2, merge_args, g, wg, wu, wd, tm, chunk=256):
    n = x2.shape[0]
    return pl.pallas_call(
        functools.partial(_merge_ffn_kernel, chunk=chunk),
        grid=(n // tm,),
        in_specs=[pl.BlockSpec((tm, D_MODEL), lambda i: (i, 0))] + _merge_specs(tm)
                 + [_const_spec((1, D_MODEL)),
                    _const_spec((D_MODEL, D_FF)), _const_spec((D_MODEL, D_FF)), _const_spec((D_FF, D_MODEL))],
        out_specs=pl.BlockSpec((tm, D_MODEL), lambda i: (i, 0)),
        out_shape=jax.ShapeDtypeStruct((n, D_MODEL), F32),
        compiler_params=_cparams(("parallel",)),
        name="merge_ffn",
    )(x2, *merge_args, g, wg, wu, wd)


def _merge_router_kernel(x_ref, ya_ref, yb_ref, yc_ref, gmix_ref, wgate_ref, bg_ref, wpa_ref, wpb_ref, wpc_ref, wo_ref,
                         g_ref, wr_ref, xo_ref, h_ref, gate_ref, rank_ref, rank_t_ref, cnt_ref, tot_ref, carry_ref,
                         *, tb):
    tm = x_ref.shape[0]

    @pl.when(pl.program_id(0) == 0)
    def _():
        carry_ref[...] = jnp.zeros_like(carry_ref)

    lane = lax.broadcasted_iota(jnp.int32, (tb, LANES), 1).astype(F32)
    row = lax.broadcasted_iota(jnp.int32, (tb, tb), 0)
    col = lax.broadcasted_iota(jnp.int32, (tb, tb), 1)
    earlier = jnp.where(col < row, 1.0, 0.0).astype(BF16)
    neg = jnp.float32(-jnp.inf)
    local = []

    def block(u):
        rows = slice(u * tb, (u + 1) * tb)
        merged = []
        yield from _merge_stages(x_ref[rows, :], rows, gmix_ref, wgate_ref, bg_ref, (ya_ref, yb_ref, yc_ref),
                                 (wpa_ref, wpb_ref, wpc_ref), wo_ref, merged)
        x = merged[0]
        xo_ref[rows, :] = x
        hf = _rms(x, g_ref[...])
        h_ref[rows, :] = hf.astype(h_ref.dtype)
        logits = _dot3(hf, wr_ref[...])
        yield
        logits = jnp.where(lane < N_EXPERTS, logits, neg)
        v1 = jnp.max(logits, axis=-1, keepdims=True)
        i1 = jnp.min(jnp.where(logits == v1, lane, float(LANES)), axis=-1, keepdims=True)
        rest = jnp.where(lane == i1, neg, logits)
        v2 = jnp.max(rest, axis=-1, keepdims=True)
        i2 = jnp.min(jnp.where(rest == v2, lane, float(LANES)), axis=-1, keepdims=True)
        e2 = jnp.exp(v2 - v1)
        w1 = 1.0 / (1.0 + e2)
        w2 = e2 / (1.0 + e2)
        gate_ref[rows, :] = jnp.where(lane == i1, w1, 0.0) + jnp.where(lane == i2, w2, 0.0)
        chosen = (lane == i1) | (lane == i2)
        assign = jnp.where(chosen, 1.0, 0.0)
        inside = _dot(earlier, assign.astype(BF16))
        yield
        local.append((chosen, inside, jnp.sum(assign, axis=0, keepdims=True)))

    _run_skewed([block(u) for u in range(tm // tb)])
    carry = carry_ref[...]
    for u, (chosen, inside, count) in enumerate(local):
        rows = slice(u * tb, (u + 1) * tb)
        rank = jnp.where(chosen, inside + carry[0:1, :], -1.0)
        rank_ref[rows, :] = rank
        rank_t_ref[:, rows] = rank.T[:N_EXPERTS, :]
        cnt_ref[u] = carry
        carry = carry + count
    carry_ref[...] = carry
    tot_ref[...] = carry


def _merge_router(x2, merge_args, g, wr_pad, tm, tb):
    n = x2.shape[0]
    nt = n // tm
    return pl.pallas_call(
        functools.partial(_merge_router_kernel, tb=tb),
        grid=(nt,),
        in_specs=[pl.BlockSpec((tm, D_MODEL), lambda i: (i, 0))] + _merge_specs(tm)
                 + [_const_spec((1, D_MODEL)),
                    _const_spec((D_MODEL, LANES))],
        out_specs=[pl.BlockSpec((tm, D_MODEL), lambda i: (i, 0)),
                   pl.BlockSpec((tm, D_MODEL), lambda i: (i, 0)),
                   pl.BlockSpec((tm, LANES), lambda i: (i, 0)),
                   pl.BlockSpec((tm, LANES), lambda i: (i, 0)),
                   pl.BlockSpec((N_EXPERTS, tm), lambda i: (0, i)),
                   pl.BlockSpec((tm // tb, 8, LANES), lambda i: (i, 0, 0)),
                   pl.BlockSpec((8, LANES), lambda i: (0, 0))],
        out_shape=[jax.ShapeDtypeStruct((n, D_MODEL), F32),
                   jax.ShapeDtypeStruct((n, D_MODEL), BF16),
                   jax.ShapeDtypeStruct((n, LANES), F32),
                   jax.ShapeDtypeStruct((n, LANES), F32),
                   jax.ShapeDtypeStruct((N_EXPERTS, n), F32),
                   jax.ShapeDtypeStruct((n // tb, 8, LANES), F32),
                   jax.ShapeDtypeStruct((8, LANES), F32)],
        scratch_shapes=[pltpu.VMEM((8, LANES), F32)],
        compiler_params=_cparams(("arbitrary",)),
        name="merge_router",
    )(x2, *merge_args, g, wr_pad)


def _moe_plan(cnt, tot, n, tb, rs, tg):
    i32 = jnp.int32
    counts = tot[0, :N_EXPERTS].astype(i32)
    cum = jnp.concatenate([cnt[:, 0, :N_EXPERTS], tot[:1, :N_EXPERTS]], axis=0).astype(i32)
    n_sb = (counts + rs - 1) // rs
    sb_end = jnp.cumsum(n_sb)
    sb_start = sb_end - n_sb
    goff = sb_start * rs
    n_valid_sb = sb_end[-1]
    s_max = 2 * n // rs + N_EXPERTS
    s_ids = jnp.arange(s_max, dtype=i32)
    last_sb = n_valid_sb - 1
    s_eff = jnp.minimum(s_ids, last_sb)
    sb_expert = jnp.minimum(jnp.sum(s_eff[:, None] >= sb_end[None, :], axis=1), N_EXPERTS - 1).astype(i32)
    sb_rows = jnp.clip(counts[sb_expert] - (s_eff - sb_start[sb_expert]) * rs, 0, rs)
    sb_tiles = ((sb_rows + tb - 1) // tb).astype(i32)

    t_ids = jnp.arange(s_max * rs // tb, dtype=i32)
    t_sb = t_ids * tb // rs
    t_exp = sb_expert[jnp.minimum(t_sb, last_sb)]
    t_r0 = t_ids * tb - goff[t_exp]
    t_valid = (t_sb <= last_sb) & (t_r0 < counts[t_exp])
    t_r1 = jnp.minimum(t_r0 + tb, counts[t_exp])
    cum_e = cum[:, t_exp]
    t_lo = jnp.sum(cum_e[1:] <= t_r0[None, :], axis=0)
    t_hi = jnp.where(t_valid, jnp.sum(cum_e[:-1] < t_r1[None, :], axis=0), t_lo)
    gather_tabs = tuple(a.astype(i32) for a in (t_exp, t_r0, t_lo, t_hi))

    start = goff[None, :] + cum[:-1]
    num = cum[1:] - cum[:-1]
    n_rows = s_max * rs
    off0 = jnp.minimum(start // ROW_ALIGN * ROW_ALIGN, n_rows - tb)
    need1 = (num > 0) & (start + num > off0 + tb)
    off1 = jnp.where(need1, jnp.minimum(off0 + tb, n_rows - tb), 0)
    offs = jnp.stack([off0, off1], axis=-1).reshape(-1).astype(i32)
    expert_tabs = (sb_expert, sb_tiles, last_sb.reshape(1).astype(i32))
    return gather_tabs, expert_tabs, (offs, need1.reshape(-1).astype(i32), goff.astype(i32))


def _gather_kernel(exp_ref, r0_ref, lo_ref, hi_ref, rank_ref, h_ref, o_ref, acc_ref, *, tb):
    step = pl.program_id(0)
    n_sub = o_ref.shape[0] // tb
    n_blk = rank_ref.shape[1]
    row = lax.broadcasted_iota(jnp.int32, (tb, tb), 0)
    tiles = [step * n_sub + u for u in range(n_sub)]
    n_trips = functools.reduce(jnp.maximum, [(hi_ref[t] - lo_ref[t] + 1) >> 1 for t in tiles])

    @pl.when(n_trips == 0)
    def _():
        o_ref[...] = jnp.zeros_like(o_ref)

    @pl.when(n_trips > 0)
    def _():
        acc_ref[...] = jnp.zeros_like(acc_ref)

        def body(p, carry):
            parts = []
            for t in tiles:
                e, lo, hi = exp_ref[t], lo_ref[t], hi_ref[t]
                want = row + r0_ref[t]
                for j in range(2):
                    kb = lo + 2 * p + j
                    kbc = jnp.minimum(kb, n_blk - 1)
                    rk = rank_ref[e, pl.ds(kbc, 1), :].astype(jnp.int32)
                    onehot = jnp.where(jnp.where(kb < hi, rk, -1) == want, 1.0, 0.0).astype(BF16)
                    parts.append(_dot(onehot, h_ref[pl.ds(pl.multiple_of(kbc * tb, tb), tb), :]))
            for u in range(n_sub):
                acc_ref[u * tb:(u + 1) * tb, :] += parts[2 * u] + parts[2 * u + 1]
            return carry

        lax.fori_loop(0, n_trips, body, 0)
        o_ref[...] = acc_ref[...].astype(o_ref.dtype)


def _gather(tabs, rank_t3, h, n_rows, tg, tb):
    n = h.shape[0]
    return pl.pallas_call(
        functools.partial(_gather_kernel, tb=tb),
        grid_spec=pltpu.PrefetchScalarGridSpec(
            num_scalar_prefetch=4,
            grid=(n_rows // tg,),
            in_specs=[_const_spec((N_EXPERTS, n // tb, tb)),
                      _const_spec((n, D_MODEL))],
            out_specs=pl.BlockSpec((tg, D_MODEL), lambda t, e, r, lo, hi: (t, 0)),
            scratch_shapes=[pltpu.VMEM((tg, D_MODEL), F32)]),
        out_shape=jax.ShapeDtypeStruct((n_rows, D_MODEL), BF16),
        compiler_params=_cparams(("arbitrary",)),
        name="moe_gather",
    )(*tabs, rank_t3, h)


def _experts_kernel(exp_ref, tiles_ref, last_ref, xs_ref, wg_ref, wu_ref, wd_ref, o_ref,
                    acc_ref, wgb_ref, wub_ref, wdb_ref, *, tb):
    s, c = pl.program_id(0), pl.program_id(1)
    nc = pl.num_programs(1)

    @pl.when((s > last_ref[0]) & (c == 0))
    def _():
        o_ref[...] = jnp.zeros_like(o_ref)

    @pl.when(s <= last_ref[0])
    def _():
        @pl.when(c == 0)
        def _():
            acc_ref[...] = jnp.zeros_like(acc_ref)

        wgb_ref[...] = wg_ref[0].astype(BF16)
        wub_ref[...] = wu_ref[0].astype(BF16)
        wdb_ref[...] = wd_ref[0].astype(BF16)

        def tile_step(t):
            rows = pl.ds(pl.multiple_of(t * tb, tb), tb)
            x = xs_ref[rows, :]
            gt = _dot(x, wgb_ref[...])
            up = _dot(x, wub_ref[...])
            yield
            a = (gt * _sigmoid(gt) * up).astype(BF16)
            y = _dot(a, wdb_ref[...])
            yield
            acc_ref[rows, :] += y

        def body(t4, carry):
            _run_skewed([tile_step(4 * t4 + u) for u in range(4)])
            return carry

        n_tiles = tiles_ref[s]
        lax.fori_loop(0, n_tiles >> 2, body, 0)
        done = n_tiles & ~3

        @pl.when((n_tiles & 2) == 2)
        def _():
            _run_skewed([tile_step(done), tile_step(done + 1)])

        @pl.when((n_tiles & 1) == 1)
        def _():
            _run_skewed([tile_step(n_tiles - 1)])

        @pl.when(c == nc - 1)
        def _():
            o_ref[...] = acc_ref[...].astype(o_ref.dtype)


def _experts(tabs, xs, wg, wu, wd, rs, tb, fc):
    n_rows = xs.shape[0]
    nc = D_FF_EXPERT // fc

    def sb(s, last):
        return jnp.minimum(s, last[0])

    def chunk(s, c, last):
        return jnp.where(s <= last[0], c, nc - 1)

    return pl.pallas_call(
        functools.partial(_experts_kernel, tb=tb),
        grid_spec=pltpu.PrefetchScalarGridSpec(
            num_scalar_prefetch=3,
            grid=(n_rows // rs, nc),
            in_specs=[pl.BlockSpec((rs, D_MODEL), lambda s, c, ex, tl, last: (sb(s, last), 0)),
                      pl.BlockSpec((1, D_MODEL, fc), lambda s, c, ex, tl, last: (ex[sb(s, last)], 0, chunk(s, c, last))),
                      pl.BlockSpec((1, D_MODEL, fc), lambda s, c, ex, tl, last: (ex[sb(s, last)], 0, chunk(s, c, last))),
                      pl.BlockSpec((1, fc, D_MODEL), lambda s, c, ex, tl, last: (ex[sb(s, last)], chunk(s, c, last), 0))],
            out_specs=pl.BlockSpec((rs, D_MODEL), lambda s, c, ex, tl, last: (s, 0)),
            scratch_shapes=[pltpu.VMEM((rs, D_MODEL), F32),
                            pltpu.VMEM((D_MODEL, fc), BF16), pltpu.VMEM((D_MODEL, fc), BF16),
                            pltpu.VMEM((fc, D_MODEL), BF16)]),
        out_shape=jax.ShapeDtypeStruct((n_rows, D_MODEL), BF16),
        compiler_params=_cparams(("arbitrary", "arbitrary")),
        name="moe_experts",
    )(*tabs, xs, wg, wu, wd)


def _combine_kernel(off_ref, need_ref, goff_ref, x_ref, gate_ref, rank_ref, *rest, final_norm):
    y_refs, (gfin_ref, o_ref) = rest[:2 * N_EXPERTS], rest[2 * N_EXPERTS:]
    i = pl.program_id(0)
    tt = x_ref.shape[0]
    tb = y_refs[0].shape[0]
    lane = lax.broadcasted_iota(jnp.int32, (tt, LANES), 1)
    col = lax.broadcasted_iota(jnp.int32, (tt, tb), 1)
    gate, rank = gate_ref[...], rank_ref[...]

    def expert_terms(e):
        ge = jnp.sum(jnp.where(lane == e, gate, 0.0), axis=-1, keepdims=True)
        rk = jnp.sum(jnp.where(lane == e, rank, 0.0), axis=-1, keepdims=True).astype(jnp.int32)
        return ge, jnp.where(rk >= 0, rk + goff_ref[e], -1)

    def window(e, j, pos):
        onehot = jnp.where(pos - off_ref[(i * N_EXPERTS + e) * 2 + j] == col, 1.0, 0.0).astype(BF16)
        return _dot(onehot, y_refs[2 * e + j][...])

    y = x_ref[...]
    for e in range(N_EXPERTS):
        ge, pos = expert_terms(e)
        y = y + ge * window(e, 0, pos)
    o_ref[...] = y
    for e in range(N_EXPERTS):
        @pl.when(need_ref[i * N_EXPERTS + e] == 1)
        def _(e=e):
            ge, pos = expert_terms(e)
            o_ref[...] += ge * window(e, 1, pos)
    if final_norm:
        o_ref[...] = _rms(o_ref[...], gfin_ref[...])


def _combine(tabs, x2, gate, rank, ys, gfin, tt, tb, final_norm):
    n = x2.shape[0]

    def yspec(slot):
        return pl.BlockSpec((pl.Element(tb), pl.Element(D_MODEL)),
                            lambda i, off, need, goff: (pl.multiple_of(off[i * 2 * N_EXPERTS + slot], ROW_ALIGN), 0))

    tok = lambda w: pl.BlockSpec((tt, w), lambda i, off, need, goff: (i, 0))
    return pl.pallas_call(
        functools.partial(_combine_kernel, final_norm=final_norm),
        grid_spec=pltpu.PrefetchScalarGridSpec(
            num_scalar_prefetch=3,
            grid=(n // tt,),
            in_specs=[tok(D_MODEL), tok(LANES), tok(LANES)] + [yspec(k) for k in range(2 * N_EXPERTS)]
                     + [pl.BlockSpec((1, D_MODEL), lambda i, off, need, goff: (0, 0))],
            out_specs=tok(D_MODEL)),
        out_shape=jax.ShapeDtypeStruct((n, D_MODEL), F32),
        compiler_params=_cparams(("arbitrary",)),
        name="moe_combine",
    )(*tabs, x2, gate, rank, *([ys] * (2 * N_EXPERTS)), gfin)


def _merge_moe(x2, merge_args, g_ffn, w_router, wg, wu, wd, gfin, final_norm):
    n = x2.shape[0]
    rs = min(2048, n)
    tb = min(256, rs // 2)
    tg = min(512, rs)
    wr = jnp.pad(w_router, ((0, 0), (0, LANES - N_EXPERTS)))
    x2, h, gate, rank, rank_t, cnt, tot = _merge_router(x2, merge_args, g_ffn, wr, min(2 * tb, n), tb)
    gather_tabs, expert_tabs, combine_tabs = _moe_plan(cnt, tot, n, tb, rs, tg)
    n_rows = (2 * n // rs + N_EXPERTS) * rs
    xs = _gather(gather_tabs, rank_t.reshape(N_EXPERTS, n // tb, tb), h, n_rows, tg, tb)
    ys = _experts(expert_tabs, xs, wg, wu, wd, rs, tb, 512)
    return _combine(combine_tabs, x2, gate, rank, ys, gfin, tb, tb, final_norm)


def _final_norm_kernel(x_ref, g_ref, o_ref):
    o_ref[...] = _rms(x_ref[...], g_ref[...])


def _final_norm(x2, g, tm):
    n = x2.shape[0]
    return pl.pallas_call(
        _final_norm_kernel,
        grid=(n // tm,),
        in_specs=[pl.BlockSpec((tm, D_MODEL), lambda i: (i, 0)), _const_spec((1, D_MODEL))],
        out_specs=pl.BlockSpec((tm, D_MODEL), lambda i: (i, 0)),
        out_shape=jax.ShapeDtypeStruct((n, D_MODEL), F32),
        compiler_params=_cparams(("parallel",)),
        name="final_norm",
    )(x2, g)


def _rope_tables(t_len):
    rows = t_len // GRID_W
    row = np.repeat(np.arange(rows, dtype=np.float64), GRID_W)
    col = np.tile(np.arange(GRID_W, dtype=np.float64), rows)
    lane = np.arange(LANES)

    def table(rot_dim, lane_in_slice, active):
        n_freq = rot_dim // 4
        inv_freq = ROPE_THETA ** (-np.arange(n_freq, dtype=np.float64) / n_freq)
        freq = inv_freq[lane_in_slice % n_freq]
        use_row = (lane_in_slice % rot_dim) < (rot_dim // 2)
        ang = np.where(use_row[None, :], row[:, None], col[:, None]) * freq[None, :]
        sign = np.where((lane_in_slice % (rot_dim // 2)) < n_freq, -1.0, 1.0)
        cos = np.where(active[None, :], np.cos(ang), 1.0)
        sin = np.where(active[None, :], np.sin(ang) * sign[None, :], 0.0)
        return jnp.asarray(cos, F32), jnp.asarray(sin, F32)

    cos_a, sin_a = table(A_HEAD_DIM, lane % A_HEAD_DIM, np.ones(LANES, bool))
    in_rope = (lane >= B_NOPE_DIM) & (lane < B_NOPE_DIM + B_ROPE_DIM)
    cos_b, sin_b = table(B_ROPE_DIM, (lane - B_NOPE_DIM) % B_ROPE_DIM, in_rope)
    return cos_a, sin_a, cos_b, sin_b


def _pack_w_in(w):
    o_c, o_gate = 1312, 3392
    zeros = lambda k: jnp.zeros((D_MODEL, k), w.dtype)
    cols = [w[:, :o_c], zeros(LANES - B_ROPE_DIM), w[:, o_c:o_gate], zeros(LANES - 2 * C_GATE_RANK)]
    return jnp.concatenate(cols, axis=1).astype(BF16), w[:, o_gate:].astype(BF16)


def _pack_b_weights(w_q_up, w_kv_up):
    hq = w_q_up.reshape(B_Q_RANK, B_HEADS, B_NOPE_DIM + B_ROPE_DIM)
    wq = jnp.pad(hq, ((0, 0), (0, 0), (0, LANES - B_NOPE_DIM - B_ROPE_DIM))).reshape(B_Q_RANK, B_HEADS * LANES)
    hkv = w_kv_up.reshape(B_KV_RANK, B_HEADS, B_NOPE_DIM + B_V_DIM)
    wk = jnp.pad(hkv[:, :, :B_NOPE_DIM], ((0, 0), (0, 0), (0, LANES - B_NOPE_DIM))).reshape(B_KV_RANK, B_HEADS * LANES)
    wv = hkv[:, :, B_NOPE_DIM:].reshape(B_KV_RANK, B_HEADS * B_V_DIM)
    return wq.astype(BF16), wk.astype(BF16), wv.astype(BF16)


def _rope_placement():
    e = np.zeros((LANES, LANES), np.float32)
    e[np.arange(B_ROPE_DIM), B_NOPE_DIM + np.arange(B_ROPE_DIM)] = 1.0
    return jnp.asarray(e, BF16)


def kernel(x, w_in, b_gate, g_mix, g_a_q, g_a_k, g_b_q, w_b_q_up, g_b_kv, w_b_kv_up, w_c_af_up, b_c_af, w_c_ab_up, b_c_ab, g_c_out, w_pa, w_pb, w_pc, w_out, g_ffn, w_ff_gate, w_ff_up, w_ff_down, w_router, w_e_gate, w_e_up, w_e_down, g_final):
    bsz, t_len, _ = x.shape
    n = bsz * t_len
    depth = w_in.shape[0]
    tm = min(512, t_len)
    tq_a = min(512, t_len)
    tq_b = min(512, t_len)
    gla_tile = min(256, t_len)

    cos_a, sin_a, cos_b, sin_b = _rope_tables(t_len)
    e_mat = _rope_placement()
    row = lambda v: v.reshape(1, -1).astype(F32)
    x2 = x.reshape(n, D_MODEL)

    for i in range(depth):
        wq, wk, wv = _pack_b_weights(w_b_q_up[i], w_b_kv_up[i])
        w_packed, w_gate = _pack_w_in(w_in[i])
        qa, ka, av2, qb, kb, vb, cqkvg, dec = _fused_in(
            x2, row(g_mix[i]), w_packed, row(jnp.tile(g_a_q[i], 2)), row(jnp.tile(g_a_k[i], 2)),
            cos_a, sin_a, row(g_b_q[i]), row(g_b_kv[i]), wq, wk, wv, e_mat, cos_b, sin_b, tm, t_len)
        ya = _attn_a(qa, ka, av2, bsz, t_len, tq_a)
        yb = _attn_b(qb, kb, vb, bsz, t_len, tq_b)

        wf = jnp.pad(w_c_af_up[i], ((0, LANES - C_GATE_RANK), (0, 0))).astype(BF16)
        wb = jnp.pad(w_c_ab_up[i], ((C_GATE_RANK, LANES - 2 * C_GATE_RANK), (0, 0))).astype(BF16)
        yc = _gla(cqkvg, dec, wf, wb, row(b_c_af[i]), row(b_c_ab[i]), row(g_c_out[i]), bsz, t_len, gla_tile)

        merge_args = (ya, yb, yc, row(g_mix[i]), w_gate, row(b_gate[i]), w_pa[i].astype(BF16), w_pb[i].astype(BF16),
                      w_pc[i].astype(BF16), w_out[i].astype(BF16))
        j = i // 2
        last = i == depth - 1
        if i % 2 == 0:
            x2 = _merge_ffn(x2, merge_args, row(g_ffn[i]), w_ff_gate[j].astype(BF16), w_ff_up[j].astype(BF16),
                            w_ff_down[j].astype(BF16), tm)
            if last:
                x2 = _final_norm(x2, row(g_final), tm)
        else:
            x2 = _merge_moe(x2, merge_args, row(g_ffn[i]), w_router[j], w_e_gate[j], w_e_up[j], w_e_down[j],
                            row(g_final), last)
    return x2.reshape(bsz, t_len, D_MODEL)
```

```python
import functools

import jax
import jax.numpy as jnp
import numpy as np
from jax import lax
from jax.experimental import pallas as pl
from jax.experimental.pallas import tpu as pltpu

F32 = jnp.float32
BF16 = jnp.bfloat16

D_MODEL = 1024
GRID_W = 64
ROPE_THETA = 10000.0
EPS = 1e-6
A_HEADS, A_KV_HEADS, A_HEAD_DIM = 8, 2, 64
B_HEADS, B_NOPE_DIM, B_ROPE_DIM, B_V_DIM = 8, 64, 32, 64
B_Q_RANK = B_KV_RANK = 256
C_HEADS, C_DK, C_DV, C_GATE_RANK, C_TAU, C_CHUNK = 4, 128, 128, 16, 16.0, 64
N_BRANCH = 3
D_FF = 2816
N_EXPERTS, TOP_K, D_FF_EXPERT = 8, 2, 3584

LANES = 128
ROW_ALIGN = 16
VMEM_LIMIT = 56 * 1024 * 1024


def _cparams(sem):
    return pltpu.CompilerParams(dimension_semantics=sem, vmem_limit_bytes=VMEM_LIMIT)


def _const_spec(shape):
    nd = len(shape)
    return pl.BlockSpec(shape, lambda *_: (0,) * nd, pipeline_mode=pl.Buffered(1))


def _rms(xf, g):
    return xf * lax.rsqrt(jnp.mean(xf * xf, axis=-1, keepdims=True) + EPS) * g


def _sigmoid(x):
    return 1.0 / (1.0 + jnp.exp(-x))


def _split_bf16(x):
    hi = x.astype(BF16)
    lo = (x - hi.astype(F32)).astype(BF16)
    return hi, lo


def _dot(a, b):
    return jnp.dot(a, b, preferred_element_type=F32)


def _dot_nt(a, b):
    return lax.dot_general(a, b, (((1,), (1,)), ((), ())), preferred_element_type=F32)


def _dot_tn(a, b):
    return lax.dot_general(a, b, (((0,), (0,)), ((), ())), preferred_element_type=F32)


def _rope_group(x, cos, sin_signed, lane, half):
    fwd = pltpu.roll(x, LANES - half, axis=1)
    bwd = pltpu.roll(x, half, axis=1)
    swapped = jnp.where((lane % (2 * half)) < half, fwd, bwd)
    return x * cos + swapped * sin_signed


def _seg64_meansq(x, lane):
    sq = x * x
    lo = jnp.sum(jnp.where(lane < 64, sq, 0.0), axis=-1, keepdims=True)
    hi = jnp.sum(jnp.where(lane < 64, 0.0, sq), axis=-1, keepdims=True)
    return jnp.where(lane < 64, lo, hi) * (1.0 / 64.0)


F_AQ, F_AKV, F_B, F_C, F_DEC, F_WIDTH = 0, 512, 768, 1408, 3456, 3584


def _dup_half(x, lane, low):
    other = pltpu.roll(x, 64, axis=1)
    return jnp.where(lane < 64, x, other) if low else jnp.where(lane < 64, other, x)


def _fused_in_kernel(x_ref, g_ref, w_ref, gaq_ref, gak_ref, cosa_ref, sina_ref,
                     gbq_ref, gbkv_ref, wq_ref, wk_ref, wv_ref, e_ref, cosb_ref, sinb_ref,
                     qa_ref, ka_ref, va_ref, qb_ref, kb_ref, vb_ref, c_ref, dec_ref):
    rows = x_ref.shape[0]
    lane = lax.broadcasted_iota(jnp.int32, (rows, LANES), 1)
    h = _rms(x_ref[...], g_ref[...]).astype(BF16)

    def proj(off, width):
        return _dot(h, w_ref[:, off:off + width])

    def chain_aq():
        y = proj(F_AQ, 512)
        yield
        cos, sin, gq = cosa_ref[...], sina_ref[...], gaq_ref[...]
        for gi in range(4):
            sl = slice(gi * LANES, (gi + 1) * LANES)
            x = y[:, sl]
            x = x * lax.rsqrt(_seg64_meansq(x, lane) + EPS) * gq
            qa_ref[:, sl] = (_rope_group(x, cos, sin, lane, 16) * (A_HEAD_DIM ** -0.5)).astype(qa_ref.dtype)

    def chain_akv():
        y = proj(F_AKV, 256)
        yield
        k, v = y[:, :LANES], y[:, LANES:]
        k = k * lax.rsqrt(_seg64_meansq(k, lane) + EPS) * gak_ref[...]
        k = _rope_group(k, cosa_ref[...], sina_ref[...], lane, 16)
        for kv in range(A_KV_HEADS):
            sl = slice(kv * LANES, (kv + 1) * LANES)
            ka_ref[:, sl] = _dup_half(k, lane, kv == 0).astype(ka_ref.dtype)
            va_ref[:, sl] = _dup_half(v, lane, kv == 0).astype(va_ref.dtype)

    def chain_b():
        y = proj(F_B, 640)
        yield
        cq = _rms(y[:, :B_Q_RANK], gbq_ref[...]).astype(BF16)
        ckv = _rms(y[:, B_Q_RANK:2 * B_Q_RANK], gbkv_ref[...]).astype(BF16)
        q = _dot(cq, wq_ref[...])
        k = _dot(ckv, wk_ref[...])
        v = _dot(ckv, wv_ref[...])
        kr = _dot(y[:, 2 * B_Q_RANK:].astype(BF16), e_ref[...])
        yield
        cos, sin = cosb_ref[...], sinb_ref[...]
        kr = _rope_group(kr, cos, sin, lane, 8)
        scale = (B_NOPE_DIM + B_ROPE_DIM) ** -0.5
        for hd in range(B_HEADS):
            sl = slice(hd * LANES, (hd + 1) * LANES)
            qb_ref[:, sl] = (_rope_group(q[:, sl], cos, sin, lane, 8) * scale).astype(qb_ref.dtype)
            kb_ref[:, sl] = (k[:, sl] + kr).astype(kb_ref.dtype)
        vb_ref[...] = v.astype(vb_ref.dtype)

    def chain_plain(o_ref, off, c0, cw):
        y = proj(off + c0, cw)
        yield
        o_ref[:, c0:c0 + cw] = y.astype(o_ref.dtype)

    chains = [chain_aq(), chain_akv(), chain_b()]
    chains += [chain_plain(c_ref, F_C, c0, 512) for c0 in range(0, 2048, 512)]
    chains += [chain_plain(dec_ref, F_DEC, 0, LANES)]
    _run_skewed(chains)


def _fused_in(x2, g, w_packed, gaq, gak, cos_a, sin_a, gbq, gbkv, wq, wk, wv, e_mat, cos_b, sin_b, tm, t_len):
    n = x2.shape[0]
    nt = t_len // tm
    widths = (512, 256, 256, 1024, 1024, 512, 2048, LANES)
    table = pl.BlockSpec((tm, LANES), lambda i: (i % nt, 0))
    return pl.pallas_call(
        _fused_in_kernel,
        grid=(n // tm,),
        in_specs=[pl.BlockSpec((tm, D_MODEL), lambda i: (i, 0)),
                  _const_spec((1, D_MODEL)),
                  _const_spec((D_MODEL, F_WIDTH)),
                  _const_spec((1, LANES)), _const_spec((1, LANES)), table, table,
                  _const_spec((1, B_Q_RANK)), _const_spec((1, B_KV_RANK)),
                  _const_spec((B_Q_RANK, B_HEADS * LANES)),
                  _const_spec((B_KV_RANK, B_HEADS * LANES)),
                  _const_spec((B_KV_RANK, B_HEADS * B_V_DIM)),
                  _const_spec((LANES, LANES)), table, table],
        out_specs=[pl.BlockSpec((tm, w), lambda i: (i, 0)) for w in widths],
        out_shape=[jax.ShapeDtypeStruct((n, w), BF16) for w in widths],
        compiler_params=_cparams(("parallel",)),
        name="fused_in",
    )(x2, g, w_packed, gaq, gak, cos_a, sin_a, gbq, gbkv, wq, wk, wv, e_mat, cos_b, sin_b)


def _run_skewed(chains):
    pending, live = list(chains), []
    while pending or live:
        if pending:
            live.append(pending.pop(0))
        for g in reversed(list(live)):
            try:
                next(g)
            except StopIteration:
                live.remove(g)


def _attend(q, k, v3_ref, out):
    s = _dot_nt(q(), k())
    yield
    m = jnp.max(s, axis=-1, keepdims=True)
    p = jnp.exp(s - m).astype(BF16)
    r = _dot(p, v3_ref[...])
    yield
    out.append(r[:, :LANES] / r[:, LANES:])


def _attn_a_kernel(q_ref, k_ref, v_ref, o_ref, v3_ref):
    tq = q_ref.shape[1]
    n_kv = v3_ref.shape[0]

    @pl.when(pl.program_id(2) == 0)
    def _():
        for kv in range(n_kv):
            v3_ref[kv, :, :LANES] = v_ref[0, :, kv * LANES:(kv + 1) * LANES]
            v3_ref[kv, :, LANES:] = jnp.ones((v_ref.shape[1], LANES), BF16)

    lane = lax.broadcasted_iota(jnp.int32, (tq, LANES), 1)
    first = lane < 64
    outs, chains = [], []
    for gi in range(2 * n_kv):
        kv = gi // 2
        k = lambda kv=kv: k_ref[0, :, kv * LANES:(kv + 1) * LANES]
        for keep_first in (True, False):
            def q(gi=gi, keep_first=keep_first):
                qp = q_ref[0, :, gi * LANES:(gi + 1) * LANES].astype(F32)
                return jnp.where(first == keep_first, qp, 0.0).astype(BF16)
            chains.append(_attend(q, k, v3_ref.at[kv], outs))
    _run_skewed(chains)
    for gi in range(2 * n_kv):
        o_ref[0, :, gi * LANES:(gi + 1) * LANES] = jnp.where(first, outs[2 * gi], outs[2 * gi + 1]).astype(o_ref.dtype)


def _attn_a(q, k2, v2, bsz, t_len, tq, n_kv=2):
    q3, k3, v3 = (a.reshape(bsz, t_len, a.shape[-1]) for a in (q, k2, v2))
    out = pl.pallas_call(
        _attn_a_kernel,
        grid=(bsz, A_KV_HEADS // n_kv, t_len // tq),
        in_specs=[pl.BlockSpec((1, tq, 256 * n_kv), lambda b, g, i: (b, i, g)),
                  pl.BlockSpec((1, t_len, LANES * n_kv), lambda b, g, i: (b, 0, g)),
                  pl.BlockSpec((1, t_len, LANES * n_kv), lambda b, g, i: (b, 0, g))],
        out_specs=pl.BlockSpec((1, tq, 256 * n_kv), lambda b, g, i: (b, i, g)),
        out_shape=jax.ShapeDtypeStruct((bsz, t_len, 512), BF16),
        scratch_shapes=[pltpu.VMEM((n_kv, t_len, 2 * LANES), BF16)],
        compiler_params=_cparams(("parallel", "parallel", "arbitrary")),
        name="attn_a",
    )(q3, k3, v3)
    return out.reshape(bsz * t_len, 512)


def _attn_b_kernel(q_ref, k_ref, v_ref, o_ref, v3_ref):
    tq = q_ref.shape[1]
    n_pairs = v3_ref.shape[0]

    @pl.when(pl.program_id(2) == 0)
    def _():
        for pi in range(n_pairs):
            v3_ref[pi, :, :LANES] = v_ref[0, :, pi * LANES:(pi + 1) * LANES]
            v3_ref[pi, :, LANES:] = jnp.ones((v_ref.shape[1], LANES), BF16)

    lane = lax.broadcasted_iota(jnp.int32, (tq, LANES), 1)
    outs, chains = [], []
    for hd in range(2 * n_pairs):
        sl = slice(hd * LANES, (hd + 1) * LANES)
        q = lambda sl=sl: q_ref[0, :, sl]
        k = lambda sl=sl: k_ref[0, :, sl]
        chains.append(_attend(q, k, v3_ref.at[hd // 2], outs))
    _run_skewed(chains)
    for pi in range(n_pairs):
        o_ref[0, :, pi * LANES:(pi + 1) * LANES] = jnp.where(lane < 64, outs[2 * pi], outs[2 * pi + 1]).astype(o_ref.dtype)


def _attn_b(q, k, v, bsz, t_len, tq, n_pairs=4):
    q3, k3, v3 = (a.reshape(bsz, t_len, a.shape[-1]) for a in (q, k, v))
    out = pl.pallas_call(
        _attn_b_kernel,
        grid=(bsz, B_HEADS // (2 * n_pairs), t_len // tq),
        in_specs=[pl.BlockSpec((1, tq, 256 * n_pairs), lambda b, p, i: (b, i, p)),
                  pl.BlockSpec((1, t_len, 256 * n_pairs), lambda b, p, i: (b, 0, p)),
                  pl.BlockSpec((1, t_len, LANES * n_pairs), lambda b, p, i: (b, 0, p))],
        out_specs=pl.BlockSpec((1, tq, LANES * n_pairs), lambda b, p, i: (b, i, p)),
        out_shape=jax.ShapeDtypeStruct((bsz, t_len, 512), BF16),
        scratch_shapes=[pltpu.VMEM((n_pairs, t_len, 2 * LANES), BF16)],
        compiler_params=_cparams(("parallel", "parallel", "arbitrary")),
        name="attn_b",
    )(q3, k3, v3)
    return out.reshape(bsz * t_len, 512)


def _log_sigmoid(z):
    return jnp.minimum(z, 0.0) - jnp.log(1.0 + jnp.exp(-jnp.abs(z)))


def _dot3(a, b):
    ah, al = _split_bf16(a)
    bh, bl = _split_bf16(b)
    return _dot(ah, bh) + (_dot(ah, bl) + _dot(al, bh))


def _gla_kernel(q_ref, k_ref, v_ref, g_ref, dec_ref, wf_ref, wb_ref, bf_ref, bb_ref, gco_ref,
                o_ref, acc_ref, st_ref, *, tile):
    t_len = q_ref.shape[1]
    n_tiles = t_len // tile
    n_chunks = tile // C_CHUNK
    row = lax.broadcasted_iota(jnp.int32, (tile, tile), 0)
    col = lax.broadcasted_iota(jnp.int32, (tile, tile), 1)
    same = (row // C_CHUNK) == (col // C_CHUNK)
    row_chunk = lax.broadcasted_iota(jnp.int32, (tile, LANES), 0) // C_CHUNK
    keep_f = same & (col <= row)
    keep_b = same & (col >= row)
    lower = jnp.where(keep_f, 1.0, 0.0).astype(BF16)
    upper = jnp.where(keep_b, 1.0, 0.0).astype(BF16)
    q_scale = C_DK ** -0.5
    st_ref[...] = jnp.zeros_like(st_ref)

    n_heads = q_ref.shape[2] // LANES

    def tile_step(j, d, hd, z_all):
        forward = d == 0
        b_ref = bf_ref if forward else bb_ref
        cum, keep = (lower, keep_f) if forward else (upper, keep_b)
        rows = pl.ds(pl.multiple_of(j * tile, tile), tile)
        ls = slice(hd * LANES, (hd + 1) * LANES)
        si = d * n_heads + hd
        q = q_ref[0, rows, ls].astype(F32) * q_scale
        k = k_ref[0, rows, ls].astype(F32)
        vb = v_ref[0, rows, ls]
        la = _log_sigmoid(z_all[:, ls] + b_ref[:, ls]) * (1.0 / C_TAU)
        r = _dot(cum, jnp.concatenate(_split_bf16(la), axis=1))
        yield
        b = r[:, :LANES] + r[:, LANES:]
        b3 = b.reshape(n_chunks, C_CHUNK, LANES)
        edge = b3[:, C_CHUNK - 1:, :] if forward else b3[:, :1, :]
        tot = jnp.broadcast_to(edge, b3.shape).reshape(tile, LANES)
        qf = q * jnp.exp(b)
        qt = qf.astype(BF16)
        kt = (k * jnp.exp(-b)).astype(BF16)
        k2 = k * jnp.exp(tot - b)
        raw = _dot_nt(qt, kt)
        k2_blk = jnp.concatenate([jnp.where(row_chunk == c, k2, 0.0).astype(BF16) for c in range(n_chunks)], axis=1)
        kv = _dot_tn(vb, k2_blk)
        yield
        intra = _dot(jnp.where(keep, raw, 0.0).astype(BF16), vb)
        dec = jnp.exp(edge)
        q_blk = jnp.concatenate([jnp.where(row_chunk == c, qf, 0.0).astype(BF16) for c in range(n_chunks)], axis=1)
        st = st_ref[si]
        entering = [None] * n_chunks
        for c in (range(n_chunks) if forward else range(n_chunks - 1, -1, -1)):
            entering[c] = st.astype(BF16)
            st = st * dec[c] + kv[:, c * LANES:(c + 1) * LANES]
        st_ref[si] = st
        inter = _dot_nt(q_blk, jnp.concatenate(entering, axis=1))
        yield
        acc_ref[d, rows, ls] = intra + inter

    def body(it, carry):
        tiles = (it, n_tiles - 1 - it)
        z = [_dot(dec_ref[0, pl.ds(pl.multiple_of(tiles[d] * tile, tile), tile), :], (wf_ref, wb_ref)[d][...])
             for d in (0, 1)]
        _run_skewed([tile_step(tiles[d], d, hd, z[d]) for hd in range(n_heads) for d in (0, 1)])
        return carry

    lax.fori_loop(0, n_tiles, body, 0)
    for hd in range(n_heads):
        ls = slice(hd * LANES, (hd + 1) * LANES)
        y = _rms(acc_ref[0, :, ls] + acc_ref[1, :, ls], gco_ref[...])
        gate = g_ref[0, :, ls].astype(F32)
        o_ref[0, :, ls] = (y * (gate * _sigmoid(gate))).astype(o_ref.dtype)


def _gla(cqkvg, dec, wf, wb, bf, bb, gco, bsz, t_len, tile, hp=4):
    c3 = cqkvg.reshape(bsz, t_len, 4 * C_HEADS * LANES)
    d3 = dec.reshape(bsz, t_len, LANES)
    groups = C_HEADS // hp
    wl = hp * LANES

    def cspec(j):
        return pl.BlockSpec((1, t_len, wl), lambda b, h: (b, 0, j * groups + h))

    out = pl.pallas_call(
        functools.partial(_gla_kernel, tile=tile),
        grid=(bsz, groups),
        in_specs=[cspec(0), cspec(1), cspec(2), cspec(3),
                  pl.BlockSpec((1, t_len, LANES), lambda b, h: (b, 0, 0)),
                  pl.BlockSpec((LANES, wl), lambda b, h: (0, h)),
                  pl.BlockSpec((LANES, wl), lambda b, h: (0, h)),
                  pl.BlockSpec((1, wl), lambda b, h: (0, h)),
                  pl.BlockSpec((1, wl), lambda b, h: (0, h)),
                  pl.BlockSpec((1, LANES), lambda b, h: (0, 0))],
        out_specs=pl.BlockSpec((1, t_len, wl), lambda b, h: (b, 0, h)),
        out_shape=jax.ShapeDtypeStruct((bsz, t_len, C_HEADS * LANES), BF16),
        scratch_shapes=[pltpu.VMEM((2, t_len, wl), F32), pltpu.VMEM((2 * hp, C_DV, C_DK), F32)],
        compiler_params=_cparams(("parallel", "parallel")),
        name="gla",
    )(c3, c3, c3, c3, d3, wf, wb, bf, bb, gco)
    return out.reshape(bsz * t_len, C_HEADS * LANES)


def _merge_stages(x, rows, gmix_ref, wgate_ref, bg_ref, y_refs, wp_refs, wo_ref, out):
    h = _rms(x, gmix_ref[...]).astype(BF16)
    dots = []
    for j in range(N_BRANCH):
        sl = slice(j * D_MODEL, (j + 1) * D_MODEL)
        dots.append((_dot(h, wgate_ref[:, sl]), _dot(y_refs[j][rows, :], wp_refs[j][...])))
    yield
    m = None
    for j, (gate, proj) in enumerate(dots):
        term = _sigmoid(gate + bg_ref[:, j * D_MODEL:(j + 1) * D_MODEL]) * proj
        m = term if m is None else m + term
    y = _dot(m.astype(BF16), wo_ref[...])
    yield
    out.append(x + y)


def _merge_tile(x, gmix_ref, wgate_ref, bg_ref, y_refs, wp_refs, wo_ref):
    out = []
    for _ in _merge_stages(x, slice(None), gmix_ref, wgate_ref, bg_ref, y_refs, wp_refs, wo_ref, out):
        pass
    return out[0]


def _merge_specs(tm):
    tok = lambda w: pl.BlockSpec((tm, w), lambda i: (i, 0))
    return [tok(512), tok(512), tok(512),
            _const_spec((1, D_MODEL)),
            _const_spec((D_MODEL, N_BRANCH * D_MODEL)),
            _const_spec((1, N_BRANCH * D_MODEL)),
            _const_spec((512, D_MODEL)), _const_spec((512, D_MODEL)), _const_spec((512, D_MODEL)),
            _const_spec((D_MODEL, D_MODEL))]


def _merge_ffn_kernel(x_ref, ya_ref, yb_ref, yc_ref, gmix_ref, wgate_ref, bg_ref, wpa_ref, wpb_ref, wpc_ref, wo_ref,
                      g_ref, wg_ref, wu_ref, wd_ref, o_ref, *, chunk):
    x = _merge_tile(x_ref[...], gmix_ref, wgate_ref, bg_ref, (ya_ref, yb_ref, yc_ref),
                    (wpa_ref, wpb_ref, wpc_ref), wo_ref)
    h = _rms(x, g_ref[...]).astype(BF16)
    o_ref[...] = x
    for c0 in range(0, D_FF, chunk):
        gt = _dot(h, wg_ref[:, c0:c0 + chunk])
        up = _dot(h, wu_ref[:, c0:c0 + chunk])
        a = (gt * _sigmoid(gt) * up).astype(BF16)
        o_ref[...] += _dot(a, wd_ref[c0:c0 + chunk, :])


def _merge_ffn(x2, merge_args, g, wg, wu, wd, tm, chunk=256):
    n = x2.shape[0]
    return pl.pallas_call(
        functools.partial(_merge_ffn_kernel, chunk=chunk),
        grid=(n // tm,),
        in_specs=[pl.BlockSpec((tm, D_MODEL), lambda i: (i, 0))] + _merge_specs(tm)
                 + [_const_spec((1, D_MODEL)),
                    _const_spec((D_MODEL, D_FF)), _const_spec((D_MODEL, D_FF)), _const_spec((D_FF, D_MODEL))],
        out_specs=pl.BlockSpec((tm, D_MODEL), lambda i: (i, 0)),
        out_shape=jax.ShapeDtypeStruct((n, D_MODEL), F32),
        compiler_params=_cparams(("parallel",)),
        name="merge_ffn",
    )(x2, *merge_args, g, wg, wu, wd)


def _merge_router_kernel(x_ref, ya_ref, yb_ref, yc_ref, gmix_ref, wgate_ref, bg_ref, wpa_ref, wpb_ref, wpc_ref, wo_ref,
                         g_ref, wr_ref, xo_ref, h_ref, gate_ref, rank_ref, rank_t_ref, cnt_ref, tot_ref, carry_ref,
                         *, tb):
    tm = x_ref.shape[0]

    @pl.when(pl.program_id(0) == 0)
    def _():
        carry_ref[...] = jnp.zeros_like(carry_ref)

    lane = lax.broadcasted_iota(jnp.int32, (tb, LANES), 1).astype(F32)
    row = lax.broadcasted_iota(jnp.int32, (tb, tb), 0)
    col = lax.broadcasted_iota(jnp.int32, (tb, tb), 1)
    earlier = jnp.where(col < row, 1.0, 0.0).astype(BF16)
    neg = jnp.float32(-jnp.inf)
    local = []

    def block(u):
        rows = slice(u * tb, (u + 1) * tb)
        merged = []
        yield from _merge_stages(x_ref[rows, :], rows, gmix_ref, wgate_ref, bg_ref, (ya_ref, yb_ref, yc_ref),
                                 (wpa_ref, wpb_ref, wpc_ref), wo_ref, merged)
        x = merged[0]
        xo_ref[rows, :] = x
        hf = _rms(x, g_ref[...])
        h_ref[rows, :] = hf.astype(h_ref.dtype)
        logits = _dot3(hf, wr_ref[...])
        yield
        logits = jnp.where(lane < N_EXPERTS, logits, neg)
        v1 = jnp.max(logits, axis=-1, keepdims=True)
        i1 = jnp.min(jnp.where(logits == v1, lane, float(LANES)), axis=-1, keepdims=True)
        rest = jnp.where(lane == i1, neg, logits)
        v2 = jnp.max(rest, axis=-1, keepdims=True)
        i2 = jnp.min(jnp.where(rest == v2, lane, float(LANES)), axis=-1, keepdims=True)
        e2 = jnp.exp(v2 - v1)
        w1 = 1.0 / (1.0 + e2)
        w2 = e2 / (1.0 + e2)
        gate_ref[rows, :] = jnp.where(lane == i1, w1, 0.0) + jnp.where(lane == i2, w2, 0.0)
        chosen = (lane == i1) | (lane == i2)
        assign = jnp.where(chosen, 1.0, 0.0)
        inside = _dot(earlier, assign.astype(BF16))
        yield
        local.append((chosen, inside, jnp.sum(assign, axis=0, keepdims=True)))

    _run_skewed([block(u) for u in range(tm // tb)])
    carry = carry_ref[...]
    for u, (chosen, inside, count) in enumerate(local):
        rows = slice(u * tb, (u + 1) * tb)
        rank = jnp.where(chosen, inside + carry[0:1, :], -1.0)
        rank_ref[rows, :] = rank
        rank_t_ref[:, rows] = rank.T[:N_EXPERTS, :]
        cnt_ref[u] = carry
        carry = carry + count
    carry_ref[...] = carry
    tot_ref[...] = carry


def _merge_router(x2, merge_args, g, wr_pad, tm, tb):
    n = x2.shape[0]
    nt = n // tm
    return pl.pallas_call(
        functools.partial(_merge_router_kernel, tb=tb),
        grid=(nt,),
        in_specs=[pl.BlockSpec((tm, D_MODEL), lambda i: (i, 0))] + _merge_specs(tm)
                 + [_const_spec((1, D_MODEL)),
                    _const_spec((D_MODEL, LANES))],
        out_specs=[pl.BlockSpec((tm, D_MODEL), lambda i: (i, 0)),
                   pl.BlockSpec((tm, D_MODEL), lambda i: (i, 0)),
                   pl.BlockSpec((tm, LANES), lambda i: (i, 0)),
                   pl.BlockSpec((tm, LANES), lambda i: (i, 0)),
                   pl.BlockSpec((N_EXPERTS, tm), lambda i: (0, i)),
                   pl.BlockSpec((tm // tb, 8, LANES), lambda i: (i, 0, 0)),
                   pl.BlockSpec((8, LANES), lambda i: (0, 0))],
        out_shape=[jax.ShapeDtypeStruct((n, D_MODEL), F32),
                   jax.ShapeDtypeStruct((n, D_MODEL), BF16),
                   jax.ShapeDtypeStruct((n, LANES), F32),
                   jax.ShapeDtypeStruct((n, LANES), F32),
                   jax.ShapeDtypeStruct((N_EXPERTS, n), F32),
                   jax.ShapeDtypeStruct((n // tb, 8, LANES), F32),
                   jax.ShapeDtypeStruct((8, LANES), F32)],
        scratch_shapes=[pltpu.VMEM((8, LANES), F32)],
        compiler_params=_cparams(("arbitrary",)),
        name="merge_router",
    )(x2, *merge_args, g, wr_pad)


def _moe_plan(cnt, tot, n, tb, rs, tg):
    i32 = jnp.int32
    counts = tot[0, :N_EXPERTS].astype(i32)
    cum = jnp.concatenate([cnt[:, 0, :N_EXPERTS], tot[:1, :N_EXPERTS]], axis=0).astype(i32)
    n_sb = (counts + rs - 1) // rs
    sb_end = jnp.cumsum(n_sb)
    sb_start = sb_end - n_sb
    goff = sb_start * rs
    n_valid_sb = sb_end[-1]
    s_max = 2 * n // rs + N_EXPERTS
    s_ids = jnp.arange(s_max, dtype=i32)
    last_sb = n_valid_sb - 1
    s_eff = jnp.minimum(s_ids, last_sb)
    sb_expert = jnp.minimum(jnp.sum(s_eff[:, None] >= sb_end[None, :], axis=1), N_EXPERTS - 1).astype(i32)
    sb_rows = jnp.clip(counts[sb_expert] - (s_eff - sb_start[sb_expert]) * rs, 0, rs)
    sb_tiles = ((sb_rows + tb - 1) // tb).astype(i32)

    t_ids = jnp.arange(s_max * rs // tb, dtype=i32)
    t_sb = t_ids * tb // rs
    t_exp = sb_expert[jnp.minimum(t_sb, last_sb)]
    t_r0 = t_ids * tb - goff[t_exp]
    t_valid = (t_sb <= last_sb) & (t_r0 < counts[t_exp])
    t_r1 = jnp.minimum(t_r0 + tb, counts[t_exp])
    cum_e = cum[:, t_exp]
    t_lo = jnp.sum(cum_e[1:] <= t_r0[None, :], axis=0)
    t_hi = jnp.where(t_valid, jnp.sum(cum_e[:-1] < t_r1[None, :], axis=0), t_lo)
    gather_tabs = tuple(a.astype(i32) for a in (t_exp, t_r0, t_lo, t_hi))

    start = goff[None, :] + cum[:-1]
    num = cum[1:] - cum[:-1]
    n_rows = s_max * rs
    off0 = jnp.minimum(start // ROW_ALIGN * ROW_ALIGN, n_rows - tb)
    need1 = (num > 0) & (start + num > off0 + tb)
    off1 = jnp.where(need1, jnp.minimum(off0 + tb, n_rows - tb), 0)
    offs = jnp.stack([off0, off1], axis=-1).reshape(-1).astype(i32)
    expert_tabs = (sb_expert, sb_tiles, last_sb.reshape(1).astype(i32))
    return gather_tabs, expert_tabs, (offs, need1.reshape(-1).astype(i32), goff.astype(i32))


def _gather_kernel(exp_ref, r0_ref, lo_ref, hi_ref, rank_ref, h_ref, o_ref, acc_ref, *, tb):
    step = pl.program_id(0)
    n_sub = o_ref.shape[0] // tb
    n_blk = rank_ref.shape[1]
    row = lax.broadcasted_iota(jnp.int32, (tb, tb), 0)
    tiles = [step * n_sub + u for u in range(n_sub)]
    n_trips = functools.reduce(jnp.maximum, [(hi_ref[t] - lo_ref[t] + 1) >> 1 for t in tiles])

    @pl.when(n_trips == 0)
    def _():
        o_ref[...] = jnp.zeros_like(o_ref)

    @pl.when(n_trips > 0)
    def _():
        acc_ref[...] = jnp.zeros_like(acc_ref)

        def body(p, carry):
            parts = []
            for t in tiles:
                e, lo, hi = exp_ref[t], lo_ref[t], hi_ref[t]
                want = row + r0_ref[t]
                for j in range(2):
                    kb = lo + 2 * p + j
                    kbc = jnp.minimum(kb, n_blk - 1)
                    rk = rank_ref[e, pl.ds(kbc, 1), :].astype(jnp.int32)
                    onehot = jnp.where(jnp.where(kb < hi, rk, -1) == want, 1.0, 0.0).astype(BF16)
                    parts.append(_dot(onehot, h_ref[pl.ds(pl.multiple_of(kbc * tb, tb), tb), :]))
            for u in range(n_sub):
                acc_ref[u * tb:(u + 1) * tb, :] += parts[2 * u] + parts[2 * u + 1]
            return carry

        lax.fori_loop(0, n_trips, body, 0)
        o_ref[...] = acc_ref[...].astype(o_ref.dtype)


def _gather(tabs, rank_t3, h, n_rows, tg, tb):
    n = h.shape[0]
    return pl.pallas_call(
        functools.partial(_gather_kernel, tb=tb),
        grid_spec=pltpu.PrefetchScalarGridSpec(
            num_scalar_prefetch=4,
            grid=(n_rows // tg,),
            in_specs=[_const_spec((N_EXPERTS, n // tb, tb)),
                      _const_spec((n, D_MODEL))],
            out_specs=pl.BlockSpec((tg, D_MODEL), lambda t, e, r, lo, hi: (t, 0)),
            scratch_shapes=[pltpu.VMEM((tg, D_MODEL), F32)]),
        out_shape=jax.ShapeDtypeStruct((n_rows, D_MODEL), BF16),
        compiler_params=_cparams(("arbitrary",)),
        name="moe_gather",
    )(*tabs, rank_t3, h)


def _experts_kernel(exp_ref, tiles_ref, last_ref, xs_ref, wg_ref, wu_ref, wd_ref, o_ref,
                    acc_ref, wgb_ref, wub_ref, wdb_ref, *, tb):
    s, c = pl.program_id(0), pl.program_id(1)
    nc = pl.num_programs(1)

    @pl.when((s > last_ref[0]) & (c == 0))
    def _():
        o_ref[...] = jnp.zeros_like(o_ref)

    @pl.when(s <= last_ref[0])
    def _():
        @pl.when(c == 0)
        def _():
            acc_ref[...] = jnp.zeros_like(acc_ref)

        wgb_ref[...] = wg_ref[0].astype(BF16)
        wub_ref[...] = wu_ref[0].astype(BF16)
        wdb_ref[...] = wd_ref[0].astype(BF16)

        def tile_step(t):
            rows = pl.ds(pl.multiple_of(t * tb, tb), tb)
            x = xs_ref[rows, :]
            gt = _dot(x, wgb_ref[...])
            up = _dot(x, wub_ref[...])
            yield
            a = (gt * _sigmoid(gt) * up).astype(BF16)
            y = _dot(a, wdb_ref[...])
            yield
            acc_ref[rows, :] += y

        def body(t4, carry):
            _run_skewed([tile_step(4 * t4 + u) for u in range(4)])
            return carry

        n_tiles = tiles_ref[s]
        lax.fori_loop(0, n_tiles >> 2, body, 0)
        done = n_tiles & ~3

        @pl.when((n_tiles & 2) == 2)
        def _():
            _run_skewed([tile_step(done), tile_step(done + 1)])

        @pl.when((n_tiles & 1) == 1)
        def _():
            _run_skewed([tile_step(n_tiles - 1)])

        @pl.when(c == nc - 1)
        def _():
            o_ref[...] = acc_ref[...].astype(o_ref.dtype)


def _experts(tabs, xs, wg, wu, wd, rs, tb, fc):
    n_rows = xs.shape[0]
    nc = D_FF_EXPERT // fc

    def sb(s, last):
        return jnp.minimum(s, last[0])

    def chunk(s, c, last):
        return jnp.where(s <= last[0], c, nc - 1)

    return pl.pallas_call(
        functools.partial(_experts_kernel, tb=tb),
        grid_spec=pltpu.PrefetchScalarGridSpec(
            num_scalar_prefetch=3,
            grid=(n_rows // rs, nc),
            in_specs=[pl.BlockSpec((rs, D_MODEL), lambda s, c, ex, tl, last: (sb(s, last), 0)),
                      pl.BlockSpec((1, D_MODEL, fc), lambda s, c, ex, tl, last: (ex[sb(s, last)], 0, chunk(s, c, last))),
                      pl.BlockSpec((1, D_MODEL, fc), lambda s, c, ex, tl, last: (ex[sb(s, last)], 0, chunk(s, c, last))),
                      pl.BlockSpec((1, fc, D_MODEL), lambda s, c, ex, tl, last: (ex[sb(s, last)], chunk(s, c, last), 0))],
            out_specs=pl.BlockSpec((rs, D_MODEL), lambda s, c, ex, tl, last: (s, 0)),
            scratch_shapes=[pltpu.VMEM((rs, D_MODEL), F32),
                            pltpu.VMEM((D_MODEL, fc), BF16), pltpu.VMEM((D_MODEL, fc), BF16),
                            pltpu.VMEM((fc, D_MODEL), BF16)]),
        out_shape=jax.ShapeDtypeStruct((n_rows, D_MODEL), BF16),
        compiler_params=_cparams(("arbitrary", "arbitrary")),
        name="moe_experts",
    )(*tabs, xs, wg, wu, wd)


def _combine_kernel(off_ref, need_ref, goff_ref, x_ref, gate_ref, rank_ref, *rest, final_norm):
    y_refs, (gfin_ref, o_ref) = rest[:2 * N_EXPERTS], rest[2 * N_EXPERTS:]
    i = pl.program_id(0)
    tt = x_ref.shape[0]
    tb = y_refs[0].shape[0]
    lane = lax.broadcasted_iota(jnp.int32, (tt, LANES), 1)
    col = lax.broadcasted_iota(jnp.int32, (tt, tb), 1)
    gate, rank = gate_ref[...], rank_ref[...]

    def expert_terms(e):
        ge = jnp.sum(jnp.where(lane == e, gate, 0.0), axis=-1, keepdims=True)
        rk = jnp.sum(jnp.where(lane == e, rank, 0.0), axis=-1, keepdims=True).astype(jnp.int32)
        return ge, jnp.where(rk >= 0, rk + goff_ref[e], -1)

    def window(e, j, pos):
        onehot = jnp.where(pos - off_ref[(i * N_EXPERTS + e) * 2 + j] == col, 1.0, 0.0).astype(BF16)
        return _dot(onehot, y_refs[2 * e + j][...])

    y = x_ref[...]
    for e in range(N_EXPERTS):
        ge, pos = expert_terms(e)
        y = y + ge * window(e, 0, pos)
    o_ref[...] = y
    for e in range(N_EXPERTS):
        @pl.when(need_ref[i * N_EXPERTS + e] == 1)
        def _(e=e):
            ge, pos = expert_terms(e)
            o_ref[...] += ge * window(e, 1, pos)
    if final_norm:
        o_ref[...] = _rms(o_ref[...], gfin_ref[...])


def _combine(tabs, x2, gate, rank, ys, gfin, tt, tb, final_norm):
    n = x2.shape[0]

    def yspec(slot):
        return pl.BlockSpec((pl.Element(tb), pl.Element(D_MODEL)),
                            lambda i, off, need, goff: (pl.multiple_of(off[i * 2 * N_EXPERTS + slot], ROW_ALIGN), 0))

    tok = lambda w: pl.BlockSpec((tt, w), lambda i, off, need, goff: (i, 0))
    return pl.pallas_call(
        functools.partial(_combine_kernel, final_norm=final_norm),
        grid_spec=pltpu.PrefetchScalarGridSpec(
            num_scalar_prefetch=3,
            grid=(n // tt,),
            in_specs=[tok(D_MODEL), tok(LANES), tok(LANES)] + [yspec(k) for k in range(2 * N_EXPERTS)]
                     + [pl.BlockSpec((1, D_MODEL), lambda i, off, need, goff: (0, 0))],
            out_specs=tok(D_MODEL)),
        out_shape=jax.ShapeDtypeStruct((n, D_MODEL), F32),
        compiler_params=_cparams(("arbitrary",)),
        name="moe_combine",
    )(*tabs, x2, gate, rank, *([ys] * (2 * N_EXPERTS)), gfin)


def _merge_moe(x2, merge_args, g_ffn, w_router, wg, wu, wd, gfin, final_norm):
    n = x2.shape[0]
    tb = min(256, n // 2)
    rs = max(9 * n // (8 * N_EXPERTS) // tb * tb, 2 * tb)
    tg = next(k * tb for k in (3, 2, 1) if rs % (k * tb) == 0)
    wr = jnp.pad(w_router, ((0, 0), (0, LANES - N_EXPERTS)))
    x2, h, gate, rank, rank_t, cnt, tot = _merge_router(x2, merge_args, g_ffn, wr, min(2 * tb, n), tb)
    gather_tabs, expert_tabs, combine_tabs = _moe_plan(cnt, tot, n, tb, rs, tg)
    n_rows = (2 * n // rs + N_EXPERTS) * rs
    xs = _gather(gather_tabs, rank_t.reshape(N_EXPERTS, n // tb, tb), h, n_rows, tg, tb)
    ys = _experts(expert_tabs, xs, wg, wu, wd, rs, tb, 512)
    return _combine(combine_tabs, x2, gate, rank, ys, gfin, tb, tb, final_norm)


def _final_norm_kernel(x_ref, g_ref, o_ref):
    o_ref[...] = _rms(x_ref[...], g_ref[...])


def _final_norm(x2, g, tm):
    n = x2.shape[0]
    return pl.pallas_call(
        _final_norm_kernel,
        grid=(n // tm,),
        in_specs=[pl.BlockSpec((tm, D_MODEL), lambda i: (i, 0)), _const_spec((1, D_MODEL))],
        out_specs=pl.BlockSpec((tm, D_MODEL), lambda i: (i, 0)),
        out_shape=jax.ShapeDtypeStruct((n, D_MODEL), F32),
        compiler_params=_cparams(("parallel",)),
        name="final_norm",
    )(x2, g)


def _rope_tables(t_len):
    rows = t_len // GRID_W
    row = np.repeat(np.arange(rows, dtype=np.float64), GRID_W)
    col = np.tile(np.arange(GRID_W, dtype=np.float64), rows)
    lane = np.arange(LANES)

    def table(rot_dim, lane_in_slice, active):
        n_freq = rot_dim // 4
        inv_freq = ROPE_THETA ** (-np.arange(n_freq, dtype=np.float64) / n_freq)
        freq = inv_freq[lane_in_slice % n_freq]
        use_row = (lane_in_slice % rot_dim) < (rot_dim // 2)
        ang = np.where(use_row[None, :], row[:, None], col[:, None]) * freq[None, :]
        sign = np.where((lane_in_slice % (rot_dim // 2)) < n_freq, -1.0, 1.0)
        cos = np.where(active[None, :], np.cos(ang), 1.0)
        sin = np.where(active[None, :], np.sin(ang) * sign[None, :], 0.0)
        return jnp.asarray(cos, F32), jnp.asarray(sin, F32)

    cos_a, sin_a = table(A_HEAD_DIM, lane % A_HEAD_DIM, np.ones(LANES, bool))
    in_rope = (lane >= B_NOPE_DIM) & (lane < B_NOPE_DIM + B_ROPE_DIM)
    cos_b, sin_b = table(B_ROPE_DIM, (lane - B_NOPE_DIM) % B_ROPE_DIM, in_rope)
    return cos_a, sin_a, cos_b, sin_b


def _pack_w_in(w):
    o_c, o_gate = 1312, 3392
    zeros = lambda k: jnp.zeros((D_MODEL, k), BF16)
    cols = [w[:, :o_c].astype(BF16), zeros(LANES - B_ROPE_DIM), w[:, o_c:o_gate].astype(BF16),
            zeros(LANES - 2 * C_GATE_RANK)]
    return jnp.concatenate(cols, axis=1), w[:, o_gate:].astype(BF16)


def _pack_b_weights(w_q_up, w_kv_up):
    hq = w_q_up.reshape(B_Q_RANK, B_HEADS, B_NOPE_DIM + B_ROPE_DIM)
    wq = jnp.pad(hq, ((0, 0), (0, 0), (0, LANES - B_NOPE_DIM - B_ROPE_DIM))).reshape(B_Q_RANK, B_HEADS * LANES)
    hkv = w_kv_up.reshape(B_KV_RANK, B_HEADS, B_NOPE_DIM + B_V_DIM)
    wk = jnp.pad(hkv[:, :, :B_NOPE_DIM], ((0, 0), (0, 0), (0, LANES - B_NOPE_DIM))).reshape(B_KV_RANK, B_HEADS * LANES)
    wv = hkv[:, :, B_NOPE_DIM:].reshape(B_KV_RANK, B_HEADS * B_V_DIM)
    return wq.astype(BF16), wk.astype(BF16), wv.astype(BF16)


def _rope_placement():
    e = np.zeros((LANES, LANES), np.float32)
    e[np.arange(B_ROPE_DIM), B_NOPE_DIM + np.arange(B_ROPE_DIM)] = 1.0
    return jnp.asarray(e, BF16)


def kernel(x, w_in, b_gate, g_mix, g_a_q, g_a_k, g_b_q, w_b_q_up, g_b_kv, w_b_kv_up, w_c_af_up, b_c_af, w_c_ab_up, b_c_ab, g_c_out, w_pa, w_pb, w_pc, w_out, g_ffn, w_ff_gate, w_ff_up, w_ff_down, w_router, w_e_gate, w_e_up, w_e_down, g_final):
    bsz, t_len, _ = x.shape
    n = bsz * t_len
    depth = w_in.shape[0]
    tm = min(512, t_len)
    tq_a = min(512, t_len)
    tq_b = min(512, t_len)
    gla_tile = min(256, t_len)

    cos_a, sin_a, cos_b, sin_b = _rope_tables(t_len)
    e_mat = _rope_placement()
    row = lambda v: v.reshape(1, -1).astype(F32)
    x2 = x.reshape(n, D_MODEL)

    for i in range(depth):
        wq, wk, wv = _pack_b_weights(w_b_q_up[i], w_b_kv_up[i])
        w_packed, w_gate = _pack_w_in(w_in[i])
        qa, ka, av2, qb, kb, vb, cqkvg, dec = _fused_in(
            x2, row(g_mix[i]), w_packed, row(jnp.tile(g_a_q[i], 2)), row(jnp.tile(g_a_k[i], 2)),
            cos_a, sin_a, row(g_b_q[i]), row(g_b_kv[i]), wq, wk, wv, e_mat, cos_b, sin_b, tm, t_len)
        ya = _attn_a(qa, ka, av2, bsz, t_len, tq_a)
        yb = _attn_b(qb, kb, vb, bsz, t_len, tq_b)

        wf = jnp.pad(w_c_af_up[i], ((0, LANES - C_GATE_RANK), (0, 0))).astype(BF16)
        wb = jnp.pad(w_c_ab_up[i], ((C_GATE_RANK, LANES - 2 * C_GATE_RANK), (0, 0))).astype(BF16)
        yc = _gla(cqkvg, dec, wf, wb, row(b_c_af[i]), row(b_c_ab[i]), row(g_c_out[i]), bsz, t_len, gla_tile)

        merge_args = (ya, yb, yc, row(g_mix[i]), w_gate, row(b_gate[i]), w_pa[i].astype(BF16), w_pb[i].astype(BF16),
                      w_pc[i].astype(BF16), w_out[i].astype(BF16))
        j = i // 2
        last = i == depth - 1
        if i % 2 == 0:
            x2 = _merge_ffn(x2, merge_args, row(g_ffn[i]), w_ff_gate[j].astype(BF16), w_ff_up[j].astype(BF16),
                            w_ff_down[j].astype(BF16), tm)
            if last:
                x2 = _final_norm(x2, row(g_final), tm)
        else:
            x2 = _merge_moe(x2, merge_args, row(g_ffn[i]), w_router[j], w_e_gate[j], w_e_up[j], w_e_down[j],
                            row(g_final), last)
    return x2.reshape(bsz, t_len, D_MODEL)
```

```python
import functools

import jax
import jax.numpy as jnp
import numpy as np
from jax import lax
from jax.experimental import pallas as pl
from jax.experimental.pallas import tpu as pltpu

F32 = jnp.float32
BF16 = jnp.bfloat16

D_MODEL = 1024
GRID_W = 64
ROPE_THETA = 10000.0
EPS = 1e-6
A_HEADS, A_KV_HEADS, A_HEAD_DIM = 8, 2, 64
B_HEADS, B_NOPE_DIM, B_ROPE_DIM, B_V_DIM = 8, 64, 32, 64
B_Q_RANK = B_KV_RANK = 256
C_HEADS, C_DK, C_DV, C_GATE_RANK, C_TAU, C_CHUNK = 4, 128, 128, 16, 16.0, 64
N_BRANCH = 3
D_FF = 2816
N_EXPERTS, TOP_K, D_FF_EXPERT = 8, 2, 3584

LANES = 128
ROW_ALIGN = 16
VMEM_LIMIT = 56 * 1024 * 1024


def _cparams(sem):
    return pltpu.CompilerParams(dimension_semantics=sem, vmem_limit_bytes=VMEM_LIMIT)


def _const_spec(shape):
    nd = len(shape)
    return pl.BlockSpec(shape, lambda *_: (0,) * nd, pipeline_mode=pl.Buffered(1))


def _rms(xf, g):
    return xf * lax.rsqrt(jnp.mean(xf * xf, axis=-1, keepdims=True) + EPS) * g


def _sigmoid(x):
    return 1.0 / (1.0 + jnp.exp(-x))


def _split_bf16(x):
    hi = x.astype(BF16)
    lo = (x - hi.astype(F32)).astype(BF16)
    return hi, lo


def _dot(a, b):
    return jnp.dot(a, b, preferred_element_type=F32)


def _dot_nt(a, b):
    return lax.dot_general(a, b, (((1,), (1,)), ((), ())), preferred_element_type=F32)


def _dot_tn(a, b):
    return lax.dot_general(a, b, (((0,), (0,)), ((), ())), preferred_element_type=F32)


def _rope_group(x, cos, sin_signed, lane, half):
    fwd = pltpu.roll(x, LANES - half, axis=1)
    bwd = pltpu.roll(x, half, axis=1)
    swapped = jnp.where((lane % (2 * half)) < half, fwd, bwd)
    return x * cos + swapped * sin_signed


def _seg64_meansq(x, lane):
    sq = x * x
    lo = jnp.sum(jnp.where(lane < 64, sq, 0.0), axis=-1, keepdims=True)
    hi = jnp.sum(jnp.where(lane < 64, 0.0, sq), axis=-1, keepdims=True)
    return jnp.where(lane < 64, lo, hi) * (1.0 / 64.0)


F_AQ, F_AKV, F_B, F_C, F_DEC, F_WIDTH = 0, 512, 768, 1408, 3456, 3584


def _dup_half(x, lane, low):
    other = pltpu.roll(x, 64, axis=1)
    return jnp.where(lane < 64, x, other) if low else jnp.where(lane < 64, other, x)


def _fused_in_kernel(x_ref, g_ref, w_ref, gaq_ref, gak_ref, cosa_ref, sina_ref,
                     gbq_ref, gbkv_ref, wq_ref, wk_ref, wv_ref, e_ref, cosb_ref, sinb_ref,
                     qa_ref, ka_ref, va_ref, qb_ref, kb_ref, vb_ref, c_ref, dec_ref):
    rows = x_ref.shape[0]
    lane = lax.broadcasted_iota(jnp.int32, (rows, LANES), 1)
    h = _rms(x_ref[...], g_ref[...]).astype(BF16)

    def proj(off, width):
        return _dot(h, w_ref[:, off:off + width])

    def chain_aq():
        y = proj(F_AQ, 512)
        yield
        cos, sin, gq = cosa_ref[...], sina_ref[...], gaq_ref[...]
        for gi in range(4):
            sl = slice(gi * LANES, (gi + 1) * LANES)
            x = y[:, sl]
            x = x * lax.rsqrt(_seg64_meansq(x, lane) + EPS) * gq
            qa_ref[:, sl] = (_rope_group(x, cos, sin, lane, 16) * (A_HEAD_DIM ** -0.5)).astype(qa_ref.dtype)

    def chain_akv():
        y = proj(F_AKV, 256)
        yield
        k, v = y[:, :LANES], y[:, LANES:]
        k = k * lax.rsqrt(_seg64_meansq(k, lane) + EPS) * gak_ref[...]
        k = _rope_group(k, cosa_ref[...], sina_ref[...], lane, 16)
        for kv in range(A_KV_HEADS):
            sl = slice(kv * LANES, (kv + 1) * LANES)
            ka_ref[:, sl] = _dup_half(k, lane, kv == 0).astype(ka_ref.dtype)
            va_ref[:, sl] = _dup_half(v, lane, kv == 0).astype(va_ref.dtype)

    def chain_b():
        y = proj(F_B, 640)
        yield
        cq = _rms(y[:, :B_Q_RANK], gbq_ref[...]).astype(BF16)
        ckv = _rms(y[:, B_Q_RANK:2 * B_Q_RANK], gbkv_ref[...]).astype(BF16)
        q = _dot(cq, wq_ref[...])
        k = _dot(ckv, wk_ref[...])
        v = _dot(ckv, wv_ref[...])
        kr = _dot(y[:, 2 * B_Q_RANK:].astype(BF16), e_ref[...])
        yield
        cos, sin = cosb_ref[...], sinb_ref[...]
        kr = _rope_group(kr, cos, sin, lane, 8)
        scale = (B_NOPE_DIM + B_ROPE_DIM) ** -0.5
        for hd in range(B_HEADS):
            sl = slice(hd * LANES, (hd + 1) * LANES)
            qb_ref[:, sl] = (_rope_group(q[:, sl], cos, sin, lane, 8) * scale).astype(qb_ref.dtype)
            kb_ref[:, sl] = (k[:, sl] + kr).astype(kb_ref.dtype)
        vb_ref[...] = v.astype(vb_ref.dtype)

    def chain_plain(o_ref, off, c0, cw):
        y = proj(off + c0, cw)
        yield
        o_ref[:, c0:c0 + cw] = y.astype(o_ref.dtype)

    chains = [chain_aq(), chain_akv(), chain_b()]
    chains += [chain_plain(c_ref, F_C, c0, 512) for c0 in range(0, 2048, 512)]
    chains += [chain_plain(dec_ref, F_DEC, 0, LANES)]
    _run_skewed(chains)


def _fused_in(x2, g, w_packed, gaq, gak, cos_a, sin_a, gbq, gbkv, wq, wk, wv, e_mat, cos_b, sin_b, tm, t_len):
    n = x2.shape[0]
    nt = t_len // tm
    widths = (512, 256, 256, 1024, 1024, 512, 2048, LANES)
    table = pl.BlockSpec((tm, LANES), lambda i: (i % nt, 0))
    return pl.pallas_call(
        _fused_in_kernel,
        grid=(n // tm,),
        in_specs=[pl.BlockSpec((tm, D_MODEL), lambda i: (i, 0)),
                  _const_spec((1, D_MODEL)),
                  _const_spec((D_MODEL, F_WIDTH)),
                  _const_spec((1, LANES)), _const_spec((1, LANES)), table, table,
                  _const_spec((1, B_Q_RANK)), _const_spec((1, B_KV_RANK)),
                  _const_spec((B_Q_RANK, B_HEADS * LANES)),
                  _const_spec((B_KV_RANK, B_HEADS * LANES)),
                  _const_spec((B_KV_RANK, B_HEADS * B_V_DIM)),
                  _const_spec((LANES, LANES)), table, table],
        out_specs=[pl.BlockSpec((tm, w), lambda i: (i, 0)) for w in widths],
        out_shape=[jax.ShapeDtypeStruct((n, w), BF16) for w in widths],
        compiler_params=_cparams(("parallel",)),
        name="fused_in",
    )(x2, g, w_packed, gaq, gak, cos_a, sin_a, gbq, gbkv, wq, wk, wv, e_mat, cos_b, sin_b)


CHAIN_ROWS = 512


def _run_skewed(chains):
    pending, live = list(chains), []
    while pending or live:
        if pending:
            live.append(pending.pop(0))
        for g in reversed(list(live)):
            try:
                next(g)
            except StopIteration:
                live.remove(g)


def _attend(q, k, v3_ref, out):
    s = _dot_nt(q(), k())
    yield
    m = jnp.max(s, axis=-1, keepdims=True)
    p = jnp.exp(s - m).astype(BF16)
    r = _dot(p, v3_ref[...])
    yield
    out.append(r[:, :LANES] / r[:, LANES:])


def _attn_a_kernel(q_ref, k_ref, v_ref, o_ref, v3_ref):
    tq = q_ref.shape[1]
    n_kv = v3_ref.shape[0]

    @pl.when(pl.program_id(2) == 0)
    def _():
        for kv in range(n_kv):
            v3_ref[kv, :, :LANES] = v_ref[0, :, kv * LANES:(kv + 1) * LANES]
            v3_ref[kv, :, LANES:] = jnp.ones((v_ref.shape[1], LANES), BF16)

    rq = min(tq, CHAIN_ROWS)
    lane = lax.broadcasted_iota(jnp.int32, (rq, LANES), 1)
    first = lane < 64
    outs, chains = [], []
    for r0 in range(0, tq, rq):
        for gi in range(2 * n_kv):
            kv = gi // 2
            k = lambda kv=kv: k_ref[0, :, kv * LANES:(kv + 1) * LANES]
            for keep_first in (True, False):
                def q(gi=gi, keep_first=keep_first, r0=r0):
                    qp = q_ref[0, r0:r0 + rq, gi * LANES:(gi + 1) * LANES].astype(F32)
                    return jnp.where(first == keep_first, qp, 0.0).astype(BF16)
                chains.append(_attend(q, k, v3_ref.at[kv], outs))
    _run_skewed(chains)
    for ri, r0 in enumerate(range(0, tq, rq)):
        for gi in range(2 * n_kv):
            o0, o1 = outs[(ri * 2 * n_kv + gi) * 2], outs[(ri * 2 * n_kv + gi) * 2 + 1]
            o_ref[0, r0:r0 + rq, gi * LANES:(gi + 1) * LANES] = jnp.where(first, o0, o1).astype(o_ref.dtype)


def _attn_a(q, k2, v2, bsz, t_len, tq, n_kv=2):
    q3, k3, v3 = (a.reshape(bsz, t_len, a.shape[-1]) for a in (q, k2, v2))
    out = pl.pallas_call(
        _attn_a_kernel,
        grid=(bsz, A_KV_HEADS // n_kv, t_len // tq),
        in_specs=[pl.BlockSpec((1, tq, 256 * n_kv), lambda b, g, i: (b, i, g)),
                  pl.BlockSpec((1, t_len, LANES * n_kv), lambda b, g, i: (b, 0, g)),
                  pl.BlockSpec((1, t_len, LANES * n_kv), lambda b, g, i: (b, 0, g))],
        out_specs=pl.BlockSpec((1, tq, 256 * n_kv), lambda b, g, i: (b, i, g)),
        out_shape=jax.ShapeDtypeStruct((bsz, t_len, 512), BF16),
        scratch_shapes=[pltpu.VMEM((n_kv, t_len, 2 * LANES), BF16)],
        compiler_params=_cparams(("parallel", "parallel", "arbitrary")),
        name="attn_a",
    )(q3, k3, v3)
    return out.reshape(bsz * t_len, 512)


def _attn_b_kernel(q_ref, k_ref, v_ref, o_ref, v3_ref):
    tq = q_ref.shape[1]
    n_pairs = v3_ref.shape[0]

    @pl.when(pl.program_id(2) == 0)
    def _():
        for pi in range(n_pairs):
            v3_ref[pi, :, :LANES] = v_ref[0, :, pi * LANES:(pi + 1) * LANES]
            v3_ref[pi, :, LANES:] = jnp.ones((v_ref.shape[1], LANES), BF16)

    lane = lax.broadcasted_iota(jnp.int32, (tq, LANES), 1)
    outs, chains = [], []
    for hd in range(2 * n_pairs):
        sl = slice(hd * LANES, (hd + 1) * LANES)
        q = lambda sl=sl: q_ref[0, :, sl]
        k = lambda sl=sl: k_ref[0, :, sl]
        chains.append(_attend(q, k, v3_ref.at[hd // 2], outs))
    _run_skewed(chains)
    for pi in range(n_pairs):
        o_ref[0, :, pi * LANES:(pi + 1) * LANES] = jnp.where(lane < 64, outs[2 * pi], outs[2 * pi + 1]).astype(o_ref.dtype)


def _attn_b(q, k, v, bsz, t_len, tq, n_pairs=4):
    q3, k3, v3 = (a.reshape(bsz, t_len, a.shape[-1]) for a in (q, k, v))
    out = pl.pallas_call(
        _attn_b_kernel,
        grid=(bsz, B_HEADS // (2 * n_pairs), t_len // tq),
        in_specs=[pl.BlockSpec((1, tq, 256 * n_pairs), lambda b, p, i: (b, i, p)),
                  pl.BlockSpec((1, t_len, 256 * n_pairs), lambda b, p, i: (b, 0, p)),
                  pl.BlockSpec((1, t_len, LANES * n_pairs), lambda b, p, i: (b, 0, p))],
        out_specs=pl.BlockSpec((1, tq, LANES * n_pairs), lambda b, p, i: (b, i, p)),
        out_shape=jax.ShapeDtypeStruct((bsz, t_len, 512), BF16),
        scratch_shapes=[pltpu.VMEM((n_pairs, t_len, 2 * LANES), BF16)],
        compiler_params=_cparams(("parallel", "parallel", "arbitrary")),
        name="attn_b",
    )(q3, k3, v3)
    return out.reshape(bsz * t_len, 512)


def _log_sigmoid(z):
    return jnp.minimum(z, 0.0) - jnp.log(1.0 + jnp.exp(-jnp.abs(z)))


def _gla_kernel(q_ref, k_ref, v_ref, g_ref, dec_ref, wf_ref, wb_ref, bf_ref, bb_ref, gco_ref,
                o_ref, acc_ref, st_ref, *, tile):
    t_len = q_ref.shape[1]
    n_tiles = t_len // tile
    n_chunks = tile // C_CHUNK
    row = lax.broadcasted_iota(jnp.int32, (tile, tile), 0)
    col = lax.broadcasted_iota(jnp.int32, (tile, tile), 1)
    same = (row // C_CHUNK) == (col // C_CHUNK)
    row_chunk = lax.broadcasted_iota(jnp.int32, (tile, LANES), 0) // C_CHUNK
    keep_f = same & (col <= row)
    keep_b = same & (col >= row)
    lower = jnp.where(keep_f, 1.0, 0.0).astype(BF16)
    upper = jnp.where(keep_b, 1.0, 0.0).astype(BF16)
    q_scale = C_DK ** -0.5
    st_ref[...] = jnp.zeros_like(st_ref)

    n_heads = q_ref.shape[2] // LANES

    def tile_step(j, d, hd, z_all):
        forward = d == 0
        b_ref = bf_ref if forward else bb_ref
        cum, keep = (lower, keep_f) if forward else (upper, keep_b)
        rows = pl.ds(pl.multiple_of(j * tile, tile), tile)
        ls = slice(hd * LANES, (hd + 1) * LANES)
        si = d * n_heads + hd
        q = q_ref[0, rows, ls].astype(F32) * q_scale
        k = k_ref[0, rows, ls].astype(F32)
        vb = v_ref[0, rows, ls]
        la = _log_sigmoid(z_all[:, ls] + b_ref[:, ls]) * (1.0 / C_TAU)
        r = _dot(cum, jnp.concatenate(_split_bf16(la), axis=1))
        yield
        b = r[:, :LANES] + r[:, LANES:]
        b3 = b.reshape(n_chunks, C_CHUNK, LANES)
        edge = b3[:, C_CHUNK - 1:, :] if forward else b3[:, :1, :]
        tot = jnp.broadcast_to(edge, b3.shape).reshape(tile, LANES)
        qf = q * jnp.exp(b)
        qt = qf.astype(BF16)
        kt = (k * jnp.exp(-b)).astype(BF16)
        k2 = k * jnp.exp(tot - b)
        raw = _dot_nt(qt, kt)
        k2_blk = jnp.concatenate([jnp.where(row_chunk == c, k2, 0.0).astype(BF16) for c in range(n_chunks)], axis=1)
        kv = _dot_tn(vb, k2_blk)
        yield
        intra = _dot(jnp.where(keep, raw, 0.0).astype(BF16), vb)
        dec = jnp.exp(edge)
        q_blk = jnp.concatenate([jnp.where(row_chunk == c, qf, 0.0).astype(BF16) for c in range(n_chunks)], axis=1)
        st = st_ref[si]
        entering = [None] * n_chunks
        for c in (range(n_chunks) if forward else range(n_chunks - 1, -1, -1)):
            entering[c] = st.astype(BF16)
            st = st * dec[c] + kv[:, c * LANES:(c + 1) * LANES]
        st_ref[si] = st
        inter = _dot_nt(q_blk, jnp.concatenate(entering, axis=1))
        yield
        acc_ref[d, rows, ls] = intra + inter

    def body(it, carry):
        tiles = (it, n_tiles - 1 - it)
        z = [_dot(dec_ref[0, pl.ds(pl.multiple_of(tiles[d] * tile, tile), tile), :], (wf_ref, wb_ref)[d][...])
             for d in (0, 1)]
        _run_skewed([tile_step(tiles[d], d, hd, z[d]) for hd in range(n_heads) for d in (0, 1)])
        return carry

    lax.fori_loop(0, n_tiles, body, 0)
    for hd in range(n_heads):
        ls = slice(hd * LANES, (hd + 1) * LANES)
        y = _rms(acc_ref[0, :, ls] + acc_ref[1, :, ls], gco_ref[...])
        gate = g_ref[0, :, ls].astype(F32)
        o_ref[0, :, ls] = (y * (gate * _sigmoid(gate))).astype(o_ref.dtype)


def _gla(cqkvg, dec, wf, wb, bf, bb, gco, bsz, t_len, tile, hp=4):
    c3 = cqkvg.reshape(bsz, t_len, 4 * C_HEADS * LANES)
    d3 = dec.reshape(bsz, t_len, LANES)
    groups = C_HEADS // hp
    wl = hp * LANES

    def cspec(j):
        return pl.BlockSpec((1, t_len, wl), lambda b, h: (b, 0, j * groups + h))

    out = pl.pallas_call(
        functools.partial(_gla_kernel, tile=tile),
        grid=(bsz, groups),
        in_specs=[cspec(0), cspec(1), cspec(2), cspec(3),
                  pl.BlockSpec((1, t_len, LANES), lambda b, h: (b, 0, 0)),
                  pl.BlockSpec((LANES, wl), lambda b, h: (0, h)),
                  pl.BlockSpec((LANES, wl), lambda b, h: (0, h)),
                  pl.BlockSpec((1, wl), lambda b, h: (0, h)),
                  pl.BlockSpec((1, wl), lambda b, h: (0, h)),
                  pl.BlockSpec((1, LANES), lambda b, h: (0, 0))],
        out_specs=pl.BlockSpec((1, t_len, wl), lambda b, h: (b, 0, h)),
        out_shape=jax.ShapeDtypeStruct((bsz, t_len, C_HEADS * LANES), BF16),
        scratch_shapes=[pltpu.VMEM((2, t_len, wl), F32), pltpu.VMEM((2 * hp, C_DV, C_DK), F32)],
        compiler_params=_cparams(("parallel", "parallel")),
        name="gla",
    )(c3, c3, c3, c3, d3, wf, wb, bf, bb, gco)
    return out.reshape(bsz * t_len, C_HEADS * LANES)


def _merge_stages(x, rows, gmix_ref, wgate_ref, bg_ref, y_refs, wp_refs, wo_ref, out):
    h = _rms(x, gmix_ref[...]).astype(BF16)
    dots = []
    for j in range(N_BRANCH):
        sl = slice(j * D_MODEL, (j + 1) * D_MODEL)
        dots.append((_dot(h, wgate_ref[:, sl]), _dot(y_refs[j][rows, :], wp_refs[j][...])))
    yield
    m = None
    for j, (gate, proj) in enumerate(dots):
        term = _sigmoid(gate + bg_ref[:, j * D_MODEL:(j + 1) * D_MODEL]) * proj
        m = term if m is None else m + term
    y = _dot(m.astype(BF16), wo_ref[...])
    yield
    out.append(x + y)


def _merge_tile(x, gmix_ref, wgate_ref, bg_ref, y_refs, wp_refs, wo_ref):
    out = []
    for _ in _merge_stages(x, slice(None), gmix_ref, wgate_ref, bg_ref, y_refs, wp_refs, wo_ref, out):
        pass
    return out[0]


def _merge_specs(tm):
    tok = lambda w: pl.BlockSpec((tm, w), lambda i: (i, 0))
    return [tok(512), tok(512), tok(512),
            _const_spec((1, D_MODEL)),
            _const_spec((D_MODEL, N_BRANCH * D_MODEL)),
            _const_spec((1, N_BRANCH * D_MODEL)),
            _const_spec((512, D_MODEL)), _const_spec((512, D_MODEL)), _const_spec((512, D_MODEL)),
            _const_spec((D_MODEL, D_MODEL))]


def _merge_ffn_kernel(x_ref, ya_ref, yb_ref, yc_ref, gmix_ref, wgate_ref, bg_ref, wpa_ref, wpb_ref, wpc_ref, wo_ref,
                      g_ref, wg_ref, wu_ref, wd_ref, o_ref, *, chunk):
    x = _merge_tile(x_ref[...], gmix_ref, wgate_ref, bg_ref, (ya_ref, yb_ref, yc_ref),
                    (wpa_ref, wpb_ref, wpc_ref), wo_ref)
    h = _rms(x, g_ref[...]).astype(BF16)
    o_ref[...] = x
    for c0 in range(0, D_FF, chunk):
        gt = _dot(h, wg_ref[:, c0:c0 + chunk])
        up = _dot(h, wu_ref[:, c0:c0 + chunk])
        a = (gt * _sigmoid(gt) * up).astype(BF16)
        o_ref[...] += _dot(a, wd_ref[c0:c0 + chunk, :])


def _merge_ffn(x2, merge_args, g, wg, wu, wd, tm, chunk=256):
    n = x2.shape[0]
    return pl.pallas_call(
        functools.partial(_merge_ffn_kernel, chunk=chunk),
        grid=(n // tm,),
        in_specs=[pl.BlockSpec((tm, D_MODEL), lambda i: (i, 0))] + _merge_specs(tm)
                 + [_const_spec((1, D_MODEL)),
                    _const_spec((D_MODEL, D_FF)), _const_spec((D_MODEL, D_FF)), _const_spec((D_FF, D_MODEL))],
        out_specs=pl.BlockSpec((tm, D_MODEL), lambda i: (i, 0)),
        out_shape=jax.ShapeDtypeStruct((n, D_MODEL), F32),
        compiler_params=_cparams(("parallel",)),
        name="merge_ffn",
    )(x2, *merge_args, g, wg, wu, wd)


def _merge_router_kernel(x_ref, ya_ref, yb_ref, yc_ref, gmix_ref, wgate_ref, bg_ref, wpa_ref, wpb_ref, wpc_ref, wo_ref,
                         g_ref, wr_ref, xo_ref, h_ref, gate_ref, rank_ref, rank_t_ref, cnt_ref, tot_ref, carry_ref,
                         *, tb):
    tm = x_ref.shape[0]

    @pl.when(pl.program_id(0) == 0)
    def _():
        carry_ref[...] = jnp.zeros_like(carry_ref)

    lane = lax.broadcasted_iota(jnp.int32, (tb, LANES), 1).astype(F32)
    row = lax.broadcasted_iota(jnp.int32, (tb, tb), 0)
    col = lax.broadcasted_iota(jnp.int32, (tb, tb), 1)
    earlier = jnp.where(col < row, 1.0, 0.0).astype(BF16)
    neg = jnp.float32(-jnp.inf)
    w_hi, w_lo = _split_bf16(wr_ref[...])
    w_cat = jnp.concatenate([w_hi, w_lo], axis=1)
    local = []

    def block(u):
        rows = slice(u * tb, (u + 1) * tb)
        merged = []
        yield from _merge_stages(x_ref[rows, :], rows, gmix_ref, wgate_ref, bg_ref, (ya_ref, yb_ref, yc_ref),
                                 (wpa_ref, wpb_ref, wpc_ref), wo_ref, merged)
        x = merged[0]
        xo_ref[rows, :] = x
        hf = _rms(x, g_ref[...])
        h_ref[rows, :] = hf.astype(h_ref.dtype)
        hf_hi, hf_lo = _split_bf16(hf)
        wide = _dot(hf_hi, w_cat)
        logits = wide[:, :LANES] + (wide[:, LANES:] + _dot(hf_lo, w_hi))
        yield
        logits = jnp.where(lane < N_EXPERTS, logits, neg)
        v1 = jnp.max(logits, axis=-1, keepdims=True)
        i1 = jnp.min(jnp.where(logits == v1, lane, float(LANES)), axis=-1, keepdims=True)
        rest = jnp.where(lane == i1, neg, logits)
        v2 = jnp.max(rest, axis=-1, keepdims=True)
        i2 = jnp.min(jnp.where(rest == v2, lane, float(LANES)), axis=-1, keepdims=True)
        e2 = jnp.exp(v2 - v1)
        w1 = 1.0 / (1.0 + e2)
        w2 = e2 / (1.0 + e2)
        gate_ref[rows, :] = jnp.where(lane == i1, w1, 0.0) + jnp.where(lane == i2, w2, 0.0)
        chosen = (lane == i1) | (lane == i2)
        assign = jnp.where(chosen, 1.0, 0.0)
        inside = _dot(earlier, assign.astype(BF16))
        yield
        local.append((chosen, inside, jnp.sum(assign, axis=0, keepdims=True)))

    _run_skewed([block(u) for u in range(tm // tb)])
    carry = carry_ref[...]
    for u, (chosen, inside, count) in enumerate(local):
        rows = slice(u * tb, (u + 1) * tb)
        rank = jnp.where(chosen, inside + carry[0:1, :], -1.0)
        rank_ref[rows, :] = rank
        rank_t_ref[:, rows] = rank.T[:N_EXPERTS, :]
        cnt_ref[u] = carry
        carry = carry + count
    carry_ref[...] = carry
    tot_ref[...] = carry


def _merge_router(x2, merge_args, g, wr_pad, tm, tb):
    n = x2.shape[0]
    nt = n // tm
    return pl.pallas_call(
        functools.partial(_merge_router_kernel, tb=tb),
        grid=(nt,),
        in_specs=[pl.BlockSpec((tm, D_MODEL), lambda i: (i, 0))] + _merge_specs(tm)
                 + [_const_spec((1, D_MODEL)),
                    _const_spec((D_MODEL, LANES))],
        out_specs=[pl.BlockSpec((tm, D_MODEL), lambda i: (i, 0)),
                   pl.BlockSpec((tm, D_MODEL), lambda i: (i, 0)),
                   pl.BlockSpec((tm, LANES), lambda i: (i, 0)),
                   pl.BlockSpec((tm, LANES), lambda i: (i, 0)),
                   pl.BlockSpec((N_EXPERTS, tm), lambda i: (0, i)),
                   pl.BlockSpec((tm // tb, 8, LANES), lambda i: (i, 0, 0)),
                   pl.BlockSpec((8, LANES), lambda i: (0, 0))],
        out_shape=[jax.ShapeDtypeStruct((n, D_MODEL), F32),
                   jax.ShapeDtypeStruct((n, D_MODEL), BF16),
                   jax.ShapeDtypeStruct((n, LANES), F32),
                   jax.ShapeDtypeStruct((n, LANES), F32),
                   jax.ShapeDtypeStruct((N_EXPERTS, n), F32),
                   jax.ShapeDtypeStruct((n // tb, 8, LANES), F32),
                   jax.ShapeDtypeStruct((8, LANES), F32)],
        scratch_shapes=[pltpu.VMEM((8, LANES), F32)],
        compiler_params=_cparams(("arbitrary",)),
        name="merge_router",
    )(x2, *merge_args, g, wr_pad)


def _moe_plan(cnt, tot, n, tb, rs, tg):
    i32 = jnp.int32
    counts = tot[0, :N_EXPERTS].astype(i32)
    cum = jnp.concatenate([cnt[:, 0, :N_EXPERTS], tot[:1, :N_EXPERTS]], axis=0).astype(i32)
    n_sb = (counts + rs - 1) // rs
    sb_end = jnp.cumsum(n_sb)
    sb_start = sb_end - n_sb
    goff = sb_start * rs
    n_valid_sb = sb_end[-1]
    s_max = 2 * n // rs + N_EXPERTS
    s_ids = jnp.arange(s_max, dtype=i32)
    last_sb = n_valid_sb - 1
    experts = jnp.arange(N_EXPERTS, dtype=i32)

    def owner(sb):
        e = jnp.minimum(jnp.sum(sb[:, None] >= sb_end[None, :], axis=1), N_EXPERTS - 1).astype(i32)
        return e, e[:, None] == experts[None, :]

    pick = lambda onehot, table: jnp.sum(jnp.where(onehot, table[None, :], 0), axis=1)
    s_eff = jnp.minimum(s_ids, last_sb)
    sb_expert, sb_hot = owner(s_eff)
    sb_rows = jnp.clip(pick(sb_hot, counts) - (s_eff - pick(sb_hot, sb_start)) * rs, 0, rs)
    sb_tiles = ((sb_rows + tb - 1) // tb).astype(i32)

    t_ids = jnp.arange(s_max * rs // tb, dtype=i32)
    t_sb = t_ids * tb // rs
    t_exp, t_hot = owner(jnp.minimum(t_sb, last_sb))
    t_count = pick(t_hot, counts)
    t_r0 = t_ids * tb - pick(t_hot, goff)
    t_valid = (t_sb <= last_sb) & (t_r0 < t_count)
    t_r1 = jnp.minimum(t_r0 + tb, t_count)
    cum_e = jnp.sum(jnp.where(t_hot[None, :, :], cum[:, None, :], 0), axis=2)
    t_lo = jnp.sum(cum_e[1:] <= t_r0[None, :], axis=0)
    t_hi = jnp.where(t_valid, jnp.sum(cum_e[:-1] < t_r1[None, :], axis=0), t_lo)
    gather_tabs = tuple(a.astype(i32) for a in (t_exp, t_r0, t_lo, t_hi))

    start = goff[None, :] + cum[:-1]
    num = cum[1:] - cum[:-1]
    n_rows = s_max * rs
    off0 = jnp.minimum(start // ROW_ALIGN * ROW_ALIGN, n_rows - tb)
    need1 = (num > 0) & (start + num > off0 + tb)
    off1 = jnp.where(need1, jnp.minimum(off0 + tb, n_rows - tb), 0)
    offs = jnp.stack([off0, off1], axis=-1).reshape(-1).astype(i32)
    expert_tabs = (sb_expert, sb_tiles, last_sb.reshape(1).astype(i32))
    return gather_tabs, expert_tabs, (offs, need1.reshape(-1).astype(i32), goff.astype(i32))


def _gather_kernel(exp_ref, r0_ref, lo_ref, hi_ref, rank_ref, h_ref, o_ref, acc_ref, *, tb):
    step = pl.program_id(0)
    n_sub = o_ref.shape[0] // tb
    n_blk = rank_ref.shape[1]
    row = lax.broadcasted_iota(jnp.int32, (tb, tb), 0)
    tiles = [step * n_sub + u for u in range(n_sub)]
    n_trips = functools.reduce(jnp.maximum, [(hi_ref[t] - lo_ref[t] + 1) >> 1 for t in tiles])

    @pl.when(n_trips == 0)
    def _():
        o_ref[...] = jnp.zeros_like(o_ref)

    @pl.when(n_trips > 0)
    def _():
        acc_ref[...] = jnp.zeros_like(acc_ref)

        def body(p, carry):
            parts = []
            for t in tiles:
                e, lo, hi = exp_ref[t], lo_ref[t], hi_ref[t]
                want = row + r0_ref[t]
                for j in range(2):
                    kb = lo + 2 * p + j
                    kbc = jnp.minimum(kb, n_blk - 1)
                    rk = rank_ref[e, pl.ds(kbc, 1), :].astype(jnp.int32)
                    onehot = jnp.where(jnp.where(kb < hi, rk, -1) == want, 1.0, 0.0).astype(BF16)
                    parts.append(_dot(onehot, h_ref[pl.ds(pl.multiple_of(kbc * tb, tb), tb), :]))
            for u in range(n_sub):
                acc_ref[u * tb:(u + 1) * tb, :] += parts[2 * u] + parts[2 * u + 1]
            return carry

        lax.fori_loop(0, n_trips, body, 0)
        o_ref[...] = acc_ref[...].astype(o_ref.dtype)


def _gather(tabs, rank_t3, h, n_rows, tg, tb):
    n = h.shape[0]
    return pl.pallas_call(
        functools.partial(_gather_kernel, tb=tb),
        grid_spec=pltpu.PrefetchScalarGridSpec(
            num_scalar_prefetch=4,
            grid=(n_rows // tg,),
            in_specs=[_const_spec((N_EXPERTS, n // tb, tb)),
                      _const_spec((n, D_MODEL))],
            out_specs=pl.BlockSpec((tg, D_MODEL), lambda t, e, r, lo, hi: (t, 0)),
            scratch_shapes=[pltpu.VMEM((tg, D_MODEL), F32)]),
        out_shape=jax.ShapeDtypeStruct((n_rows, D_MODEL), BF16),
        compiler_params=_cparams(("arbitrary",)),
        name="moe_gather",
    )(*tabs, rank_t3, h)


def _experts_kernel(exp_ref, tiles_ref, last_ref, xs_ref, wg_ref, wu_ref, wd_ref, o_ref,
                    acc_ref, wgb_ref, wub_ref, wdb_ref, *, tb):
    s, c = pl.program_id(0), pl.program_id(1)
    nc = pl.num_programs(1)

    @pl.when((s > last_ref[0]) & (c == 0))
    def _():
        o_ref[...] = jnp.zeros_like(o_ref)

    @pl.when(s <= last_ref[0])
    def _():
        @pl.when(c == 0)
        def _():
            acc_ref[...] = jnp.zeros_like(acc_ref)

        wgb_ref[...] = wg_ref[0].astype(BF16)
        wub_ref[...] = wu_ref[0].astype(BF16)
        wdb_ref[...] = wd_ref[0].astype(BF16)

        def tile_step(t):
            rows = pl.ds(pl.multiple_of(t * tb, tb), tb)
            x = xs_ref[rows, :]
            gt = _dot(x, wgb_ref[...])
            up = _dot(x, wub_ref[...])
            yield
            a = (gt * _sigmoid(gt) * up).astype(BF16)
            y = _dot(a, wdb_ref[...])
            yield
            acc_ref[rows, :] += y

        def body(t4, carry):
            _run_skewed([tile_step(4 * t4 + u) for u in range(4)])
            return carry

        n_tiles = tiles_ref[s]
        lax.fori_loop(0, n_tiles >> 2, body, 0)
        done = n_tiles & ~3

        @pl.when((n_tiles & 2) == 2)
        def _():
            _run_skewed([tile_step(done), tile_step(done + 1)])

        @pl.when((n_tiles & 1) == 1)
        def _():
            _run_skewed([tile_step(n_tiles - 1)])

        @pl.when(c == nc - 1)
        def _():
            o_ref[...] = acc_ref[...].astype(o_ref.dtype)


def _experts(tabs, xs, wg, wu, wd, rs, tb, fc):
    n_rows = xs.shape[0]
    nc = D_FF_EXPERT // fc

    def sb(s, last):
        return jnp.minimum(s, last[0])

    def chunk(s, c, last):
        return jnp.where(s <= last[0], c, nc - 1)

    return pl.pallas_call(
        functools.partial(_experts_kernel, tb=tb),
        grid_spec=pltpu.PrefetchScalarGridSpec(
            num_scalar_prefetch=3,
            grid=(n_rows // rs, nc),
            in_specs=[pl.BlockSpec((rs, D_MODEL), lambda s, c, ex, tl, last: (sb(s, last), 0)),
                      pl.BlockSpec((1, D_MODEL, fc), lambda s, c, ex, tl, last: (ex[sb(s, last)], 0, chunk(s, c, last))),
                      pl.BlockSpec((1, D_MODEL, fc), lambda s, c, ex, tl, last: (ex[sb(s, last)], 0, chunk(s, c, last))),
                      pl.BlockSpec((1, fc, D_MODEL), lambda s, c, ex, tl, last: (ex[sb(s, last)], chunk(s, c, last), 0))],
            out_specs=pl.BlockSpec((rs, D_MODEL), lambda s, c, ex, tl, last: (s, 0)),
            scratch_shapes=[pltpu.VMEM((rs, D_MODEL), F32),
                            pltpu.VMEM((D_MODEL, fc), BF16), pltpu.VMEM((D_MODEL, fc), BF16),
                            pltpu.VMEM((fc, D_MODEL), BF16)]),
        out_shape=jax.ShapeDtypeStruct((n_rows, D_MODEL), BF16),
        compiler_params=_cparams(("arbitrary", "arbitrary")),
        name="moe_experts",
    )(*tabs, xs, wg, wu, wd)


def _combine_kernel(off_ref, need_ref, goff_ref, x_ref, gate_ref, rank_ref, *rest, final_norm):
    y_refs, (gfin_ref, o_ref) = rest[:2 * N_EXPERTS], rest[2 * N_EXPERTS:]
    i = pl.program_id(0)
    tt = x_ref.shape[0]
    tb = y_refs[0].shape[0]
    lane = lax.broadcasted_iota(jnp.int32, (tt, LANES), 1)
    col = lax.broadcasted_iota(jnp.int32, (tt, tb), 1)
    gate, rank = gate_ref[...], rank_ref[...]

    def expert_terms(e):
        ge = jnp.sum(jnp.where(lane == e, gate, 0.0), axis=-1, keepdims=True)
        rk = jnp.sum(jnp.where(lane == e, rank, 0.0), axis=-1, keepdims=True).astype(jnp.int32)
        return ge, jnp.where(rk >= 0, rk + goff_ref[e], -1)

    def window(e, j, pos):
        onehot = jnp.where(pos - off_ref[(i * N_EXPERTS + e) * 2 + j] == col, 1.0, 0.0).astype(BF16)
        return _dot(onehot, y_refs[2 * e + j][...])

    y = x_ref[...]
    for e in range(N_EXPERTS):
        ge, pos = expert_terms(e)
        y = y + ge * window(e, 0, pos)
    o_ref[...] = y
    for e in range(N_EXPERTS):
        @pl.when(need_ref[i * N_EXPERTS + e] == 1)
        def _(e=e):
            ge, pos = expert_terms(e)
            o_ref[...] += ge * window(e, 1, pos)
    if final_norm:
        o_ref[...] = _rms(o_ref[...], gfin_ref[...])


def _combine(tabs, x2, gate, rank, ys, gfin, tt, tb, final_norm):
    n = x2.shape[0]

    def yspec(slot):
        return pl.BlockSpec((pl.Element(tb), pl.Element(D_MODEL)),
                            lambda i, off, need, goff: (pl.multiple_of(off[i * 2 * N_EXPERTS + slot], ROW_ALIGN), 0))

    tok = lambda w: pl.BlockSpec((tt, w), lambda i, off, need, goff: (i, 0))
    return pl.pallas_call(
        functools.partial(_combine_kernel, final_norm=final_norm),
        grid_spec=pltpu.PrefetchScalarGridSpec(
            num_scalar_prefetch=3,
            grid=(n // tt,),
            in_specs=[tok(D_MODEL), tok(LANES), tok(LANES)] + [yspec(k) for k in range(2 * N_EXPERTS)]
                     + [pl.BlockSpec((1, D_MODEL), lambda i, off, need, goff: (0, 0))],
            out_specs=tok(D_MODEL)),
        out_shape=jax.ShapeDtypeStruct((n, D_MODEL), F32),
        compiler_params=_cparams(("arbitrary",)),
        name="moe_combine",
    )(*tabs, x2, gate, rank, *([ys] * (2 * N_EXPERTS)), gfin)


def _merge_moe(x2, merge_args, g_ffn, w_router, wg, wu, wd, gfin, final_norm):
    n = x2.shape[0]
    tb = min(256, n // 2)
    rs = max(9 * n // (8 * N_EXPERTS) // tb * tb, 2 * tb)
    tg = next(k * tb for k in (3, 2, 1) if rs % (k * tb) == 0)
    wr = jnp.pad(w_router, ((0, 0), (0, LANES - N_EXPERTS)))
    x2, h, gate, rank, rank_t, cnt, tot = _merge_router(x2, merge_args, g_ffn, wr, min(2 * tb, n), tb)
    gather_tabs, expert_tabs, combine_tabs = _moe_plan(cnt, tot, n, tb, rs, tg)
    n_rows = (2 * n // rs + N_EXPERTS) * rs
    xs = _gather(gather_tabs, rank_t.reshape(N_EXPERTS, n // tb, tb), h, n_rows, tg, tb)
    ys = _experts(expert_tabs, xs, wg, wu, wd, rs, tb, 512)
    return _combine(combine_tabs, x2, gate, rank, ys, gfin, tb, tb, final_norm)


def _final_norm_kernel(x_ref, g_ref, o_ref):
    o_ref[...] = _rms(x_ref[...], g_ref[...])


def _final_norm(x2, g, tm):
    n = x2.shape[0]
    return pl.pallas_call(
        _final_norm_kernel,
        grid=(n // tm,),
        in_specs=[pl.BlockSpec((tm, D_MODEL), lambda i: (i, 0)), _const_spec((1, D_MODEL))],
        out_specs=pl.BlockSpec((tm, D_MODEL), lambda i: (i, 0)),
        out_shape=jax.ShapeDtypeStruct((n, D_MODEL), F32),
        compiler_params=_cparams(("parallel",)),
        name="final_norm",
    )(x2, g)


def _rope_tables(t_len):
    rows = t_len // GRID_W
    row = np.repeat(np.arange(rows, dtype=np.float64), GRID_W)
    col = np.tile(np.arange(GRID_W, dtype=np.float64), rows)
    lane = np.arange(LANES)

    def table(rot_dim, lane_in_slice, active):
        n_freq = rot_dim // 4
        inv_freq = ROPE_THETA ** (-np.arange(n_freq, dtype=np.float64) / n_freq)
        freq = inv_freq[lane_in_slice % n_freq]
        use_row = (lane_in_slice % rot_dim) < (rot_dim // 2)
        ang = np.where(use_row[None, :], row[:, None], col[:, None]) * freq[None, :]
        sign = np.where((lane_in_slice % (rot_dim // 2)) < n_freq, -1.0, 1.0)
        cos = np.where(active[None, :], np.cos(ang), 1.0)
        sin = np.where(active[None, :], np.sin(ang) * sign[None, :], 0.0)
        return jnp.asarray(cos, F32), jnp.asarray(sin, F32)

    cos_a, sin_a = table(A_HEAD_DIM, lane % A_HEAD_DIM, np.ones(LANES, bool))
    in_rope = (lane >= B_NOPE_DIM) & (lane < B_NOPE_DIM + B_ROPE_DIM)
    cos_b, sin_b = table(B_ROPE_DIM, (lane - B_NOPE_DIM) % B_ROPE_DIM, in_rope)
    return cos_a, sin_a, cos_b, sin_b


def _pack_w_in(w):
    o_c, o_gate = 1312, 3392
    zeros = lambda k: jnp.zeros((D_MODEL, k), BF16)
    cols = [w[:, :o_c].astype(BF16), zeros(LANES - B_ROPE_DIM), w[:, o_c:o_gate].astype(BF16),
            zeros(LANES - 2 * C_GATE_RANK)]
    return jnp.concatenate(cols, axis=1), w[:, o_gate:].astype(BF16)


def _pack_b_weights(w_q_up, w_kv_up):
    hq = w_q_up.reshape(B_Q_RANK, B_HEADS, B_NOPE_DIM + B_ROPE_DIM)
    wq = jnp.pad(hq, ((0, 0), (0, 0), (0, LANES - B_NOPE_DIM - B_ROPE_DIM))).reshape(B_Q_RANK, B_HEADS * LANES)
    hkv = w_kv_up.reshape(B_KV_RANK, B_HEADS, B_NOPE_DIM + B_V_DIM)
    wk = jnp.pad(hkv[:, :, :B_NOPE_DIM], ((0, 0), (0, 0), (0, LANES - B_NOPE_DIM))).reshape(B_KV_RANK, B_HEADS * LANES)
    wv = hkv[:, :, B_NOPE_DIM:].reshape(B_KV_RANK, B_HEADS * B_V_DIM)
    return wq.astype(BF16), wk.astype(BF16), wv.astype(BF16)


def _rope_placement():
    e = np.zeros((LANES, LANES), np.float32)
    e[np.arange(B_ROPE_DIM), B_NOPE_DIM + np.arange(B_ROPE_DIM)] = 1.0
    return jnp.asarray(e, BF16)


def kernel(x, w_in, b_gate, g_mix, g_a_q, g_a_k, g_b_q, w_b_q_up, g_b_kv, w_b_kv_up, w_c_af_up, b_c_af, w_c_ab_up, b_c_ab, g_c_out, w_pa, w_pb, w_pc, w_out, g_ffn, w_ff_gate, w_ff_up, w_ff_down, w_router, w_e_gate, w_e_up, w_e_down, g_final):
    bsz, t_len, _ = x.shape
    n = bsz * t_len
    depth = w_in.shape[0]
    tm = min(512, t_len)
    tq_a = min(512, t_len)
    tq_b = min(512, t_len)
    gla_tile = min(256, t_len)

    cos_a, sin_a, cos_b, sin_b = _rope_tables(t_len)
    e_mat = _rope_placement()
    row = lambda v: v.reshape(1, -1).astype(F32)
    x2 = x.reshape(n, D_MODEL)

    for i in range(depth):
        wq, wk, wv = _pack_b_weights(w_b_q_up[i], w_b_kv_up[i])
        w_packed, w_gate = _pack_w_in(w_in[i])
        qa, ka, av2, qb, kb, vb, cqkvg, dec = _fused_in(
            x2, row(g_mix[i]), w_packed, row(jnp.tile(g_a_q[i], 2)), row(jnp.tile(g_a_k[i], 2)),
            cos_a, sin_a, row(g_b_q[i]), row(g_b_kv[i]), wq, wk, wv, e_mat, cos_b, sin_b, tm, t_len)
        ya = _attn_a(qa, ka, av2, bsz, t_len, tq_a)
        yb = _attn_b(qb, kb, vb, bsz, t_len, tq_b)

        wf = jnp.pad(w_c_af_up[i], ((0, LANES - C_GATE_RANK), (0, 0))).astype(BF16)
        wb = jnp.pad(w_c_ab_up[i], ((C_GATE_RANK, LANES - 2 * C_GATE_RANK), (0, 0))).astype(BF16)
        yc = _gla(cqkvg, dec, wf, wb, row(b_c_af[i]), row(b_c_ab[i]), row(g_c_out[i]), bsz, t_len, gla_tile)

        merge_args = (ya, yb, yc, row(g_mix[i]), w_gate, row(b_gate[i]), w_pa[i].astype(BF16), w_pb[i].astype(BF16),
                      w_pc[i].astype(BF16), w_out[i].astype(BF16))
        j = i // 2
        last = i == depth - 1
        if i % 2 == 0:
            x2 = _merge_ffn(x2, merge_args, row(g_ffn[i]), w_ff_gate[j].astype(BF16), w_ff_up[j].astype(BF16),
                            w_ff_down[j].astype(BF16), tm)
            if last:
                x2 = _final_norm(x2, row(g_final), tm)
        else:
            x2 = _merge_moe(x2, merge_args, row(g_ffn[i]), w_router[j], w_e_gate[j], w_e_up[j], w_e_down[j],
                            row(g_final), last)
    return x2.reshape(bsz, t_len, D_MODEL)
```

```python
import functools

import jax
import jax.numpy as jnp
import numpy as np
from jax import lax
from jax.experimental import pallas as pl
from jax.experimental.pallas import tpu as pltpu

F32 = jnp.float32
BF16 = jnp.bfloat16

D_MODEL = 1024
GRID_W = 64
ROPE_THETA = 10000.0
EPS = 1e-6
A_HEADS, A_KV_HEADS, A_HEAD_DIM = 8, 2, 64
B_HEADS, B_NOPE_DIM, B_ROPE_DIM, B_V_DIM = 8, 64, 32, 64
B_Q_RANK = B_KV_RANK = 256
C_HEADS, C_DK, C_DV, C_GATE_RANK, C_TAU, C_CHUNK = 4, 128, 128, 16, 16.0, 64
N_BRANCH = 3
D_FF = 2816
N_EXPERTS, TOP_K, D_FF_EXPERT = 8, 2, 3584

LANES = 128
ROW_ALIGN = 16
VMEM_LIMIT = 56 * 1024 * 1024


def _cparams(sem):
    return pltpu.CompilerParams(dimension_semantics=sem, vmem_limit_bytes=VMEM_LIMIT)


def _const_spec(shape):
    nd = len(shape)
    return pl.BlockSpec(shape, lambda *_: (0,) * nd, pipeline_mode=pl.Buffered(1))


def _rms(xf, g):
    return xf * lax.rsqrt(jnp.mean(xf * xf, axis=-1, keepdims=True) + EPS) * g


def _sigmoid(x):
    return 1.0 / (1.0 + jnp.exp(-x))


def _split_bf16(x):
    hi = x.astype(BF16)
    lo = (x - hi.astype(F32)).astype(BF16)
    return hi, lo


def _dot(a, b):
    return jnp.dot(a, b, preferred_element_type=F32)


def _dot_nt(a, b):
    return lax.dot_general(a, b, (((1,), (1,)), ((), ())), preferred_element_type=F32)


def _dot_tn(a, b):
    return lax.dot_general(a, b, (((0,), (0,)), ((), ())), preferred_element_type=F32)


def _rope_group(x, cos, sin_signed, lane, half):
    fwd = pltpu.roll(x, LANES - half, axis=1)
    bwd = pltpu.roll(x, half, axis=1)
    swapped = jnp.where((lane % (2 * half)) < half, fwd, bwd)
    return x * cos + swapped * sin_signed


def _seg64_meansq(x, lane):
    sq = x * x
    lo = jnp.sum(jnp.where(lane < 64, sq, 0.0), axis=-1, keepdims=True)
    hi = jnp.sum(jnp.where(lane < 64, 0.0, sq), axis=-1, keepdims=True)
    return jnp.where(lane < 64, lo, hi) * (1.0 / 64.0)


F_AQ, F_AKV, F_B, F_C, F_DEC, F_WIDTH = 0, 512, 768, 1408, 3456, 3584


def _dup_half(x, lane, low):
    other = pltpu.roll(x, 64, axis=1)
    return jnp.where(lane < 64, x, other) if low else jnp.where(lane < 64, other, x)


def _fused_in_kernel(x_ref, g_ref, w_ref, gaq_ref, gak_ref, cosa_ref, sina_ref,
                     gbq_ref, gbkv_ref, wq_ref, wk_ref, wv_ref, e_ref, cosb_ref, sinb_ref,
                     qa_ref, ka_ref, va_ref, qb_ref, kb_ref, vb_ref, c_ref, dec_ref):
    rows = x_ref.shape[0]
    lane = lax.broadcasted_iota(jnp.int32, (rows, LANES), 1)
    h = _rms(x_ref[...], g_ref[...]).astype(BF16)

    def proj(off, width):
        return _dot(h, w_ref[:, off:off + width])

    def chain_aq():
        y = proj(F_AQ, 512)
        yield
        cos, sin, gq = cosa_ref[...], sina_ref[...], gaq_ref[...]
        for gi in range(4):
            sl = slice(gi * LANES, (gi + 1) * LANES)
            x = y[:, sl]
            x = x * lax.rsqrt(_seg64_meansq(x, lane) + EPS) * gq
            qa_ref[:, sl] = (_rope_group(x, cos, sin, lane, 16) * (A_HEAD_DIM ** -0.5)).astype(qa_ref.dtype)

    def chain_akv():
        y = proj(F_AKV, 256)
        yield
        k, v = y[:, :LANES], y[:, LANES:]
        k = k * lax.rsqrt(_seg64_meansq(k, lane) + EPS) * gak_ref[...]
        k = _rope_group(k, cosa_ref[...], sina_ref[...], lane, 16)
        for kv in range(A_KV_HEADS):
            sl = slice(kv * LANES, (kv + 1) * LANES)
            ka_ref[:, sl] = _dup_half(k, lane, kv == 0).astype(ka_ref.dtype)
            va_ref[:, sl] = _dup_half(v, lane, kv == 0).astype(va_ref.dtype)

    def chain_b():
        y = proj(F_B, 640)
        yield
        cq = _rms(y[:, :B_Q_RANK], gbq_ref[...]).astype(BF16)
        ckv = _rms(y[:, B_Q_RANK:2 * B_Q_RANK], gbkv_ref[...]).astype(BF16)
        q = _dot(cq, wq_ref[...])
        k = _dot(ckv, wk_ref[...])
        v = _dot(ckv, wv_ref[...])
        kr = _dot(y[:, 2 * B_Q_RANK:].astype(BF16), e_ref[...])
        yield
        cos, sin = cosb_ref[...], sinb_ref[...]
        kr = _rope_group(kr, cos, sin, lane, 8)
        scale = (B_NOPE_DIM + B_ROPE_DIM) ** -0.5
        for hd in range(B_HEADS):
            sl = slice(hd * LANES, (hd + 1) * LANES)
            qb_ref[:, sl] = (_rope_group(q[:, sl], cos, sin, lane, 8) * scale).astype(qb_ref.dtype)
            kb_ref[:, sl] = (k[:, sl] + kr).astype(kb_ref.dtype)
        vb_ref[...] = v.astype(vb_ref.dtype)

    def chain_plain(o_ref, off, c0, cw):
        y = proj(off + c0, cw)
        yield
        o_ref[:, c0:c0 + cw] = y.astype(o_ref.dtype)

    chains = [chain_aq(), chain_akv(), chain_b()]
    chains += [chain_plain(c_ref, F_C, c0, 512) for c0 in range(0, 2048, 512)]
    chains += [chain_plain(dec_ref, F_DEC, 0, LANES)]
    _run_skewed(chains)


def _fused_in(x2, g, w_packed, gaq, gak, cos_a, sin_a, gbq, gbkv, wq, wk, wv, e_mat, cos_b, sin_b, tm, t_len):
    n = x2.shape[0]
    nt = t_len // tm
    widths = (512, 256, 256, 1024, 1024, 512, 2048, LANES)
    table = pl.BlockSpec((tm, LANES), lambda i: (i % nt, 0))
    return pl.pallas_call(
        _fused_in_kernel,
        grid=(n // tm,),
        in_specs=[pl.BlockSpec((tm, D_MODEL), lambda i: (i, 0)),
                  _const_spec((1, D_MODEL)),
                  _const_spec((D_MODEL, F_WIDTH)),
                  _const_spec((1, LANES)), _const_spec((1, LANES)), table, table,
                  _const_spec((1, B_Q_RANK)), _const_spec((1, B_KV_RANK)),
                  _const_spec((B_Q_RANK, B_HEADS * LANES)),
                  _const_spec((B_KV_RANK, B_HEADS * LANES)),
                  _const_spec((B_KV_RANK, B_HEADS * B_V_DIM)),
                  _const_spec((LANES, LANES)), table, table],
        out_specs=[pl.BlockSpec((tm, w), lambda i: (i, 0)) for w in widths],
        out_shape=[jax.ShapeDtypeStruct((n, w), BF16) for w in widths],
        compiler_params=_cparams(("parallel",)),
        name="fused_in",
    )(x2, g, w_packed, gaq, gak, cos_a, sin_a, gbq, gbkv, wq, wk, wv, e_mat, cos_b, sin_b)


CHAIN_ROWS = 512


def _run_skewed(chains):
    pending, live = list(chains), []
    while pending or live:
        if pending:
            live.append(pending.pop(0))
        for g in reversed(list(live)):
            try:
                next(g)
            except StopIteration:
                live.remove(g)


def _attend(q, k, v3_ref, out):
    s = _dot_nt(q(), k())
    yield
    m = jnp.max(s, axis=-1, keepdims=True)
    p = jnp.exp(s - m).astype(BF16)
    r = _dot(p, v3_ref[...])
    yield
    out.append(r[:, :LANES] / r[:, LANES:])


def _attn_a_kernel(q_ref, k_ref, v_ref, o_ref, v3_ref):
    tq = q_ref.shape[1]
    n_kv = v3_ref.shape[0]

    @pl.when(pl.program_id(2) == 0)
    def _():
        for kv in range(n_kv):
            v3_ref[kv, :, :LANES] = v_ref[0, :, kv * LANES:(kv + 1) * LANES]
            v3_ref[kv, :, LANES:] = jnp.ones((v_ref.shape[1], LANES), BF16)

    rq = min(tq, CHAIN_ROWS)
    lane = lax.broadcasted_iota(jnp.int32, (rq, LANES), 1)
    first = lane < 64
    outs, chains = [], []
    for r0 in range(0, tq, rq):
        for gi in range(2 * n_kv):
            kv = gi // 2
            k = lambda kv=kv: k_ref[0, :, kv * LANES:(kv + 1) * LANES]
            for keep_first in (True, False):
                def q(gi=gi, keep_first=keep_first, r0=r0):
                    qp = q_ref[0, r0:r0 + rq, gi * LANES:(gi + 1) * LANES].astype(F32)
                    return jnp.where(first == keep_first, qp, 0.0).astype(BF16)
                chains.append(_attend(q, k, v3_ref.at[kv], outs))
    _run_skewed(chains)
    for ri, r0 in enumerate(range(0, tq, rq)):
        for gi in range(2 * n_kv):
            o0, o1 = outs[(ri * 2 * n_kv + gi) * 2], outs[(ri * 2 * n_kv + gi) * 2 + 1]
            o_ref[0, r0:r0 + rq, gi * LANES:(gi + 1) * LANES] = jnp.where(first, o0, o1).astype(o_ref.dtype)


def _attn_a(q, k2, v2, bsz, t_len, tq, n_kv=2):
    q3, k3, v3 = (a.reshape(bsz, t_len, a.shape[-1]) for a in (q, k2, v2))
    out = pl.pallas_call(
        _attn_a_kernel,
        grid=(bsz, A_KV_HEADS // n_kv, t_len // tq),
        in_specs=[pl.BlockSpec((1, tq, 256 * n_kv), lambda b, g, i: (b, i, g)),
                  pl.BlockSpec((1, t_len, LANES * n_kv), lambda b, g, i: (b, 0, g)),
                  pl.BlockSpec((1, t_len, LANES * n_kv), lambda b, g, i: (b, 0, g))],
        out_specs=pl.BlockSpec((1, tq, 256 * n_kv), lambda b, g, i: (b, i, g)),
        out_shape=jax.ShapeDtypeStruct((bsz, t_len, 512), BF16),
        scratch_shapes=[pltpu.VMEM((n_kv, t_len, 2 * LANES), BF16)],
        compiler_params=_cparams(("parallel", "parallel", "arbitrary")),
        name="attn_a",
    )(q3, k3, v3)
    return out.reshape(bsz * t_len, 512)


def _attn_b_kernel(q_ref, k_ref, v_ref, o_ref, v3_ref):
    tq = q_ref.shape[1]
    n_pairs = v3_ref.shape[0]

    @pl.when(pl.program_id(2) == 0)
    def _():
        for pi in range(n_pairs):
            v3_ref[pi, :, :LANES] = v_ref[0, :, pi * LANES:(pi + 1) * LANES]
            v3_ref[pi, :, LANES:] = jnp.ones((v_ref.shape[1], LANES), BF16)

    rq = min(tq, CHAIN_ROWS)
    lane = lax.broadcasted_iota(jnp.int32, (rq, LANES), 1)
    outs, chains = [], []
    for r0 in range(0, tq, rq):
        for hd in range(2 * n_pairs):
            sl = slice(hd * LANES, (hd + 1) * LANES)
            q = lambda sl=sl, r0=r0: q_ref[0, r0:r0 + rq, sl]
            k = lambda sl=sl: k_ref[0, :, sl]
            chains.append(_attend(q, k, v3_ref.at[hd // 2], outs))
    _run_skewed(chains)
    for ri, r0 in enumerate(range(0, tq, rq)):
        for pi in range(n_pairs):
            o0, o1 = outs[(ri * n_pairs + pi) * 2], outs[(ri * n_pairs + pi) * 2 + 1]
            o_ref[0, r0:r0 + rq, pi * LANES:(pi + 1) * LANES] = jnp.where(lane < 64, o0, o1).astype(o_ref.dtype)


def _attn_b(q, k, v, bsz, t_len, tq, n_pairs=4):
    q3, k3, v3 = (a.reshape(bsz, t_len, a.shape[-1]) for a in (q, k, v))
    out = pl.pallas_call(
        _attn_b_kernel,
        grid=(bsz, B_HEADS // (2 * n_pairs), t_len // tq),
        in_specs=[pl.BlockSpec((1, tq, 256 * n_pairs), lambda b, p, i: (b, i, p)),
                  pl.BlockSpec((1, t_len, 256 * n_pairs), lambda b, p, i: (b, 0, p)),
                  pl.BlockSpec((1, t_len, LANES * n_pairs), lambda b, p, i: (b, 0, p))],
        out_specs=pl.BlockSpec((1, tq, LANES * n_pairs), lambda b, p, i: (b, i, p)),
        out_shape=jax.ShapeDtypeStruct((bsz, t_len, 512), BF16),
        scratch_shapes=[pltpu.VMEM((n_pairs, t_len, 2 * LANES), BF16)],
        compiler_params=_cparams(("parallel", "parallel", "arbitrary")),
        name="attn_b",
    )(q3, k3, v3)
    return out.reshape(bsz * t_len, 512)


def _log_sigmoid(z):
    return jnp.minimum(z, 0.0) - jnp.log(1.0 + jnp.exp(-jnp.abs(z)))


def _gla_kernel(q_ref, k_ref, v_ref, g_ref, dec_ref, wf_ref, wb_ref, bf_ref, bb_ref, gco_ref,
                o_ref, acc_ref, st_ref, *, tile):
    t_len = q_ref.shape[1]
    n_tiles = t_len // tile
    n_chunks = tile // C_CHUNK
    row = lax.broadcasted_iota(jnp.int32, (tile, tile), 0)
    col = lax.broadcasted_iota(jnp.int32, (tile, tile), 1)
    same = (row // C_CHUNK) == (col // C_CHUNK)
    row_chunk = lax.broadcasted_iota(jnp.int32, (tile, LANES), 0) // C_CHUNK
    keep_f = same & (col <= row)
    keep_b = same & (col >= row)
    lower = jnp.where(keep_f, 1.0, 0.0).astype(BF16)
    upper = jnp.where(keep_b, 1.0, 0.0).astype(BF16)
    q_scale = C_DK ** -0.5
    st_ref[...] = jnp.zeros_like(st_ref)

    n_heads = q_ref.shape[2] // LANES

    def tile_step(j, d, hd, z_all):
        forward = d == 0
        b_ref = bf_ref if forward else bb_ref
        cum, keep = (lower, keep_f) if forward else (upper, keep_b)
        rows = pl.ds(pl.multiple_of(j * tile, tile), tile)
        ls = slice(hd * LANES, (hd + 1) * LANES)
        si = d * n_heads + hd
        q = q_ref[0, rows, ls].astype(F32) * q_scale
        k = k_ref[0, rows, ls].astype(F32)
        vb = v_ref[0, rows, ls]
        la = _log_sigmoid(z_all[:, ls] + b_ref[:, ls]) * (1.0 / C_TAU)
        r = _dot(cum, jnp.concatenate(_split_bf16(la), axis=1))
        yield
        b = r[:, :LANES] + r[:, LANES:]
        b3 = b.reshape(n_chunks, C_CHUNK, LANES)
        edge = b3[:, C_CHUNK - 1:, :] if forward else b3[:, :1, :]
        tot = jnp.broadcast_to(edge, b3.shape).reshape(tile, LANES)
        qf = q * jnp.exp(b)
        qt = qf.astype(BF16)
        kt = (k * jnp.exp(-b)).astype(BF16)
        k2 = k * jnp.exp(tot - b)
        raw = _dot_nt(qt, kt)
        k2_blk = jnp.concatenate([jnp.where(row_chunk == c, k2, 0.0).astype(BF16) for c in range(n_chunks)], axis=1)
        kv = _dot_tn(vb, k2_blk)
        yield
        intra = _dot(jnp.where(keep, raw, 0.0).astype(BF16), vb)
        dec = jnp.exp(edge)
        q_blk = jnp.concatenate([jnp.where(row_chunk == c, qf, 0.0).astype(BF16) for c in range(n_chunks)], axis=1)
        st = st_ref[si]
        entering = [None] * n_chunks
        for c in (range(n_chunks) if forward else range(n_chunks - 1, -1, -1)):
            entering[c] = st.astype(BF16)
            st = st * dec[c] + kv[:, c * LANES:(c + 1) * LANES]
        st_ref[si] = st
        inter = _dot_nt(q_blk, jnp.concatenate(entering, axis=1))
        yield
        acc_ref[d, rows, ls] = intra + inter

    def body(it, carry):
        tiles = (it, n_tiles - 1 - it)
        z = [_dot(dec_ref[0, pl.ds(pl.multiple_of(tiles[d] * tile, tile), tile), :], (wf_ref, wb_ref)[d][...])
             for d in (0, 1)]
        _run_skewed([tile_step(tiles[d], d, hd, z[d]) for hd in range(n_heads) for d in (0, 1)])
        return carry

    lax.fori_loop(0, n_tiles, body, 0)
    for hd in range(n_heads):
        ls = slice(hd * LANES, (hd + 1) * LANES)
        y = _rms(acc_ref[0, :, ls] + acc_ref[1, :, ls], gco_ref[...])
        gate = g_ref[0, :, ls].astype(F32)
        o_ref[0, :, ls] = (y * (gate * _sigmoid(gate))).astype(o_ref.dtype)


def _gla(cqkvg, dec, wf, wb, bf, bb, gco, bsz, t_len, tile, hp=4):
    c3 = cqkvg.reshape(bsz, t_len, 4 * C_HEADS * LANES)
    d3 = dec.reshape(bsz, t_len, LANES)
    groups = C_HEADS // hp
    wl = hp * LANES

    def cspec(j):
        return pl.BlockSpec((1, t_len, wl), lambda b, h: (b, 0, j * groups + h))

    out = pl.pallas_call(
        functools.partial(_gla_kernel, tile=tile),
        grid=(bsz, groups),
        in_specs=[cspec(0), cspec(1), cspec(2), cspec(3),
                  pl.BlockSpec((1, t_len, LANES), lambda b, h: (b, 0, 0)),
                  pl.BlockSpec((LANES, wl), lambda b, h: (0, h)),
                  pl.BlockSpec((LANES, wl), lambda b, h: (0, h)),
                  pl.BlockSpec((1, wl), lambda b, h: (0, h)),
                  pl.BlockSpec((1, wl), lambda b, h: (0, h)),
                  pl.BlockSpec((1, LANES), lambda b, h: (0, 0))],
        out_specs=pl.BlockSpec((1, t_len, wl), lambda b, h: (b, 0, h)),
        out_shape=jax.ShapeDtypeStruct((bsz, t_len, C_HEADS * LANES), BF16),
        scratch_shapes=[pltpu.VMEM((2, t_len, wl), F32), pltpu.VMEM((2 * hp, C_DV, C_DK), F32)],
        compiler_params=_cparams(("parallel", "parallel")),
        name="gla",
    )(c3, c3, c3, c3, d3, wf, wb, bf, bb, gco)
    return out.reshape(bsz * t_len, C_HEADS * LANES)


def _merge_stages(x, rows, gmix_ref, wgate_ref, bg_ref, y_refs, wp_refs, wo_ref, out):
    h = _rms(x, gmix_ref[...]).astype(BF16)
    dots = []
    for j in range(N_BRANCH):
        sl = slice(j * D_MODEL, (j + 1) * D_MODEL)
        dots.append((_dot(h, wgate_ref[:, sl]), _dot(y_refs[j][rows, :], wp_refs[j][...])))
    yield
    m = None
    for j, (gate, proj) in enumerate(dots):
        term = _sigmoid(gate + bg_ref[:, j * D_MODEL:(j + 1) * D_MODEL]) * proj
        m = term if m is None else m + term
    y = _dot(m.astype(BF16), wo_ref[...])
    yield
    out.append(x + y)


def _merge_tile(x, gmix_ref, wgate_ref, bg_ref, y_refs, wp_refs, wo_ref):
    out = []
    for _ in _merge_stages(x, slice(None), gmix_ref, wgate_ref, bg_ref, y_refs, wp_refs, wo_ref, out):
        pass
    return out[0]


def _merge_specs(tm):
    tok = lambda w: pl.BlockSpec((tm, w), lambda i: (i, 0))
    return [tok(512), tok(512), tok(512),
            _const_spec((1, D_MODEL)),
            _const_spec((D_MODEL, N_BRANCH * D_MODEL)),
            _const_spec((1, N_BRANCH * D_MODEL)),
            _const_spec((512, D_MODEL)), _const_spec((512, D_MODEL)), _const_spec((512, D_MODEL)),
            _const_spec((D_MODEL, D_MODEL))]


def _merge_ffn_kernel(x_ref, ya_ref, yb_ref, yc_ref, gmix_ref, wgate_ref, bg_ref, wpa_ref, wpb_ref, wpc_ref, wo_ref,
                      g_ref, wg_ref, wu_ref, wd_ref, o_ref, *, chunk):
    x = _merge_tile(x_ref[...], gmix_ref, wgate_ref, bg_ref, (ya_ref, yb_ref, yc_ref),
                    (wpa_ref, wpb_ref, wpc_ref), wo_ref)
    h = _rms(x, g_ref[...]).astype(BF16)
    o_ref[...] = x
    for c0 in range(0, D_FF, chunk):
        gt = _dot(h, wg_ref[:, c0:c0 + chunk])
        up = _dot(h, wu_ref[:, c0:c0 + chunk])
        a = (gt * _sigmoid(gt) * up).astype(BF16)
        o_ref[...] += _dot(a, wd_ref[c0:c0 + chunk, :])


def _merge_ffn(x2, merge_args, g, wg, wu, wd, tm, chunk=256):
    n = x2.shape[0]
    return pl.pallas_call(
        functools.partial(_merge_ffn_kernel, chunk=chunk),
        grid=(n // tm,),
        in_specs=[pl.BlockSpec((tm, D_MODEL), lambda i: (i, 0))] + _merge_specs(tm)
                 + [_const_spec((1, D_MODEL)),
                    _const_spec((D_MODEL, D_FF)), _const_spec((D_MODEL, D_FF)), _const_spec((D_FF, D_MODEL))],
        out_specs=pl.BlockSpec((tm, D_MODEL), lambda i: (i, 0)),
        out_shape=jax.ShapeDtypeStruct((n, D_MODEL), F32),
        compiler_params=_cparams(("parallel",)),
        name="merge_ffn",
    )(x2, *merge_args, g, wg, wu, wd)


def _merge_router_kernel(x_ref, ya_ref, yb_ref, yc_ref, gmix_ref, wgate_ref, bg_ref, wpa_ref, wpb_ref, wpc_ref, wo_ref,
                         g_ref, wr_ref, xo_ref, h_ref, gate_ref, rank_ref, rank_t_ref, cnt_ref, tot_ref, carry_ref,
                         *, tb):
    tm = x_ref.shape[0]

    @pl.when(pl.program_id(0) == 0)
    def _():
        carry_ref[...] = jnp.zeros_like(carry_ref)

    lane = lax.broadcasted_iota(jnp.int32, (tb, LANES), 1).astype(F32)
    row = lax.broadcasted_iota(jnp.int32, (tb, tb), 0)
    col = lax.broadcasted_iota(jnp.int32, (tb, tb), 1)
    earlier = jnp.where(col < row, 1.0, 0.0).astype(BF16)
    neg = jnp.float32(-jnp.inf)
    w_hi, w_lo = _split_bf16(wr_ref[...])
    w_cat = jnp.concatenate([w_hi, w_lo], axis=1)
    local = []

    def block(u):
        rows = slice(u * tb, (u + 1) * tb)
        merged = []
        yield from _merge_stages(x_ref[rows, :], rows, gmix_ref, wgate_ref, bg_ref, (ya_ref, yb_ref, yc_ref),
                                 (wpa_ref, wpb_ref, wpc_ref), wo_ref, merged)
        x = merged[0]
        xo_ref[rows, :] = x
        hf = _rms(x, g_ref[...])
        h_ref[rows, :] = hf.astype(h_ref.dtype)
        hf_hi, hf_lo = _split_bf16(hf)
        wide = _dot(hf_hi, w_cat)
        logits = wide[:, :LANES] + (wide[:, LANES:] + _dot(hf_lo, w_hi))
        yield
        logits = jnp.where(lane < N_EXPERTS, logits, neg)
        v1 = jnp.max(logits, axis=-1, keepdims=True)
        i1 = jnp.min(jnp.where(logits == v1, lane, float(LANES)), axis=-1, keepdims=True)
        rest = jnp.where(lane == i1, neg, logits)
        v2 = jnp.max(rest, axis=-1, keepdims=True)
        i2 = jnp.min(jnp.where(rest == v2, lane, float(LANES)), axis=-1, keepdims=True)
        e2 = jnp.exp(v2 - v1)
        w1 = 1.0 / (1.0 + e2)
        w2 = e2 / (1.0 + e2)
        gate_ref[rows, :] = jnp.where(lane == i1, w1, 0.0) + jnp.where(lane == i2, w2, 0.0)
        chosen = (lane == i1) | (lane == i2)
        assign = jnp.where(chosen, 1.0, 0.0)
        inside = _dot(earlier, assign.astype(BF16))
        yield
        local.append((chosen, inside, jnp.sum(assign, axis=0, keepdims=True)))

    _run_skewed([block(u) for u in range(tm // tb)])
    carry = carry_ref[...]
    for u, (chosen, inside, count) in enumerate(local):
        rows = slice(u * tb, (u + 1) * tb)
        rank = jnp.where(chosen, inside + carry[0:1, :], -1.0)
        rank_ref[rows, :] = rank
        rank_t_ref[:, rows] = rank.T[:N_EXPERTS, :]
        cnt_ref[u] = carry
        carry = carry + count
    carry_ref[...] = carry
    tot_ref[...] = carry


def _merge_router(x2, merge_args, g, wr_pad, tm, tb):
    n = x2.shape[0]
    nt = n // tm
    return pl.pallas_call(
        functools.partial(_merge_router_kernel, tb=tb),
        grid=(nt,),
        in_specs=[pl.BlockSpec((tm, D_MODEL), lambda i: (i, 0))] + _merge_specs(tm)
                 + [_const_spec((1, D_MODEL)),
                    _const_spec((D_MODEL, LANES))],
        out_specs=[pl.BlockSpec((tm, D_MODEL), lambda i: (i, 0)),
                   pl.BlockSpec((tm, D_MODEL), lambda i: (i, 0)),
                   pl.BlockSpec((tm, LANES), lambda i: (i, 0)),
                   pl.BlockSpec((tm, LANES), lambda i: (i, 0)),
                   pl.BlockSpec((N_EXPERTS, tm), lambda i: (0, i)),
                   pl.BlockSpec((tm // tb, 8, LANES), lambda i: (i, 0, 0)),
                   pl.BlockSpec((8, LANES), lambda i: (0, 0))],
        out_shape=[jax.ShapeDtypeStruct((n, D_MODEL), F32),
                   jax.ShapeDtypeStruct((n, D_MODEL), BF16),
                   jax.ShapeDtypeStruct((n, LANES), F32),
                   jax.ShapeDtypeStruct((n, LANES), F32),
                   jax.ShapeDtypeStruct((N_EXPERTS, n), F32),
                   jax.ShapeDtypeStruct((n // tb, 8, LANES), F32),
                   jax.ShapeDtypeStruct((8, LANES), F32)],
        scratch_shapes=[pltpu.VMEM((8, LANES), F32)],
        compiler_params=_cparams(("arbitrary",)),
        name="merge_router",
    )(x2, *merge_args, g, wr_pad)


def _moe_plan(cnt, tot, n, tb, rs, tg):
    i32 = jnp.int32
    counts = tot[0, :N_EXPERTS].astype(i32)
    cum = jnp.concatenate([cnt[:, 0, :N_EXPERTS], tot[:1, :N_EXPERTS]], axis=0).astype(i32)
    n_sb = (counts + rs - 1) // rs
    sb_end = jnp.cumsum(n_sb)
    sb_start = sb_end - n_sb
    goff = sb_start * rs
    n_valid_sb = sb_end[-1]
    s_max = 2 * n // rs + N_EXPERTS
    s_ids = jnp.arange(s_max, dtype=i32)
    last_sb = n_valid_sb - 1
    experts = jnp.arange(N_EXPERTS, dtype=i32)

    def owner(sb):
        e = jnp.minimum(jnp.sum(sb[:, None] >= sb_end[None, :], axis=1), N_EXPERTS - 1).astype(i32)
        return e, e[:, None] == experts[None, :]

    pick = lambda onehot, table: jnp.sum(jnp.where(onehot, table[None, :], 0), axis=1)
    s_eff = jnp.minimum(s_ids, last_sb)
    sb_expert, sb_hot = owner(s_eff)
    sb_rows = jnp.clip(pick(sb_hot, counts) - (s_eff - pick(sb_hot, sb_start)) * rs, 0, rs)
    sb_tiles = ((sb_rows + tb - 1) // tb).astype(i32)

    t_ids = jnp.arange(s_max * rs // tb, dtype=i32)
    t_sb = t_ids * tb // rs
    t_exp, t_hot = owner(jnp.minimum(t_sb, last_sb))
    t_count = pick(t_hot, counts)
    t_r0 = t_ids * tb - pick(t_hot, goff)
    t_valid = (t_sb <= last_sb) & (t_r0 < t_count)
    t_r1 = jnp.minimum(t_r0 + tb, t_count)
    cum_e = jnp.sum(jnp.where(t_hot[None, :, :], cum[:, None, :], 0), axis=2)
    t_lo = jnp.sum(cum_e[1:] <= t_r0[None, :], axis=0)
    t_hi = jnp.where(t_valid, jnp.sum(cum_e[:-1] < t_r1[None, :], axis=0), t_lo)
    gather_tabs = tuple(a.astype(i32) for a in (t_exp, t_r0, t_lo, t_hi))

    start = goff[None, :] + cum[:-1]
    num = cum[1:] - cum[:-1]
    n_rows = s_max * rs
    off0 = jnp.minimum(start // ROW_ALIGN * ROW_ALIGN, n_rows - tb)
    need1 = (num > 0) & (start + num > off0 + tb)
    off1 = jnp.where(need1, jnp.minimum(off0 + tb, n_rows - tb), 0)
    offs = jnp.stack([off0, off1], axis=-1).reshape(-1).astype(i32)
    expert_tabs = (sb_expert, sb_tiles, last_sb.reshape(1).astype(i32))
    return gather_tabs, expert_tabs, (offs, need1.reshape(-1).astype(i32), goff.astype(i32))


def _gather_kernel(exp_ref, r0_ref, lo_ref, hi_ref, rank_ref, h_ref, o_ref, acc_ref, *, tb):
    step = pl.program_id(0)
    n_sub = o_ref.shape[0] // tb
    n_blk = rank_ref.shape[1]
    row = lax.broadcasted_iota(jnp.int32, (tb, tb), 0)
    tiles = [step * n_sub + u for u in range(n_sub)]
    n_trips = functools.reduce(jnp.maximum, [(hi_ref[t] - lo_ref[t] + 1) >> 1 for t in tiles])

    @pl.when(n_trips == 0)
    def _():
        o_ref[...] = jnp.zeros_like(o_ref)

    @pl.when(n_trips > 0)
    def _():
        acc_ref[...] = jnp.zeros_like(acc_ref)

        def body(p, carry):
            parts = []
            for t in tiles:
                e, lo, hi = exp_ref[t], lo_ref[t], hi_ref[t]
                want = row + r0_ref[t]
                for j in range(2):
                    kb = lo + 2 * p + j
                    kbc = jnp.minimum(kb, n_blk - 1)
                    rk = rank_ref[e, pl.ds(kbc, 1), :].astype(jnp.int32)
                    onehot = jnp.where(jnp.where(kb < hi, rk, -1) == want, 1.0, 0.0).astype(BF16)
                    parts.append(_dot(onehot, h_ref[pl.ds(pl.multiple_of(kbc * tb, tb), tb), :]))
            for u in range(n_sub):
                acc_ref[u * tb:(u + 1) * tb, :] += parts[2 * u] + parts[2 * u + 1]
            return carry

        lax.fori_loop(0, n_trips, body, 0)
        o_ref[...] = acc_ref[...].astype(o_ref.dtype)


def _gather(tabs, rank_t3, h, n_rows, tg, tb):
    n = h.shape[0]
    return pl.pallas_call(
        functools.partial(_gather_kernel, tb=tb),
        grid_spec=pltpu.PrefetchScalarGridSpec(
            num_scalar_prefetch=4,
            grid=(n_rows // tg,),
            in_specs=[_const_spec((N_EXPERTS, n // tb, tb)),
                      _const_spec((n, D_MODEL))],
            out_specs=pl.BlockSpec((tg, D_MODEL), lambda t, e, r, lo, hi: (t, 0)),
            scratch_shapes=[pltpu.VMEM((tg, D_MODEL), F32)]),
        out_shape=jax.ShapeDtypeStruct((n_rows, D_MODEL), BF16),
        compiler_params=_cparams(("arbitrary",)),
        name="moe_gather",
    )(*tabs, rank_t3, h)


def _experts_kernel(exp_ref, tiles_ref, last_ref, xs_ref, wg_ref, wu_ref, wd_ref, o_ref,
                    acc_ref, wgb_ref, wub_ref, wdb_ref, *, tb):
    s, c = pl.program_id(0), pl.program_id(1)
    nc = pl.num_programs(1)

    @pl.when((s > last_ref[0]) & (c == 0))
    def _():
        o_ref[...] = jnp.zeros_like(o_ref)

    @pl.when(s <= last_ref[0])
    def _():
        @pl.when(c == 0)
        def _():
            acc_ref[...] = jnp.zeros_like(acc_ref)

        wgb_ref[...] = wg_ref[0].astype(BF16)
        wub_ref[...] = wu_ref[0].astype(BF16)
        wdb_ref[...] = wd_ref[0].astype(BF16)

        def tile_step(t):
            rows = pl.ds(pl.multiple_of(t * tb, tb), tb)
            x = xs_ref[rows, :]
            gt = _dot(x, wgb_ref[...])
            up = _dot(x, wub_ref[...])
            yield
            a = (gt * _sigmoid(gt) * up).astype(BF16)
            y = _dot(a, wdb_ref[...])
            yield
            acc_ref[rows, :] += y

        def body(t4, carry):
            _run_skewed([tile_step(4 * t4 + u) for u in range(4)])
            return carry

        n_tiles = tiles_ref[s]
        lax.fori_loop(0, n_tiles >> 2, body, 0)
        done = n_tiles & ~3

        @pl.when((n_tiles & 2) == 2)
        def _():
            _run_skewed([tile_step(done), tile_step(done + 1)])

        @pl.when((n_tiles & 1) == 1)
        def _():
            _run_skewed([tile_step(n_tiles - 1)])

        @pl.when(c == nc - 1)
        def _():
            o_ref[...] = acc_ref[...].astype(o_ref.dtype)


def _experts(tabs, xs, wg, wu, wd, rs, tb, fc):
    n_rows = xs.shape[0]
    nc = D_FF_EXPERT // fc

    def sb(s, last):
        return jnp.minimum(s, last[0])

    def chunk(s, c, last):
        return jnp.where(s <= last[0], c, nc - 1)

    return pl.pallas_call(
        functools.partial(_experts_kernel, tb=tb),
        grid_spec=pltpu.PrefetchScalarGridSpec(
            num_scalar_prefetch=3,
            grid=(n_rows // rs, nc),
            in_specs=[pl.BlockSpec((rs, D_MODEL), lambda s, c, ex, tl, last: (sb(s, last), 0)),
                      pl.BlockSpec((1, D_MODEL, fc), lambda s, c, ex, tl, last: (ex[sb(s, last)], 0, chunk(s, c, last))),
                      pl.BlockSpec((1, D_MODEL, fc), lambda s, c, ex, tl, last: (ex[sb(s, last)], 0, chunk(s, c, last))),
                      pl.BlockSpec((1, fc, D_MODEL), lambda s, c, ex, tl, last: (ex[sb(s, last)], chunk(s, c, last), 0))],
            out_specs=pl.BlockSpec((rs, D_MODEL), lambda s, c, ex, tl, last: (s, 0)),
            scratch_shapes=[pltpu.VMEM((rs, D_MODEL), F32),
                            pltpu.VMEM((D_MODEL, fc), BF16), pltpu.VMEM((D_MODEL, fc), BF16),
                            pltpu.VMEM((fc, D_MODEL), BF16)]),
        out_shape=jax.ShapeDtypeStruct((n_rows, D_MODEL), BF16),
        compiler_params=_cparams(("arbitrary", "arbitrary")),
        name="moe_experts",
    )(*tabs, xs, wg, wu, wd)


def _combine_kernel(off_ref, need_ref, goff_ref, x_ref, gate_ref, rank_ref, *rest, final_norm):
    y_refs, (gfin_ref, o_ref) = rest[:2 * N_EXPERTS], rest[2 * N_EXPERTS:]
    i = pl.program_id(0)
    tt = x_ref.shape[0]
    tb = y_refs[0].shape[0]
    lane = lax.broadcasted_iota(jnp.int32, (tt, LANES), 1)
    col = lax.broadcasted_iota(jnp.int32, (tt, tb), 1)
    gate, rank = gate_ref[...], rank_ref[...]

    def expert_terms(e):
        ge = jnp.sum(jnp.where(lane == e, gate, 0.0), axis=-1, keepdims=True)
        rk = jnp.sum(jnp.where(lane == e, rank, 0.0), axis=-1, keepdims=True).astype(jnp.int32)
        return ge, jnp.where(rk >= 0, rk + goff_ref[e], -1)

    def window(e, j, pos):
        onehot = jnp.where(pos - off_ref[(i * N_EXPERTS + e) * 2 + j] == col, 1.0, 0.0).astype(BF16)
        return _dot(onehot, y_refs[2 * e + j][...])

    y = x_ref[...]
    for e in range(N_EXPERTS):
        ge, pos = expert_terms(e)
        y = y + ge * window(e, 0, pos)
    o_ref[...] = y
    for e in range(N_EXPERTS):
        @pl.when(need_ref[i * N_EXPERTS + e] == 1)
        def _(e=e):
            ge, pos = expert_terms(e)
            o_ref[...] += ge * window(e, 1, pos)
    if final_norm:
        o_ref[...] = _rms(o_ref[...], gfin_ref[...])


def _combine(tabs, x2, gate, rank, ys, gfin, tt, tb, final_norm):
    n = x2.shape[0]

    def yspec(slot):
        return pl.BlockSpec((pl.Element(tb), pl.Element(D_MODEL)),
                            lambda i, off, need, goff: (pl.multiple_of(off[i * 2 * N_EXPERTS + slot], ROW_ALIGN), 0))

    tok = lambda w: pl.BlockSpec((tt, w), lambda i, off, need, goff: (i, 0))
    return pl.pallas_call(
        functools.partial(_combine_kernel, final_norm=final_norm),
        grid_spec=pltpu.PrefetchScalarGridSpec(
            num_scalar_prefetch=3,
            grid=(n // tt,),
            in_specs=[tok(D_MODEL), tok(LANES), tok(LANES)] + [yspec(k) for k in range(2 * N_EXPERTS)]
                     + [pl.BlockSpec((1, D_MODEL), lambda i, off, need, goff: (0, 0))],
            out_specs=tok(D_MODEL)),
        out_shape=jax.ShapeDtypeStruct((n, D_MODEL), F32),
        compiler_params=_cparams(("arbitrary",)),
        name="moe_combine",
    )(*tabs, x2, gate, rank, *([ys] * (2 * N_EXPERTS)), gfin)


def _merge_moe(x2, merge_args, g_ffn, w_router, wg, wu, wd, gfin, final_norm):
    n = x2.shape[0]
    tb = min(256, n // 2)
    rs = max(9 * n // (8 * N_EXPERTS) // tb * tb, 2 * tb)
    tg = next(k * tb for k in (3, 2, 1) if rs % (k * tb) == 0)
    wr = jnp.pad(w_router, ((0, 0), (0, LANES - N_EXPERTS)))
    x2, h, gate, rank, rank_t, cnt, tot = _merge_router(x2, merge_args, g_ffn, wr, min(2 * tb, n), tb)
    gather_tabs, expert_tabs, combine_tabs = _moe_plan(cnt, tot, n, tb, rs, tg)
    n_rows = (2 * n // rs + N_EXPERTS) * rs
    xs = _gather(gather_tabs, rank_t.reshape(N_EXPERTS, n // tb, tb), h, n_rows, tg, tb)
    ys = _experts(expert_tabs, xs, wg, wu, wd, rs, tb, 512)
    return _combine(combine_tabs, x2, gate, rank, ys, gfin, tb, tb, final_norm)


def _final_norm_kernel(x_ref, g_ref, o_ref):
    o_ref[...] = _rms(x_ref[...], g_ref[...])


def _final_norm(x2, g, tm):
    n = x2.shape[0]
    return pl.pallas_call(
        _final_norm_kernel,
        grid=(n // tm,),
        in_specs=[pl.BlockSpec((tm, D_MODEL), lambda i: (i, 0)), _const_spec((1, D_MODEL))],
        out_specs=pl.BlockSpec((tm, D_MODEL), lambda i: (i, 0)),
        out_shape=jax.ShapeDtypeStruct((n, D_MODEL), F32),
        compiler_params=_cparams(("parallel",)),
        name="final_norm",
    )(x2, g)


def _rope_tables(t_len):
    rows = t_len // GRID_W
    row = np.repeat(np.arange(rows, dtype=np.float64), GRID_W)
    col = np.tile(np.arange(GRID_W, dtype=np.float64), rows)
    lane = np.arange(LANES)

    def table(rot_dim, lane_in_slice, active):
        n_freq = rot_dim // 4
        inv_freq = ROPE_THETA ** (-np.arange(n_freq, dtype=np.float64) / n_freq)
        freq = inv_freq[lane_in_slice % n_freq]
        use_row = (lane_in_slice % rot_dim) < (rot_dim // 2)
        ang = np.where(use_row[None, :], row[:, None], col[:, None]) * freq[None, :]
        sign = np.where((lane_in_slice % (rot_dim // 2)) < n_freq, -1.0, 1.0)
        cos = np.where(active[None, :], np.cos(ang), 1.0)
        sin = np.where(active[None, :], np.sin(ang) * sign[None, :], 0.0)
        return jnp.asarray(cos, F32), jnp.asarray(sin, F32)

    cos_a, sin_a = table(A_HEAD_DIM, lane % A_HEAD_DIM, np.ones(LANES, bool))
    in_rope = (lane >= B_NOPE_DIM) & (lane < B_NOPE_DIM + B_ROPE_DIM)
    cos_b, sin_b = table(B_ROPE_DIM, (lane - B_NOPE_DIM) % B_ROPE_DIM, in_rope)
    return cos_a, sin_a, cos_b, sin_b


def _pack_w_in(w):
    o_c, o_gate = 1312, 3392
    zeros = lambda k: jnp.zeros((D_MODEL, k), BF16)
    cols = [w[:, :o_c].astype(BF16), zeros(LANES - B_ROPE_DIM), w[:, o_c:o_gate].astype(BF16),
            zeros(LANES - 2 * C_GATE_RANK)]
    return jnp.concatenate(cols, axis=1), w[:, o_gate:].astype(BF16)


def _pack_b_weights(w_q_up, w_kv_up):
    hq = w_q_up.reshape(B_Q_RANK, B_HEADS, B_NOPE_DIM + B_ROPE_DIM)
    wq = jnp.pad(hq, ((0, 0), (0, 0), (0, LANES - B_NOPE_DIM - B_ROPE_DIM))).reshape(B_Q_RANK, B_HEADS * LANES)
    hkv = w_kv_up.reshape(B_KV_RANK, B_HEADS, B_NOPE_DIM + B_V_DIM)
    wk = jnp.pad(hkv[:, :, :B_NOPE_DIM], ((0, 0), (0, 0), (0, LANES - B_NOPE_DIM))).reshape(B_KV_RANK, B_HEADS * LANES)
    wv = hkv[:, :, B_NOPE_DIM:].reshape(B_KV_RANK, B_HEADS * B_V_DIM)
    return wq.astype(BF16), wk.astype(BF16), wv.astype(BF16)


def _rope_placement():
    e = np.zeros((LANES, LANES), np.float32)
    e[np.arange(B_ROPE_DIM), B_NOPE_DIM + np.arange(B_ROPE_DIM)] = 1.0
    return jnp.asarray(e, BF16)


def kernel(x, w_in, b_gate, g_mix, g_a_q, g_a_k, g_b_q, w_b_q_up, g_b_kv, w_b_kv_up, w_c_af_up, b_c_af, w_c_ab_up, b_c_ab, g_c_out, w_pa, w_pb, w_pc, w_out, g_ffn, w_ff_gate, w_ff_up, w_ff_down, w_router, w_e_gate, w_e_up, w_e_down, g_final):
    bsz, t_len, _ = x.shape
    n = bsz * t_len
    depth = w_in.shape[0]
    tm = min(512, t_len)
    tq_a = min(1024, t_len)
    tq_b = min(1024, t_len)
    gla_tile = min(256, t_len)

    cos_a, sin_a, cos_b, sin_b = _rope_tables(t_len)
    e_mat = _rope_placement()
    row = lambda v: v.reshape(1, -1).astype(F32)
    x2 = x.reshape(n, D_MODEL)

    for i in range(depth):
        wq, wk, wv = _pack_b_weights(w_b_q_up[i], w_b_kv_up[i])
        w_packed, w_gate = _pack_w_in(w_in[i])
        qa, ka, av2, qb, kb, vb, cqkvg, dec = _fused_in(
            x2, row(g_mix[i]), w_packed, row(jnp.tile(g_a_q[i], 2)), row(jnp.tile(g_a_k[i], 2)),
            cos_a, sin_a, row(g_b_q[i]), row(g_b_kv[i]), wq, wk, wv, e_mat, cos_b, sin_b, tm, t_len)
        ya = _attn_a(qa, ka, av2, bsz, t_len, tq_a)
        yb = _attn_b(qb, kb, vb, bsz, t_len, tq_b)

        wf = jnp.pad(w_c_af_up[i], ((0, LANES - C_GATE_RANK), (0, 0))).astype(BF16)
        wb = jnp.pad(w_c_ab_up[i], ((C_GATE_RANK, LANES - 2 * C_GATE_RANK), (0, 0))).astype(BF16)
        yc = _gla(cqkvg, dec, wf, wb, row(b_c_af[i]), row(b_c_ab[i]), row(g_c_out[i]), bsz, t_len, gla_tile)

        merge_args = (ya, yb, yc, row(g_mix[i]), w_gate, row(b_gate[i]), w_pa[i].astype(BF16), w_pb[i].astype(BF16),
                      w_pc[i].astype(BF16), w_out[i].astype(BF16))
        j = i // 2
        last = i == depth - 1
        if i % 2 == 0:
            x2 = _merge_ffn(x2, merge_args, row(g_ffn[i]), w_ff_gate[j].astype(BF16), w_ff_up[j].astype(BF16),
                            w_ff_down[j].astype(BF16), tm)
            if last:
                x2 = _final_norm(x2, row(g_final), tm)
        else:
            x2 = _merge_moe(x2, merge_args, row(g_ffn[i]), w_router[j], w_e_gate[j], w_e_up[j], w_e_down[j],
                            row(g_final), last)
    return x2.reshape(bsz, t_len, D_MODEL)
```

```python
import functools

import jax
import jax.numpy as jnp
import numpy as np
from jax import lax
from jax.experimental import pallas as pl
from jax.experimental.pallas import tpu as pltpu

F32 = jnp.float32
BF16 = jnp.bfloat16

D_MODEL = 1024
GRID_W = 64
ROPE_THETA = 10000.0
EPS = 1e-6
A_HEADS, A_KV_HEADS, A_HEAD_DIM = 8, 2, 64
B_HEADS, B_NOPE_DIM, B_ROPE_DIM, B_V_DIM = 8, 64, 32, 64
B_Q_RANK = B_KV_RANK = 256
C_HEADS, C_DK, C_DV, C_GATE_RANK, C_TAU, C_CHUNK = 4, 128, 128, 16, 16.0, 64
N_BRANCH = 3
D_FF = 2816
N_EXPERTS, TOP_K, D_FF_EXPERT = 8, 2, 3584

LANES = 128
ROW_ALIGN = 16
VMEM_LIMIT = 56 * 1024 * 1024


def _cparams(sem):
    return pltpu.CompilerParams(dimension_semantics=sem, vmem_limit_bytes=VMEM_LIMIT)


def _const_spec(shape):
    nd = len(shape)
    return pl.BlockSpec(shape, lambda *_: (0,) * nd, pipeline_mode=pl.Buffered(1))


def _rms(xf, g):
    return xf * lax.rsqrt(jnp.mean(xf * xf, axis=-1, keepdims=True) + EPS) * g


def _sigmoid(x):
    return 1.0 / (1.0 + jnp.exp(-x))


def _split_bf16(x):
    hi = x.astype(BF16)
    lo = (x - hi.astype(F32)).astype(BF16)
    return hi, lo


def _dot(a, b):
    return jnp.dot(a, b, preferred_element_type=F32)


def _dot_nt(a, b):
    return lax.dot_general(a, b, (((1,), (1,)), ((), ())), preferred_element_type=F32)


def _dot_tn(a, b):
    return lax.dot_general(a, b, (((0,), (0,)), ((), ())), preferred_element_type=F32)


def _rope_group(x, cos, sin_signed, lane, half):
    fwd = pltpu.roll(x, LANES - half, axis=1)
    bwd = pltpu.roll(x, half, axis=1)
    swapped = jnp.where((lane % (2 * half)) < half, fwd, bwd)
    return x * cos + swapped * sin_signed


def _seg64_meansq(x, lane):
    sq = x * x
    lo = jnp.sum(jnp.where(lane < 64, sq, 0.0), axis=-1, keepdims=True)
    hi = jnp.sum(jnp.where(lane < 64, 0.0, sq), axis=-1, keepdims=True)
    return jnp.where(lane < 64, lo, hi) * (1.0 / 64.0)


F_AQ, F_MIX, F_B, F_C, F_WIDTH = 0, 512, 1024, 1536, 3584


def _dup_half(x, lane, low):
    other = pltpu.roll(x, 64, axis=1)
    return jnp.where(lane < 64, x, other) if low else jnp.where(lane < 64, other, x)


def _fused_in_kernel(x_ref, g_ref, w_ref, gaq_ref, gak_ref, cosa_ref, sina_ref,
                     gbq_ref, gbkv_ref, wq_ref, wk_ref, wv_ref, e_ref, cosb_ref, sinb_ref,
                     qa_ref, ka_ref, va_ref, qb_ref, kb_ref, vb_ref, c_ref, dec_ref):
    rows = x_ref.shape[0]
    lane = lax.broadcasted_iota(jnp.int32, (rows, LANES), 1)
    h = _rms(x_ref[...], g_ref[...]).astype(BF16)

    def proj(off, width):
        return _dot(h, w_ref[:, off:off + width])

    def chain_aq():
        y = proj(F_AQ, 512)
        yield
        cos, sin, gq = cosa_ref[...], sina_ref[...], gaq_ref[...]
        for gi in range(4):
            sl = slice(gi * LANES, (gi + 1) * LANES)
            x = y[:, sl]
            x = x * lax.rsqrt(_seg64_meansq(x, lane) + EPS) * gq
            qa_ref[:, sl] = (_rope_group(x, cos, sin, lane, 16) * (A_HEAD_DIM ** -0.5)).astype(qa_ref.dtype)

    k_rope = []

    def chain_akv():
        y = proj(F_MIX, 512)
        yield
        k, v = y[:, :LANES], y[:, LANES:2 * LANES]
        k_rope.append(y[:, 2 * LANES:3 * LANES].astype(BF16))
        dec_ref[...] = y[:, 3 * LANES:].astype(dec_ref.dtype)
        k = k * lax.rsqrt(_seg64_meansq(k, lane) + EPS) * gak_ref[...]
        k = _rope_group(k, cosa_ref[...], sina_ref[...], lane, 16)
        for kv in range(A_KV_HEADS):
            sl = slice(kv * LANES, (kv + 1) * LANES)
            ka_ref[:, sl] = _dup_half(k, lane, kv == 0).astype(ka_ref.dtype)
            va_ref[:, sl] = _dup_half(v, lane, kv == 0).astype(va_ref.dtype)

    def chain_b():
        y = proj(F_B, 512)
        yield
        cq = _rms(y[:, :B_Q_RANK], gbq_ref[...]).astype(BF16)
        ckv = _rms(y[:, B_Q_RANK:], gbkv_ref[...]).astype(BF16)
        q = _dot(cq, wq_ref[...])
        k = _dot(ckv, wk_ref[...])
        v = _dot(ckv, wv_ref[...])
        kr = _dot(k_rope[0], e_ref[...])
        yield
        cos, sin = cosb_ref[...], sinb_ref[...]
        kr = _rope_group(kr, cos, sin, lane, 8)
        scale = (B_NOPE_DIM + B_ROPE_DIM) ** -0.5
        for hd in range(B_HEADS):
            sl = slice(hd * LANES, (hd + 1) * LANES)
            qb_ref[:, sl] = (_rope_group(q[:, sl], cos, sin, lane, 8) * scale).astype(qb_ref.dtype)
            kb_ref[:, sl] = (k[:, sl] + kr).astype(kb_ref.dtype)
        vb_ref[...] = v.astype(vb_ref.dtype)

    def chain_plain(o_ref, off, c0, cw):
        y = proj(off + c0, cw)
        yield
        o_ref[:, c0:c0 + cw] = y.astype(o_ref.dtype)

    chains = [chain_aq(), chain_akv(), chain_b()]
    chains += [chain_plain(c_ref, F_C, c0, 512) for c0 in range(0, 2048, 512)]
    _run_skewed(chains)


def _fused_in(x2, g, w_packed, gaq, gak, cos_a, sin_a, gbq, gbkv, wq, wk, wv, e_mat, cos_b, sin_b, tm, t_len):
    n = x2.shape[0]
    nt = t_len // tm
    widths = (512, 256, 256, 1024, 1024, 512, 2048, LANES)
    table = pl.BlockSpec((tm, LANES), lambda i: (i % nt, 0))
    return pl.pallas_call(
        _fused_in_kernel,
        grid=(n // tm,),
        in_specs=[pl.BlockSpec((tm, D_MODEL), lambda i: (i, 0)),
                  _const_spec((1, D_MODEL)),
                  _const_spec((D_MODEL, F_WIDTH)),
                  _const_spec((1, LANES)), _const_spec((1, LANES)), table, table,
                  _const_spec((1, B_Q_RANK)), _const_spec((1, B_KV_RANK)),
                  _const_spec((B_Q_RANK, B_HEADS * LANES)),
                  _const_spec((B_KV_RANK, B_HEADS * LANES)),
                  _const_spec((B_KV_RANK, B_HEADS * B_V_DIM)),
                  _const_spec((LANES, LANES)), table, table],
        out_specs=[pl.BlockSpec((tm, w), lambda i: (i, 0)) for w in widths],
        out_shape=[jax.ShapeDtypeStruct((n, w), BF16) for w in widths],
        compiler_params=_cparams(("parallel",)),
        name="fused_in",
    )(x2, g, w_packed, gaq, gak, cos_a, sin_a, gbq, gbkv, wq, wk, wv, e_mat, cos_b, sin_b)


CHAIN_ROWS = 512


def _run_skewed(chains):
    pending, live = list(chains), []
    while pending or live:
        if pending:
            live.append(pending.pop(0))
        for g in reversed(list(live)):
            try:
                next(g)
            except StopIteration:
                live.remove(g)


def _attend(q, k, v3_ref, out):
    s = _dot_nt(q(), k())
    yield
    m = jnp.max(s, axis=-1, keepdims=True)
    p = jnp.exp(s - m).astype(BF16)
    r = _dot(p, v3_ref[...])
    yield
    out.append(r[:, :LANES] / r[:, LANES:])


def _attn_a_kernel(q_ref, k_ref, v_ref, o_ref, v3_ref):
    tq = q_ref.shape[1]
    n_kv = v3_ref.shape[0]

    @pl.when(pl.program_id(2) == 0)
    def _():
        for kv in range(n_kv):
            v3_ref[kv, :, :LANES] = v_ref[0, :, kv * LANES:(kv + 1) * LANES]
            v3_ref[kv, :, LANES:] = jnp.ones((v_ref.shape[1], LANES), BF16)

    rq = min(tq, CHAIN_ROWS)
    lane = lax.broadcasted_iota(jnp.int32, (rq, LANES), 1)
    first = lane < 64
    outs, chains = [], []
    for r0 in range(0, tq, rq):
        for gi in range(2 * n_kv):
            kv = gi // 2
            k = lambda kv=kv: k_ref[0, :, kv * LANES:(kv + 1) * LANES]
            for keep_first in (True, False):
                def q(gi=gi, keep_first=keep_first, r0=r0):
                    qp = q_ref[0, r0:r0 + rq, gi * LANES:(gi + 1) * LANES].astype(F32)
                    return jnp.where(first == keep_first, qp, 0.0).astype(BF16)
                chains.append(_attend(q, k, v3_ref.at[kv], outs))
    _run_skewed(chains)
    for ri, r0 in enumerate(range(0, tq, rq)):
        for gi in range(2 * n_kv):
            o0, o1 = outs[(ri * 2 * n_kv + gi) * 2], outs[(ri * 2 * n_kv + gi) * 2 + 1]
            o_ref[0, r0:r0 + rq, gi * LANES:(gi + 1) * LANES] = jnp.where(first, o0, o1).astype(o_ref.dtype)


def _attn_a(q, k2, v2, bsz, t_len, tq, n_kv=2):
    q3, k3, v3 = (a.reshape(bsz, t_len, a.shape[-1]) for a in (q, k2, v2))
    out = pl.pallas_call(
        _attn_a_kernel,
        grid=(bsz, A_KV_HEADS // n_kv, t_len // tq),
        in_specs=[pl.BlockSpec((1, tq, 256 * n_kv), lambda b, g, i: (b, i, g)),
                  pl.BlockSpec((1, t_len, LANES * n_kv), lambda b, g, i: (b, 0, g)),
                  pl.BlockSpec((1, t_len, LANES * n_kv), lambda b, g, i: (b, 0, g))],
        out_specs=pl.BlockSpec((1, tq, 256 * n_kv), lambda b, g, i: (b, i, g)),
        out_shape=jax.ShapeDtypeStruct((bsz, t_len, 512), BF16),
        scratch_shapes=[pltpu.VMEM((n_kv, t_len, 2 * LANES), BF16)],
        compiler_params=_cparams(("parallel", "parallel", "arbitrary")),
        name="attn_a",
    )(q3, k3, v3)
    return out.reshape(bsz * t_len, 512)


def _attn_b_kernel(q_ref, k_ref, v_ref, o_ref, v3_ref):
    tq = q_ref.shape[1]
    n_pairs = v3_ref.shape[0]

    @pl.when(pl.program_id(2) == 0)
    def _():
        for pi in range(n_pairs):
            v3_ref[pi, :, :LANES] = v_ref[0, :, pi * LANES:(pi + 1) * LANES]
            v3_ref[pi, :, LANES:] = jnp.ones((v_ref.shape[1], LANES), BF16)

    rq = min(tq, CHAIN_ROWS)
    lane = lax.broadcasted_iota(jnp.int32, (rq, LANES), 1)
    outs, chains = [], []
    for r0 in range(0, tq, rq):
        for hd in range(2 * n_pairs):
            sl = slice(hd * LANES, (hd + 1) * LANES)
            q = lambda sl=sl, r0=r0: q_ref[0, r0:r0 + rq, sl]
            k = lambda sl=sl: k_ref[0, :, sl]
            chains.append(_attend(q, k, v3_ref.at[hd // 2], outs))
    _run_skewed(chains)
    for ri, r0 in enumerate(range(0, tq, rq)):
        for pi in range(n_pairs):
            o0, o1 = outs[(ri * n_pairs + pi) * 2], outs[(ri * n_pairs + pi) * 2 + 1]
            o_ref[0, r0:r0 + rq, pi * LANES:(pi + 1) * LANES] = jnp.where(lane < 64, o0, o1).astype(o_ref.dtype)


def _attn_b(q, k, v, bsz, t_len, tq, n_pairs=4):
    q3, k3, v3 = (a.reshape(bsz, t_len, a.shape[-1]) for a in (q, k, v))
    out = pl.pallas_call(
        _attn_b_kernel,
        grid=(bsz, B_HEADS // (2 * n_pairs), t_len // tq),
        in_specs=[pl.BlockSpec((1, tq, 256 * n_pairs), lambda b, p, i: (b, i, p)),
                  pl.BlockSpec((1, t_len, 256 * n_pairs), lambda b, p, i: (b, 0, p)),
                  pl.BlockSpec((1, t_len, LANES * n_pairs), lambda b, p, i: (b, 0, p))],
        out_specs=pl.BlockSpec((1, tq, LANES * n_pairs), lambda b, p, i: (b, i, p)),
        out_shape=jax.ShapeDtypeStruct((bsz, t_len, 512), BF16),
        scratch_shapes=[pltpu.VMEM((n_pairs, t_len, 2 * LANES), BF16)],
        compiler_params=_cparams(("parallel", "parallel", "arbitrary")),
        name="attn_b",
    )(q3, k3, v3)
    return out.reshape(bsz * t_len, 512)


def _log_sigmoid(z):
    return jnp.minimum(z, 0.0) - jnp.log(1.0 + jnp.exp(-jnp.abs(z)))


def _gla_kernel(q_ref, k_ref, v_ref, g_ref, dec_ref, wf_ref, wb_ref, bf_ref, bb_ref, gco_ref,
                o_ref, acc_ref, st_ref, *, tile):
    t_len = q_ref.shape[1]
    n_tiles = t_len // tile
    n_chunks = tile // C_CHUNK
    row = lax.broadcasted_iota(jnp.int32, (tile, tile), 0)
    col = lax.broadcasted_iota(jnp.int32, (tile, tile), 1)
    same = (row // C_CHUNK) == (col // C_CHUNK)
    row_chunk = lax.broadcasted_iota(jnp.int32, (tile, LANES), 0) // C_CHUNK
    keep_f = same & (col <= row)
    keep_b = same & (col >= row)
    lower = jnp.where(keep_f, 1.0, 0.0).astype(BF16)
    upper = jnp.where(keep_b, 1.0, 0.0).astype(BF16)
    q_scale = C_DK ** -0.5
    st_ref[...] = jnp.zeros_like(st_ref)

    n_heads = q_ref.shape[2] // LANES

    def tile_step(j, d, hd, z_all):
        forward = d == 0
        b_ref = bf_ref if forward else bb_ref
        cum, keep = (lower, keep_f) if forward else (upper, keep_b)
        rows = pl.ds(pl.multiple_of(j * tile, tile), tile)
        ls = slice(hd * LANES, (hd + 1) * LANES)
        si = d * n_heads + hd
        q = q_ref[0, rows, ls].astype(F32) * q_scale
        k = k_ref[0, rows, ls].astype(F32)
        vb = v_ref[0, rows, ls]
        la = _log_sigmoid(z_all[:, ls] + b_ref[:, ls]) * (1.0 / C_TAU)
        r = _dot(cum, jnp.concatenate(_split_bf16(la), axis=1))
        yield
        b = r[:, :LANES] + r[:, LANES:]
        b3 = b.reshape(n_chunks, C_CHUNK, LANES)
        edge = b3[:, C_CHUNK - 1:, :] if forward else b3[:, :1, :]
        tot = jnp.broadcast_to(edge, b3.shape).reshape(tile, LANES)
        qf = q * jnp.exp(b)
        qt = qf.astype(BF16)
        kt = (k * jnp.exp(-b)).astype(BF16)
        k2 = k * jnp.exp(tot - b)
        raw = _dot_nt(qt, kt)
        k2_blk = jnp.concatenate([jnp.where(row_chunk == c, k2, 0.0).astype(BF16) for c in range(n_chunks)], axis=1)
        kv = _dot_tn(vb, k2_blk)
        yield
        intra = _dot(jnp.where(keep, raw, 0.0).astype(BF16), vb)
        dec = jnp.exp(edge)
        q_blk = jnp.concatenate([jnp.where(row_chunk == c, qf, 0.0).astype(BF16) for c in range(n_chunks)], axis=1)
        st = st_ref[si]
        entering = [None] * n_chunks
        for c in (range(n_chunks) if forward else range(n_chunks - 1, -1, -1)):
            entering[c] = st.astype(BF16)
            st = st * dec[c] + kv[:, c * LANES:(c + 1) * LANES]
        st_ref[si] = st
        inter = _dot_nt(q_blk, jnp.concatenate(entering, axis=1))
        yield
        acc_ref[d, rows, ls] = intra + inter

    def body(it, carry):
        tiles = (it, n_tiles - 1 - it)
        z = [_dot(dec_ref[0, pl.ds(pl.multiple_of(tiles[d] * tile, tile), tile), :], (wf_ref, wb_ref)[d][...])
             for d in (0, 1)]
        _run_skewed([tile_step(tiles[d], d, hd, z[d]) for hd in range(n_heads) for d in (0, 1)])
        return carry

    lax.fori_loop(0, n_tiles, body, 0)
    for hd in range(n_heads):
        ls = slice(hd * LANES, (hd + 1) * LANES)
        y = _rms(acc_ref[0, :, ls] + acc_ref[1, :, ls], gco_ref[...])
        gate = g_ref[0, :, ls].astype(F32)
        o_ref[0, :, ls] = (y * (gate * _sigmoid(gate))).astype(o_ref.dtype)


def _gla(cqkvg, dec, wf, wb, bf, bb, gco, bsz, t_len, tile, hp=4):
    c3 = cqkvg.reshape(bsz, t_len, 4 * C_HEADS * LANES)
    d3 = dec.reshape(bsz, t_len, LANES)
    groups = C_HEADS // hp
    wl = hp * LANES

    def cspec(j):
        return pl.BlockSpec((1, t_len, wl), lambda b, h: (b, 0, j * groups + h))

    out = pl.pallas_call(
        functools.partial(_gla_kernel, tile=tile),
        grid=(bsz, groups),
        in_specs=[cspec(0), cspec(1), cspec(2), cspec(3),
                  pl.BlockSpec((1, t_len, LANES), lambda b, h: (b, 0, 0)),
                  pl.BlockSpec((LANES, wl), lambda b, h: (0, h)),
                  pl.BlockSpec((LANES, wl), lambda b, h: (0, h)),
                  pl.BlockSpec((1, wl), lambda b, h: (0, h)),
                  pl.BlockSpec((1, wl), lambda b, h: (0, h)),
                  pl.BlockSpec((1, LANES), lambda b, h: (0, 0))],
        out_specs=pl.BlockSpec((1, t_len, wl), lambda b, h: (b, 0, h)),
        out_shape=jax.ShapeDtypeStruct((bsz, t_len, C_HEADS * LANES), BF16),
        scratch_shapes=[pltpu.VMEM((2, t_len, wl), F32), pltpu.VMEM((2 * hp, C_DV, C_DK), F32)],
        compiler_params=_cparams(("parallel", "parallel")),
        name="gla",
    )(c3, c3, c3, c3, d3, wf, wb, bf, bb, gco)
    return out.reshape(bsz * t_len, C_HEADS * LANES)


def _merge_stages(x, rows, gmix_ref, wgate_ref, bg_ref, y_refs, wp_refs, wo_ref, out):
    h = _rms(x, gmix_ref[...]).astype(BF16)
    dots = []
    for j in range(N_BRANCH):
        sl = slice(j * D_MODEL, (j + 1) * D_MODEL)
        dots.append((_dot(h, wgate_ref[:, sl]), _dot(y_refs[j][rows, :], wp_refs[j][...])))
    yield
    m = None
    for j, (gate, proj) in enumerate(dots):
        term = _sigmoid(gate + bg_ref[:, j * D_MODEL:(j + 1) * D_MODEL]) * proj
        m = term if m is None else m + term
    y = _dot(m.astype(BF16), wo_ref[...])
    yield
    out.append(x + y)


def _merge_tile(x, gmix_ref, wgate_ref, bg_ref, y_refs, wp_refs, wo_ref):
    out = []
    for _ in _merge_stages(x, slice(None), gmix_ref, wgate_ref, bg_ref, y_refs, wp_refs, wo_ref, out):
        pass
    return out[0]


def _merge_specs(tm):
    tok = lambda w: pl.BlockSpec((tm, w), lambda i: (i, 0))
    return [tok(512), tok(512), tok(512),
            _const_spec((1, D_MODEL)),
            _const_spec((D_MODEL, N_BRANCH * D_MODEL)),
            _const_spec((1, N_BRANCH * D_MODEL)),
            _const_spec((512, D_MODEL)), _const_spec((512, D_MODEL)), _const_spec((512, D_MODEL)),
            _const_spec((D_MODEL, D_MODEL))]


def _merge_ffn_kernel(x_ref, ya_ref, yb_ref, yc_ref, gmix_ref, wgate_ref, bg_ref, wpa_ref, wpb_ref, wpc_ref, wo_ref,
                      g_ref, wg_ref, wu_ref, wd_ref, o_ref, *, chunk):
    x = _merge_tile(x_ref[...], gmix_ref, wgate_ref, bg_ref, (ya_ref, yb_ref, yc_ref),
                    (wpa_ref, wpb_ref, wpc_ref), wo_ref)
    h = _rms(x, g_ref[...]).astype(BF16)
    o_ref[...] = x
    for c0 in range(0, D_FF, chunk):
        gt = _dot(h, wg_ref[:, c0:c0 + chunk])
        up = _dot(h, wu_ref[:, c0:c0 + chunk])
        a = (gt * _sigmoid(gt) * up).astype(BF16)
        o_ref[...] += _dot(a, wd_ref[c0:c0 + chunk, :])


def _merge_ffn(x2, merge_args, g, wg, wu, wd, tm, chunk=256):
    n = x2.shape[0]
    return pl.pallas_call(
        functools.partial(_merge_ffn_kernel, chunk=chunk),
        grid=(n // tm,),
        in_specs=[pl.BlockSpec((tm, D_MODEL), lambda i: (i, 0))] + _merge_specs(tm)
                 + [_const_spec((1, D_MODEL)),
                    _const_spec((D_MODEL, D_FF)), _const_spec((D_MODEL, D_FF)), _const_spec((D_FF, D_MODEL))],
        out_specs=pl.BlockSpec((tm, D_MODEL), lambda i: (i, 0)),
        out_shape=jax.ShapeDtypeStruct((n, D_MODEL), F32),
        compiler_params=_cparams(("parallel",)),
        name="merge_ffn",
    )(x2, *merge_args, g, wg, wu, wd)


def _merge_router_kernel(x_ref, ya_ref, yb_ref, yc_ref, gmix_ref, wgate_ref, bg_ref, wpa_ref, wpb_ref, wpc_ref, wo_ref,
                         g_ref, wr_ref, xo_ref, h_ref, gate_ref, rank_ref, rank_t_ref, cnt_ref, tot_ref, carry_ref,
                         *, tb):
    tm = x_ref.shape[0]

    @pl.when(pl.program_id(0) == 0)
    def _():
        carry_ref[...] = jnp.zeros_like(carry_ref)

    lane = lax.broadcasted_iota(jnp.int32, (tb, LANES), 1).astype(F32)
    row = lax.broadcasted_iota(jnp.int32, (tb, tb), 0)
    col = lax.broadcasted_iota(jnp.int32, (tb, tb), 1)
    earlier = jnp.where(col < row, 1.0, 0.0).astype(BF16)
    neg = jnp.float32(-jnp.inf)
    w_hi, w_lo = _split_bf16(wr_ref[...])
    w_cat = jnp.concatenate([w_hi, w_lo], axis=1)
    local = []

    def block(u):
        rows = slice(u * tb, (u + 1) * tb)
        merged = []
        yield from _merge_stages(x_ref[rows, :], rows, gmix_ref, wgate_ref, bg_ref, (ya_ref, yb_ref, yc_ref),
                                 (wpa_ref, wpb_ref, wpc_ref), wo_ref, merged)
        x = merged[0]
        xo_ref[rows, :] = x
        hf = _rms(x, g_ref[...])
        h_ref[rows, :] = hf.astype(h_ref.dtype)
        hf_hi, hf_lo = _split_bf16(hf)
        wide = _dot(hf_hi, w_cat)
        logits = wide[:, :LANES] + (wide[:, LANES:] + _dot(hf_lo, w_hi))
        yield
        logits = jnp.where(lane < N_EXPERTS, logits, neg)
        v1 = jnp.max(logits, axis=-1, keepdims=True)
        i1 = jnp.min(jnp.where(logits == v1, lane, float(LANES)), axis=-1, keepdims=True)
        rest = jnp.where(lane == i1, neg, logits)
        v2 = jnp.max(rest, axis=-1, keepdims=True)
        i2 = jnp.min(jnp.where(rest == v2, lane, float(LANES)), axis=-1, keepdims=True)
        e2 = jnp.exp(v2 - v1)
        w1 = 1.0 / (1.0 + e2)
        w2 = e2 / (1.0 + e2)
        gate_ref[rows, :] = jnp.where(lane == i1, w1, 0.0) + jnp.where(lane == i2, w2, 0.0)
        chosen = (lane == i1) | (lane == i2)
        assign = jnp.where(chosen, 1.0, 0.0)
        inside = _dot(earlier, assign.astype(BF16))
        yield
        local.append((chosen, inside, jnp.sum(assign, axis=0, keepdims=True)))

    _run_skewed([block(u) for u in range(tm // tb)])
    carry = carry_ref[...]
    for u, (chosen, inside, count) in enumerate(local):
        rows = slice(u * tb, (u + 1) * tb)
        rank = jnp.where(chosen, inside + carry[0:1, :], -1.0)
        rank_ref[rows, :] = rank
        rank_t_ref[:, rows] = rank.T[:N_EXPERTS, :]
        cnt_ref[u] = carry
        carry = carry + count
    carry_ref[...] = carry
    tot_ref[...] = carry


def _merge_router(x2, merge_args, g, wr_pad, tm, tb):
    n = x2.shape[0]
    nt = n // tm
    return pl.pallas_call(
        functools.partial(_merge_router_kernel, tb=tb),
        grid=(nt,),
        in_specs=[pl.BlockSpec((tm, D_MODEL), lambda i: (i, 0))] + _merge_specs(tm)
                 + [_const_spec((1, D_MODEL)),
                    _const_spec((D_MODEL, LANES))],
        out_specs=[pl.BlockSpec((tm, D_MODEL), lambda i: (i, 0)),
                   pl.BlockSpec((tm, D_MODEL), lambda i: (i, 0)),
                   pl.BlockSpec((tm, LANES), lambda i: (i, 0)),
                   pl.BlockSpec((tm, LANES), lambda i: (i, 0)),
                   pl.BlockSpec((N_EXPERTS, tm), lambda i: (0, i)),
                   pl.BlockSpec((tm // tb, 8, LANES), lambda i: (i, 0, 0)),
                   pl.BlockSpec((8, LANES), lambda i: (0, 0))],
        out_shape=[jax.ShapeDtypeStruct((n, D_MODEL), F32),
                   jax.ShapeDtypeStruct((n, D_MODEL), BF16),
                   jax.ShapeDtypeStruct((n, LANES), F32),
                   jax.ShapeDtypeStruct((n, LANES), F32),
                   jax.ShapeDtypeStruct((N_EXPERTS, n), F32),
                   jax.ShapeDtypeStruct((n // tb, 8, LANES), F32),
                   jax.ShapeDtypeStruct((8, LANES), F32)],
        scratch_shapes=[pltpu.VMEM((8, LANES), F32)],
        compiler_params=_cparams(("arbitrary",)),
        name="merge_router",
    )(x2, *merge_args, g, wr_pad)


def _moe_plan(cnt, tot, n, tb, rs, tg):
    i32 = jnp.int32
    counts = tot[0, :N_EXPERTS].astype(i32)
    cum = jnp.concatenate([cnt[:, 0, :N_EXPERTS], tot[:1, :N_EXPERTS]], axis=0).astype(i32)
    n_sb = (counts + rs - 1) // rs
    sb_end = jnp.cumsum(n_sb)
    sb_start = sb_end - n_sb
    goff = sb_start * rs
    n_valid_sb = sb_end[-1]
    s_max = 2 * n // rs + N_EXPERTS
    s_ids = jnp.arange(s_max, dtype=i32)
    last_sb = n_valid_sb - 1
    experts = jnp.arange(N_EXPERTS, dtype=i32)

    def owner(sb):
        e = jnp.minimum(jnp.sum(sb[:, None] >= sb_end[None, :], axis=1), N_EXPERTS - 1).astype(i32)
        return e, e[:, None] == experts[None, :]

    pick = lambda onehot, table: jnp.sum(jnp.where(onehot, table[None, :], 0), axis=1)
    s_eff = jnp.minimum(s_ids, last_sb)
    sb_expert, sb_hot = owner(s_eff)
    sb_rows = jnp.clip(pick(sb_hot, counts) - (s_eff - pick(sb_hot, sb_start)) * rs, 0, rs)
    sb_tiles = ((sb_rows + tb - 1) // tb).astype(i32)

    t_ids = jnp.arange(s_max * rs // tb, dtype=i32)
    t_sb = t_ids * tb // rs
    t_exp, t_hot = owner(jnp.minimum(t_sb, last_sb))
    t_count = pick(t_hot, counts)
    t_r0 = t_ids * tb - pick(t_hot, goff)
    t_valid = (t_sb <= last_sb) & (t_r0 < t_count)
    t_r1 = jnp.minimum(t_r0 + tb, t_count)
    cum_e = jnp.sum(jnp.where(t_hot[None, :, :], cum[:, None, :], 0), axis=2)
    t_lo = jnp.sum(cum_e[1:] <= t_r0[None, :], axis=0)
    t_hi = jnp.where(t_valid, jnp.sum(cum_e[:-1] < t_r1[None, :], axis=0), t_lo)
    gather_tabs = tuple(a.astype(i32) for a in (t_exp, t_r0, t_lo, t_hi))

    start = goff[None, :] + cum[:-1]
    num = cum[1:] - cum[:-1]
    n_rows = s_max * rs
    off0 = jnp.minimum(start // ROW_ALIGN * ROW_ALIGN, n_rows - tb)
    need1 = (num > 0) & (start + num > off0 + tb)
    off1 = jnp.where(need1, jnp.minimum(off0 + tb, n_rows - tb), 0)
    offs = jnp.stack([off0, off1], axis=-1).reshape(-1).astype(i32)
    expert_tabs = (sb_expert, sb_tiles, last_sb.reshape(1).astype(i32))
    return gather_tabs, expert_tabs, (offs, need1.reshape(-1).astype(i32), goff.astype(i32))


def _gather_kernel(exp_ref, r0_ref, lo_ref, hi_ref, rank_ref, h_ref, o_ref, acc_ref, *, tb):
    step = pl.program_id(0)
    n_sub = o_ref.shape[0] // tb
    n_blk = rank_ref.shape[1]
    row = lax.broadcasted_iota(jnp.int32, (tb, tb), 0)
    tiles = [step * n_sub + u for u in range(n_sub)]
    n_trips = functools.reduce(jnp.maximum, [(hi_ref[t] - lo_ref[t] + 1) >> 1 for t in tiles])

    @pl.when(n_trips == 0)
    def _():
        o_ref[...] = jnp.zeros_like(o_ref)

    @pl.when(n_trips > 0)
    def _():
        acc_ref[...] = jnp.zeros_like(acc_ref)

        def body(p, carry):
            parts = []
            for t in tiles:
                e, lo, hi = exp_ref[t], lo_ref[t], hi_ref[t]
                want = row + r0_ref[t]
                for j in range(2):
                    kb = lo + 2 * p + j
                    kbc = jnp.minimum(kb, n_blk - 1)
                    rk = rank_ref[e, pl.ds(kbc, 1), :].astype(jnp.int32)
                    onehot = jnp.where(jnp.where(kb < hi, rk, -1) == want, 1.0, 0.0).astype(BF16)
                    parts.append(_dot(onehot, h_ref[pl.ds(pl.multiple_of(kbc * tb, tb), tb), :]))
            for u in range(n_sub):
                acc_ref[u * tb:(u + 1) * tb, :] += parts[2 * u] + parts[2 * u + 1]
            return carry

        lax.fori_loop(0, n_trips, body, 0)
        o_ref[...] = acc_ref[...].astype(o_ref.dtype)


def _gather(tabs, rank_t3, h, n_rows, tg, tb):
    n = h.shape[0]
    return pl.pallas_call(
        functools.partial(_gather_kernel, tb=tb),
        grid_spec=pltpu.PrefetchScalarGridSpec(
            num_scalar_prefetch=4,
            grid=(n_rows // tg,),
            in_specs=[_const_spec((N_EXPERTS, n // tb, tb)),
                      _const_spec((n, D_MODEL))],
            out_specs=pl.BlockSpec((tg, D_MODEL), lambda t, e, r, lo, hi: (t, 0)),
            scratch_shapes=[pltpu.VMEM((tg, D_MODEL), F32)]),
        out_shape=jax.ShapeDtypeStruct((n_rows, D_MODEL), BF16),
        compiler_params=_cparams(("arbitrary",)),
        name="moe_gather",
    )(*tabs, rank_t3, h)


def _experts_kernel(exp_ref, tiles_ref, last_ref, xs_ref, wg_ref, wu_ref, wd_ref, o_ref,
                    acc_ref, wgb_ref, wub_ref, wdb_ref, *, tb):
    s, c = pl.program_id(0), pl.program_id(1)
    nc = pl.num_programs(1)

    @pl.when((s > last_ref[0]) & (c == 0))
    def _():
        o_ref[...] = jnp.zeros_like(o_ref)

    @pl.when(s <= last_ref[0])
    def _():
        @pl.when(c == 0)
        def _():
            acc_ref[...] = jnp.zeros_like(acc_ref)

        wgb_ref[...] = wg_ref[0].astype(BF16)
        wub_ref[...] = wu_ref[0].astype(BF16)
        wdb_ref[...] = wd_ref[0].astype(BF16)

        def tile_step(t):
            rows = pl.ds(pl.multiple_of(t * tb, tb), tb)
            x = xs_ref[rows, :]
            gt = _dot(x, wgb_ref[...])
            up = _dot(x, wub_ref[...])
            yield
            a = (gt * _sigmoid(gt) * up).astype(BF16)
            y = _dot(a, wdb_ref[...])
            yield
            acc_ref[rows, :] += y

        def body(t4, carry):
            _run_skewed([tile_step(4 * t4 + u) for u in range(4)])
            return carry

        n_tiles = tiles_ref[s]
        lax.fori_loop(0, n_tiles >> 2, body, 0)
        done = n_tiles & ~3

        @pl.when((n_tiles & 2) == 2)
        def _():
            _run_skewed([tile_step(done), tile_step(done + 1)])

        @pl.when((n_tiles & 1) == 1)
        def _():
            _run_skewed([tile_step(n_tiles - 1)])

        @pl.when(c == nc - 1)
        def _():
            o_ref[...] = acc_ref[...].astype(o_ref.dtype)


def _experts(tabs, xs, wg, wu, wd, rs, tb, fc):
    n_rows = xs.shape[0]
    nc = D_FF_EXPERT // fc

    def sb(s, last):
        return jnp.minimum(s, last[0])

    def chunk(s, c, last):
        return jnp.where(s <= last[0], c, nc - 1)

    return pl.pallas_call(
        functools.partial(_experts_kernel, tb=tb),
        grid_spec=pltpu.PrefetchScalarGridSpec(
            num_scalar_prefetch=3,
            grid=(n_rows // rs, nc),
            in_specs=[pl.BlockSpec((rs, D_MODEL), lambda s, c, ex, tl, last: (sb(s, last), 0)),
                      pl.BlockSpec((1, D_MODEL, fc), lambda s, c, ex, tl, last: (ex[sb(s, last)], 0, chunk(s, c, last))),
                      pl.BlockSpec((1, D_MODEL, fc), lambda s, c, ex, tl, last: (ex[sb(s, last)], 0, chunk(s, c, last))),
                      pl.BlockSpec((1, fc, D_MODEL), lambda s, c, ex, tl, last: (ex[sb(s, last)], chunk(s, c, last), 0))],
            out_specs=pl.BlockSpec((rs, D_MODEL), lambda s, c, ex, tl, last: (s, 0)),
            scratch_shapes=[pltpu.VMEM((rs, D_MODEL), F32),
                            pltpu.VMEM((D_MODEL, fc), BF16), pltpu.VMEM((D_MODEL, fc), BF16),
                            pltpu.VMEM((fc, D_MODEL), BF16)]),
        out_shape=jax.ShapeDtypeStruct((n_rows, D_MODEL), BF16),
        compiler_params=_cparams(("arbitrary", "arbitrary")),
        name="moe_experts",
    )(*tabs, xs, wg, wu, wd)


def _combine_kernel(off_ref, need_ref, goff_ref, x_ref, gate_ref, rank_ref, *rest, final_norm):
    n_win = len(rest) - 2
    y_refs, (gfin_ref, o_ref) = rest[:n_win], rest[n_win:]
    tb = y_refs[0].shape[0]
    n_tok = x_ref.shape[0] // tb
    lane = lax.broadcasted_iota(jnp.int32, (tb, LANES), 1)
    col = lax.broadcasted_iota(jnp.int32, (tb, tb), 1)

    for u in range(n_tok):
        t = pl.program_id(0) * n_tok + u
        rows = slice(u * tb, (u + 1) * tb)
        gate, rank = gate_ref[rows, :], rank_ref[rows, :]

        def expert_terms(e, gate=gate, rank=rank):
            ge = jnp.sum(jnp.where(lane == e, gate, 0.0), axis=-1, keepdims=True)
            rk = jnp.sum(jnp.where(lane == e, rank, 0.0), axis=-1, keepdims=True).astype(jnp.int32)
            return ge, jnp.where(rk >= 0, rk + goff_ref[e], -1)

        def window(e, j, pos, t=t, u=u):
            onehot = jnp.where(pos - off_ref[(t * N_EXPERTS + e) * 2 + j] == col, 1.0, 0.0).astype(BF16)
            return _dot(onehot, y_refs[(u * N_EXPERTS + e) * 2 + j][...])

        y = x_ref[rows, :]
        for e in range(N_EXPERTS):
            ge, pos = expert_terms(e)
            y = y + ge * window(e, 0, pos)
        o_ref[rows, :] = y
        for e in range(N_EXPERTS):
            @pl.when(need_ref[t * N_EXPERTS + e] == 1)
            def _(e=e, rows=rows, expert_terms=expert_terms, window=window):
                ge, pos = expert_terms(e)
                o_ref[rows, :] += ge * window(e, 1, pos)
    if final_norm:
        o_ref[...] = _rms(o_ref[...], gfin_ref[...])


def _combine(tabs, x2, gate, rank, ys, gfin, tt, tb, final_norm):
    n = x2.shape[0]
    n_win = tt // tb * 2 * N_EXPERTS

    def yspec(slot):
        return pl.BlockSpec((pl.Element(tb), pl.Element(D_MODEL)),
                            lambda i, off, need, goff: (pl.multiple_of(off[i * n_win + slot], ROW_ALIGN), 0))

    tok = lambda w: pl.BlockSpec((tt, w), lambda i, off, need, goff: (i, 0))
    return pl.pallas_call(
        functools.partial(_combine_kernel, final_norm=final_norm),
        grid_spec=pltpu.PrefetchScalarGridSpec(
            num_scalar_prefetch=3,
            grid=(n // tt,),
            in_specs=[tok(D_MODEL), tok(LANES), tok(LANES)] + [yspec(k) for k in range(n_win)]
                     + [pl.BlockSpec((1, D_MODEL), lambda i, off, need, goff: (0, 0))],
            out_specs=tok(D_MODEL)),
        out_shape=jax.ShapeDtypeStruct((n, D_MODEL), F32),
        compiler_params=_cparams(("arbitrary",)),
        name="moe_combine",
    )(*tabs, x2, gate, rank, *([ys] * n_win), gfin)


def _merge_moe(x2, merge_args, g_ffn, w_router, wg, wu, wd, gfin, final_norm):
    n = x2.shape[0]
    tb = min(256, n // 2)
    rs = max(9 * n // (8 * N_EXPERTS) // tb * tb, 2 * tb)
    tg = next(k * tb for k in (3, 2, 1) if rs % (k * tb) == 0)
    wr = jnp.pad(w_router, ((0, 0), (0, LANES - N_EXPERTS)))
    x2, h, gate, rank, rank_t, cnt, tot = _merge_router(x2, merge_args, g_ffn, wr, min(2 * tb, n), tb)
    gather_tabs, expert_tabs, combine_tabs = _moe_plan(cnt, tot, n, tb, rs, tg)
    n_rows = (2 * n // rs + N_EXPERTS) * rs
    xs = _gather(gather_tabs, rank_t.reshape(N_EXPERTS, n // tb, tb), h, n_rows, tg, tb)
    ys = _experts(expert_tabs, xs, wg, wu, wd, rs, tb, 512)
    return _combine(combine_tabs, x2, gate, rank, ys, gfin, min(2 * tb, n), tb, final_norm)


def _final_norm_kernel(x_ref, g_ref, o_ref):
    o_ref[...] = _rms(x_ref[...], g_ref[...])


def _final_norm(x2, g, tm):
    n = x2.shape[0]
    return pl.pallas_call(
        _final_norm_kernel,
        grid=(n // tm,),
        in_specs=[pl.BlockSpec((tm, D_MODEL), lambda i: (i, 0)), _const_spec((1, D_MODEL))],
        out_specs=pl.BlockSpec((tm, D_MODEL), lambda i: (i, 0)),
        out_shape=jax.ShapeDtypeStruct((n, D_MODEL), F32),
        compiler_params=_cparams(("parallel",)),
        name="final_norm",
    )(x2, g)


def _rope_tables(t_len):
    rows = t_len // GRID_W
    row = np.repeat(np.arange(rows, dtype=np.float64), GRID_W)
    col = np.tile(np.arange(GRID_W, dtype=np.float64), rows)
    lane = np.arange(LANES)

    def table(rot_dim, lane_in_slice, active):
        n_freq = rot_dim // 4
        inv_freq = ROPE_THETA ** (-np.arange(n_freq, dtype=np.float64) / n_freq)
        freq = inv_freq[lane_in_slice % n_freq]
        use_row = (lane_in_slice % rot_dim) < (rot_dim // 2)
        ang = np.where(use_row[None, :], row[:, None], col[:, None]) * freq[None, :]
        sign = np.where((lane_in_slice % (rot_dim // 2)) < n_freq, -1.0, 1.0)
        cos = np.where(active[None, :], np.cos(ang), 1.0)
        sin = np.where(active[None, :], np.sin(ang) * sign[None, :], 0.0)
        return jnp.asarray(cos, F32), jnp.asarray(sin, F32)

    cos_a, sin_a = table(A_HEAD_DIM, lane % A_HEAD_DIM, np.ones(LANES, bool))
    in_rope = (lane >= B_NOPE_DIM) & (lane < B_NOPE_DIM + B_ROPE_DIM)
    cos_b, sin_b = table(B_ROPE_DIM, (lane - B_NOPE_DIM) % B_ROPE_DIM, in_rope)
    return cos_a, sin_a, cos_b, sin_b


def _pack_w_in(w):
    o_bc, o_kr, o_c, o_dec, o_gate = 768, 1280, 1312, 3360, 3392
    zeros = lambda k: jnp.zeros((D_MODEL, k), BF16)
    piece = lambda a, b: w[:, a:b].astype(BF16)
    cols = [piece(0, o_bc), piece(o_kr, o_c), zeros(LANES - B_ROPE_DIM), piece(o_dec, o_gate),
            zeros(LANES - 2 * C_GATE_RANK), piece(o_bc, o_kr), piece(o_c, o_dec)]
    return jnp.concatenate(cols, axis=1), piece(o_gate, w.shape[1])


def _pack_b_weights(w_q_up, w_kv_up):
    hq = w_q_up.reshape(B_Q_RANK, B_HEADS, B_NOPE_DIM + B_ROPE_DIM)
    wq = jnp.pad(hq, ((0, 0), (0, 0), (0, LANES - B_NOPE_DIM - B_ROPE_DIM))).reshape(B_Q_RANK, B_HEADS * LANES)
    hkv = w_kv_up.reshape(B_KV_RANK, B_HEADS, B_NOPE_DIM + B_V_DIM)
    wk = jnp.pad(hkv[:, :, :B_NOPE_DIM], ((0, 0), (0, 0), (0, LANES - B_NOPE_DIM))).reshape(B_KV_RANK, B_HEADS * LANES)
    wv = hkv[:, :, B_NOPE_DIM:].reshape(B_KV_RANK, B_HEADS * B_V_DIM)
    return wq.astype(BF16), wk.astype(BF16), wv.astype(BF16)


def _rope_placement():
    e = np.zeros((LANES, LANES), np.float32)
    e[np.arange(B_ROPE_DIM), B_NOPE_DIM + np.arange(B_ROPE_DIM)] = 1.0
    return jnp.asarray(e, BF16)


def kernel(x, w_in, b_gate, g_mix, g_a_q, g_a_k, g_b_q, w_b_q_up, g_b_kv, w_b_kv_up, w_c_af_up, b_c_af, w_c_ab_up, b_c_ab, g_c_out, w_pa, w_pb, w_pc, w_out, g_ffn, w_ff_gate, w_ff_up, w_ff_down, w_router, w_e_gate, w_e_up, w_e_down, g_final):
    bsz, t_len, _ = x.shape
    n = bsz * t_len
    depth = w_in.shape[0]
    tm = min(512, t_len)
    tq_a = min(512, t_len)
    tq_b = min(512, t_len)
    gla_tile = min(256, t_len)

    cos_a, sin_a, cos_b, sin_b = _rope_tables(t_len)
    e_mat = _rope_placement()
    row = lambda v: v.reshape(1, -1).astype(F32)
    x2 = x.reshape(n, D_MODEL)

    for i in range(depth):
        wq, wk, wv = _pack_b_weights(w_b_q_up[i], w_b_kv_up[i])
        w_packed, w_gate = _pack_w_in(w_in[i])
        qa, ka, av2, qb, kb, vb, cqkvg, dec = _fused_in(
            x2, row(g_mix[i]), w_packed, row(jnp.tile(g_a_q[i], 2)), row(jnp.tile(g_a_k[i], 2)),
            cos_a, sin_a, row(g_b_q[i]), row(g_b_kv[i]), wq, wk, wv, e_mat, cos_b, sin_b, tm, t_len)
        ya = _attn_a(qa, ka, av2, bsz, t_len, tq_a)
        yb = _attn_b(qb, kb, vb, bsz, t_len, tq_b)

        wf = jnp.pad(w_c_af_up[i], ((0, LANES - C_GATE_RANK), (0, 0))).astype(BF16)
        wb = jnp.pad(w_c_ab_up[i], ((C_GATE_RANK, LANES - 2 * C_GATE_RANK), (0, 0))).astype(BF16)
        yc = _gla(cqkvg, dec, wf, wb, row(b_c_af[i]), row(b_c_ab[i]), row(g_c_out[i]), bsz, t_len, gla_tile)

        merge_args = (ya, yb, yc, row(g_mix[i]), w_gate, row(b_gate[i]), w_pa[i].astype(BF16), w_pb[i].astype(BF16),
                      w_pc[i].astype(BF16), w_out[i].astype(BF16))
        j = i // 2
        last = i == depth - 1
        if i % 2 == 0:
            x2 = _merge_ffn(x2, merge_args, row(g_ffn[i]), w_ff_gate[j].astype(BF16), w_ff_up[j].astype(BF16),
                            w_ff_down[j].astype(BF16), tm)
            if last:
                x2 = _final_norm(x2, row(g_final), tm)
        else:
            x2 = _merge_moe(x2, merge_args, row(g_ffn[i]), w_router[j], w_e_gate[j], w_e_up[j], w_e_down[j],
                            row(g_final), last)
    return x2.reshape(bsz, t_len, D_MODEL)
```
